```python
import jax, jax.numpy as jnp
from jax import lax
import numpy as np

D_MODEL = 1024
BATCH = 2
SEQ = 8192
DEPTH = 1
DEC_BATCH = 32
DEC_SEQ = 4
PAST_LEN = 16384
PAGE_SIZE = 128

A_WIDTH = D_MODEL // 2
A_GROUPS = 8
A_GDIM = A_WIDTH // A_GROUPS
CHUNK = 128
N_HEADS = 8
HEAD_DIM = 64
N_KV = 2
GROUP = N_HEADS // N_KV
B_WIDTH = N_HEADS * HEAD_DIM
KV_WIDTH = N_KV * HEAD_DIM
CMP_BLOCK = 32
CMP_STRIDE = 16
SEL_BLOCK = 64
N_SELECT = 16
WINDOW = 512
ROT_DIM = HEAD_DIM // 4
ROPE_THETA = 500000.0
D_FF = 4 * D_MODEL
IN_COLS = 2 * A_WIDTH + B_WIDTH + 6 * KV_WIDTH + 3 * N_HEADS + 2 * D_MODEL
EPS = 1e-6
Q_BLOCK = 128
NEG = -1e30
FORCED_BONUS = 1e4

kernel_name = 'hybrid_gmlp_nsa_decode_step'


def rms_norm(x, g):
    xf = x.astype(jnp.float32)
    y = xf * lax.rsqrt(jnp.mean(xf * xf, axis=-1, keepdims=True) + EPS)
    return (y * g.astype(jnp.float32)).astype(x.dtype)


def layer_norm(x, g, b):
    xf = x.astype(jnp.float32)
    xc = xf - jnp.mean(xf, axis=-1, keepdims=True)
    y = xc * lax.rsqrt(jnp.mean(xc * xc, axis=-1, keepdims=True) + EPS)
    return (y * g.astype(jnp.float32) + b.astype(jnp.float32)).astype(x.dtype)


def rope(x, pos):
    inv = ROPE_THETA ** (-jnp.arange(0, ROT_DIM, 2, dtype=jnp.float32) / ROT_DIM)
    ang = pos.astype(jnp.float32)[:, None] * inv[None, :]
    cos = jnp.cos(ang)[None, :, None, :]
    sin = jnp.sin(ang)[None, :, None, :]
    xr = x[..., :ROT_DIM].astype(jnp.float32)
    x1, x2 = xr[..., :ROT_DIM // 2], xr[..., ROT_DIM // 2:]
    rot = jnp.concatenate([x1 * cos - x2 * sin, x2 * cos + x1 * sin], axis=-1)
    return jnp.concatenate([rot.astype(x.dtype), x[..., ROT_DIM:]], axis=-1)


def masked_softmax(s, mask):
    s = jnp.where(mask, s.astype(jnp.float32), NEG)
    return jax.nn.softmax(s, axis=-1) * mask


def in_projection(x, pos, g_norm1, w_in, ln_v_g, ln_v_b, g_q, g_ks, g_kw):
    B, T, _ = x.shape
    h = rms_norm(x, g_norm1)
    z = h @ w_in
    widths = [A_WIDTH, A_WIDTH, B_WIDTH] + [KV_WIDTH] * 6 + [3 * N_HEADS]
    points, acc = [], 0
    for w in widths:
        acc += w
        points.append(acc)
    u, v, q, kc, vc, ks, vs, kw, vw, nsa_gate, merge_gate = jnp.split(z, points, axis=-1)
    u = jax.nn.gelu(u)
    v_n = layer_norm(jax.nn.gelu(v), ln_v_g, ln_v_b)
    heads = lambda a, n: a.reshape(B, T, n, HEAD_DIM)
    q_n = rms_norm(heads(q, N_HEADS), g_q)
    q_r = rope(q_n, pos)
    ks = rope(rms_norm(heads(ks, N_KV), g_ks), pos)
    kw = rope(rms_norm(heads(kw, N_KV), g_kw), pos)
    return (u, v_n, q_n, q_r, heads(kc, N_KV), heads(vc, N_KV), ks, heads(vs, N_KV),
            kw, heads(vw, N_KV), nsa_gate, merge_gate)


def gmlp_mix(u, v_n, w_s, b_s):
    B, T, _ = v_n.shape
    n_chunks = -(-T // CHUNK)
    pad = n_chunks * CHUNK - T
    vp = jnp.pad(v_n, ((0, 0), (0, pad), (0, 0))).reshape(B, n_chunks, CHUNK, A_GROUPS, A_GDIM)
    causal = jnp.tril(jnp.ones((CHUNK, CHUNK), dtype=bool))
    ws = jnp.where(causal[None], w_s, jnp.zeros_like(w_s))
    s = jnp.einsum('gpr,bnrgc->bnpgc', ws, vp) + b_s.T[None, None, :, :, None]
    return u * s.reshape(B, n_chunks * CHUNK, A_WIDTH)[:, :T]


def compress(rows, w1, w2, pe):
    B, T, KV, hd = rows.shape
    C = T // CMP_STRIDE
    half = CMP_STRIDE * hd
    ch = rows[:, :C * CMP_STRIDE].reshape(B, C, CMP_STRIDE, KV, hd).transpose(0, 1, 3, 2, 4).reshape(B, C, KV, half)
    first = ch @ w1[:half]
    second = ch @ w1[half:]
    bias = pe.reshape(-1) @ w1
    return jax.nn.gelu(first[:, :-1] + second[:, 1:] + bias) @ w2


def nsa_attend(q_n, q_r, gate, pos_q, kc, vc, fetch_sel, k_w, v_w, pos_w, n_total):
    B, Tq = q_n.shape[:2]
    scale = HEAD_DIM ** -0.5
    qn = q_n.reshape(B, Tq, N_KV, GROUP, HEAD_DIM)
    qr = q_r.reshape(B, Tq, N_KV, GROUP, HEAD_DIM)
    NC = kc.shape[1]
    cmp_end = jnp.arange(NC) * CMP_STRIDE + CMP_BLOCK - 1
    m_c = cmp_end[None, :] <= pos_q[:, None]
    p_c = masked_softmax(jnp.einsum('bqkgd,bckd->bkgqc', qn, kc) * scale, m_c)
    o_c = jnp.einsum('bkgqc,bckd->bqkgd', p_c.astype(vc.dtype), vc)
    imp = p_c.sum(axis=2)
    n_sel = -(-n_total // SEL_BLOCK)
    n_ch = n_sel * (SEL_BLOCK // CMP_STRIDE)
    chunk_score = (jnp.pad(imp, ((0, 0), (0, 0), (0, 0), (0, n_ch - NC)))
                   + jnp.pad(imp, ((0, 0), (0, 0), (0, 0), (1, n_ch - NC - 1))))
    blk_score = chunk_score.reshape(B, N_KV, Tq, n_sel, SEL_BLOCK // CMP_STRIDE).sum(-1)
    j = jnp.arange(n_sel)[None, :]
    cur = (pos_q // SEL_BLOCK)[:, None]
    valid_b = j * SEL_BLOCK <= pos_q[:, None]
    forced = (j == 0) | (j == cur) | (j == cur - 1)
    rank = jnp.where(forced, FORCED_BONUS, jnp.where(valid_b, blk_score, -FORCED_BONUS))
    n_top = min(N_SELECT, n_sel)
    _, idx = lax.top_k(rank, n_top)
    pq = pos_q[None, None, :, None, None]
    pos_s = idx[..., None] * SEL_BLOCK + jnp.arange(SEL_BLOCK)
    m_s = ((idx[..., None] * SEL_BLOCK <= pq) & (pos_s <= pq)).reshape(B, N_KV, Tq, n_top * SEL_BLOCK)
    k_s, v_s = fetch_sel(pos_s.reshape(B, N_KV, Tq, n_top * SEL_BLOCK))
    p_s = masked_softmax(jnp.einsum('bqkgd,bkqsd->bkgqs', qr, k_s) * scale, m_s[:, :, None])
    o_s = jnp.einsum('bkgqs,bkqsd->bqkgd', p_s.astype(v_s.dtype), v_s)
    dpos = pos_q[:, None] - pos_w[None, :]
    m_w = (dpos >= 0) & (dpos < WINDOW) & (pos_w[None, :] >= 0)
    p_w = masked_softmax(jnp.einsum('bqkgd,bskd->bkgqs', qr, k_w) * scale, m_w)
    o_w = jnp.einsum('bkgqs,bskd->bqkgd', p_w.astype(v_w.dtype), v_w)
    g = jax.nn.sigmoid(gate).reshape(B, Tq, N_KV, GROUP, 3)
    o = g[..., 0:1] * o_c + g[..., 1:2] * o_s + g[..., 2:3] * o_w
    return o.reshape(B, Tq, B_WIDTH)


def nsa_prompt(q_n, q_r, gate, kc, vc, k_sel, v_sel, k_win, v_win):
    B, T = q_n.shape[:2]
    kw_pad = jnp.pad(k_win, ((0, 0), (WINDOW, 0), (0, 0), (0, 0)))
    vw_pad = jnp.pad(v_win, ((0, 0), (WINDOW, 0), (0, 0), (0, 0)))
    bi = jnp.arange(B)[:, None, None, None]
    ki = jnp.arange(N_KV)[None, :, None, None]

    def fetch(pos):
        p = jnp.clip(pos, 0, T - 1)
        return k_sel[bi, p, ki], v_sel[bi, p, ki]

    def one_block(s):
        pos_q = s + jnp.arange(Q_BLOCK, dtype=jnp.int32)
        sl = lambda a, n: lax.dynamic_slice_in_dim(a, s, n, axis=1)
        pos_w = s - WINDOW + jnp.arange(Q_BLOCK + WINDOW, dtype=jnp.int32)
        return nsa_attend(sl(q_n, Q_BLOCK), sl(q_r, Q_BLOCK), sl(gate, Q_BLOCK), pos_q, kc, vc, fetch,
                          sl(kw_pad, Q_BLOCK + WINDOW), sl(vw_pad, Q_BLOCK + WINDOW), pos_w, T)

    starts = jnp.arange(T // Q_BLOCK, dtype=jnp.int32) * Q_BLOCK
    o = lax.map(one_block, starts)
    return o.transpose(1, 0, 2, 3).reshape(B, T, B_WIDTH)


def nsa_sample(q_n, q_r, gate, kc, vc, ks_new, vs_new, k_w, v_w, pos_w, sk_pool, sv_pool, page_table):
    Bd = q_n.shape[0]
    bi = jnp.arange(Bd)[:, None, None, None]
    ki = jnp.arange(N_KV)[None, :, None, None]

    def fetch(pos):
        past = (pos < PAST_LEN)[..., None]
        pc = jnp.clip(pos, 0, PAST_LEN - 1)
        phys = page_table[bi, pc // PAGE_SIZE]
        off = pc % PAGE_SIZE
        pn = jnp.clip(pos - PAST_LEN, 0, DEC_SEQ - 1)
        k = jnp.where(past, sk_pool[phys, off, ki], ks_new[bi, pn, ki])
        v = jnp.where(past, sv_pool[phys, off, ki], vs_new[bi, pn, ki])
        return k, v

    pos_q = PAST_LEN + jnp.arange(DEC_SEQ, dtype=jnp.int32)
    return nsa_attend(q_n, q_r, gate, pos_q, kc, vc, fetch, k_w, v_w, pos_w, PAST_LEN + DEC_SEQ)


def gather_pages(pool, page_table):
    Bd, n_pages = page_table.shape
    return pool[page_table].reshape(Bd, n_pages * PAGE_SIZE, N_KV, HEAD_DIM)


def merge_and_ffn(x, o_a, o_b, merge_gate, w_branch, w_out, g_norm2, w_up, w_down):
    y_a = o_a @ w_branch[:A_WIDTH]
    y_b = o_b @ w_branch[A_WIDTH:]
    g_a, g_b = jnp.split(merge_gate, 2, axis=-1)
    x = x + (jax.nn.sigmoid(g_a) * y_a + jax.nn.sigmoid(g_b) * y_b) @ w_out
    h = rms_norm(x, g_norm2)
    return x + jnp.square(jax.nn.relu(h @ w_up)) @ w_down


def setup_inputs(seed: int = 0) -> dict:
    key = jax.random.key(seed)
    k = jax.random.split(key, 40)
    nrm = lambda kk, shape, scale: jax.random.normal(kk, shape, jnp.float32) * scale
    gain = lambda kk, n: 1.0 + nrm(kk, (DEPTH, n), 0.05)
    n_pages = PAST_LEN // PAGE_SIZE
    n_phys = (5 * DEC_BATCH * n_pages) // 4
    w_buf = min(WINDOW, PAST_LEN)
    pool = (DEPTH, n_phys, PAGE_SIZE, N_KV, HEAD_DIM)
    perm = jax.random.permutation(k[9], n_phys)
    page_table = perm[:DEC_BATCH * n_pages].reshape(DEC_BATCH, n_pages).astype(jnp.int32)
    return {
        'x_prompt': nrm(k[0], (BATCH, SEQ, D_MODEL), 1.0),
        'x_sample': nrm(k[1], (DEC_BATCH, DEC_SEQ, D_MODEL), 1.0),
        'cache_k_cmp': nrm(k[2], pool, 1.0),
        'cache_v_cmp': nrm(k[3], pool, 1.0),
        'cache_k_sel': nrm(k[4], pool, 1.0),
        'cache_v_sel': nrm(k[5], pool, 1.0),
        'cache_k_win': nrm(k[6], (DEPTH, DEC_BATCH, w_buf, N_KV, HEAD_DIM), 1.0),
        'cache_v_win': nrm(k[7], (DEPTH, DEC_BATCH, w_buf, N_KV, HEAD_DIM), 1.0),
        'page_table': page_table,
        'g_norm1': gain(k[10], D_MODEL),
        'w_in': nrm(k[11], (DEPTH, D_MODEL, IN_COLS), D_MODEL ** -0.5),
        'ln_v_g': gain(k[12], A_WIDTH),
        'ln_v_b': nrm(k[13], (DEPTH, A_WIDTH), 0.02),
        'w_s': nrm(k[14], (DEPTH, A_GROUPS, CHUNK, CHUNK), 0.5 * CHUNK ** -0.5),
        'b_s': 1.0 + nrm(k[15], (DEPTH, A_GROUPS, CHUNK), 0.1),
        'g_q': gain(k[16], HEAD_DIM),
        'g_kc': gain(k[17], HEAD_DIM),
        'g_ks': gain(k[18], HEAD_DIM),
        'g_kw': gain(k[19], HEAD_DIM),
        'w_ck1': nrm(k[20], (DEPTH, CMP_BLOCK * HEAD_DIM, HEAD_DIM), (CMP_BLOCK * HEAD_DIM) ** -0.5),
        'w_ck2': nrm(k[21], (DEPTH, HEAD_DIM, HEAD_DIM), HEAD_DIM ** -0.5),
        'pe_k': nrm(k[22], (DEPTH, CMP_BLOCK, HEAD_DIM), 0.1),
        'w_cv1': nrm(k[23], (DEPTH, CMP_BLOCK * HEAD_DIM, HEAD_DIM), (CMP_BLOCK * HEAD_DIM) ** -0.5),
        'w_cv2': nrm(k[24], (DEPTH, HEAD_DIM, HEAD_DIM), HEAD_DIM ** -0.5),
        'pe_v': nrm(k[25], (DEPTH, CMP_BLOCK, HEAD_DIM), 0.1),
        'w_branch': nrm(k[26], (DEPTH, A_WIDTH + B_WIDTH, D_MODEL), (A_WIDTH + B_WIDTH) ** -0.5),
        'w_out': nrm(k[27], (DEPTH, D_MODEL, D_MODEL), D_MODEL ** -0.5),
        'g_norm2': gain(k[28], D_MODEL),
        'w_up': nrm(k[29], (DEPTH, D_MODEL, D_FF), D_MODEL ** -0.5),
        'w_down': nrm(k[30], (DEPTH, D_FF, D_MODEL), D_FF ** -0.5),
    }


def reference(x_prompt, x_sample, cache_k_cmp, cache_v_cmp, cache_k_sel, cache_v_sel, cache_k_win,
              cache_v_win, page_table, g_norm1, w_in, ln_v_g, ln_v_b, w_s, b_s, g_q, g_kc, g_ks, g_kw,
              w_ck1, w_ck2, pe_k, w_cv1, w_cv2, pe_v, w_branch, w_out, g_norm2, w_up, w_down):
    pos_p = jnp.arange(SEQ, dtype=jnp.int32)
    pos_d = PAST_LEN + jnp.arange(DEC_SEQ, dtype=jnp.int32)
    w_p = min(WINDOW, SEQ)
    chunk_start = ((SEQ - 1) // CHUNK) * CHUNK
    xp, xs = x_prompt, x_sample
    new_p = [[] for _ in range(7)]
    new_s = [[] for _ in range(7)]
    for l in range(DEPTH):
        proj = lambda x, pos: in_projection(x, pos, g_norm1[l], w_in[l], ln_v_g[l], ln_v_b[l], g_q[l], g_ks[l], g_kw[l])
        u, v_n, q_n, q_r, kcr, vcr, ks, vs, kw, vw, gate, mg = proj(xp, pos_p)
        o_a = gmlp_mix(u, v_n, w_s[l], b_s[l])
        kc = rms_norm(compress(kcr, w_ck1[l], w_ck2[l], pe_k[l]), g_kc[l])
        vc = compress(vcr, w_cv1[l], w_cv2[l], pe_v[l])
        o_b = nsa_prompt(q_n, q_r, gate, kc, vc, ks, vs, kw, vw)
        xp = merge_and_ffn(xp, o_a, o_b, mg, w_branch[l], w_out[l], g_norm2[l], w_up[l], w_down[l])
        for lst, a in zip(new_p, (kcr, vcr, ks, vs, kw[:, SEQ - w_p:], vw[:, SEQ - w_p:], v_n[:, chunk_start:])):
            lst.append(a)
        u, v_n, q_n, q_r, kcr, vcr, ks, vs, kw, vw, gate, mg = proj(xs, pos_d)
        o_a = gmlp_mix(u, v_n, w_s[l], b_s[l])
        kc_all = jnp.concatenate([gather_pages(cache_k_cmp[l], page_table), kcr], axis=1)
        vc_all = jnp.concatenate([gather_pages(cache_v_cmp[l], page_table), vcr], axis=1)
        kc = rms_norm(compress(kc_all, w_ck1[l], w_ck2[l], pe_k[l]), g_kc[l])
        vc = compress(vc_all, w_cv1[l], w_cv2[l], pe_v[l])
        w_buf = cache_k_win.shape[2]
        k_w = jnp.concatenate([cache_k_win[l], kw], axis=1)
        v_w = jnp.concatenate([cache_v_win[l], vw], axis=1)
        pos_w = PAST_LEN - w_buf + jnp.arange(w_buf + DEC_SEQ, dtype=jnp.int32)
        o_b = nsa_sample(q_n, q_r, gate, kc, vc, ks, vs, k_w, v_w, pos_w, cache_k_sel[l], cache_v_sel[l], page_table)
        xs = merge_and_ffn(xs, o_a, o_b, mg, w_branch[l], w_out[l], g_norm2[l], w_up[l], w_down[l])
        for lst, a in zip(new_s, (kcr, vcr, ks, vs, kw, vw, v_n)):
            lst.append(a)
    p_kc, p_vc, p_ks, p_vs, p_kw, p_vw, p_va = [jnp.stack(a) for a in new_p]
    s_kc, s_vc, s_ks, s_vs, s_kw, s_vw, s_va = [jnp.stack(a) for a in new_s]
    return (xp, xs, p_kc, p_vc, p_ks, p_vs, p_kw, p_vw, p_va, s_kc, s_vc, s_ks, s_vs, s_kw, s_vw, s_va)
```

```python
import functools

import jax
import jax.numpy as jnp
from jax import lax
from jax.experimental import pallas as pl
from jax.experimental.pallas import tpu as pltpu

F32 = jnp.float32
BF16 = jnp.bfloat16

D_MODEL = 1024
A_WIDTH = 512
A_GROUPS = 8
CHUNK = 128
N_HEADS = 8
HEAD_DIM = 64
N_KV = 2
GROUP = N_HEADS // N_KV
B_WIDTH = N_HEADS * HEAD_DIM
KV_WIDTH = N_KV * HEAD_DIM
CMP_BLOCK = 32
CMP_STRIDE = 16
SEL_BLOCK = 64
N_SELECT = 16
WINDOW = 512
ROT_DIM = 16
ROPE_THETA = 500000.0
D_FF = 4096
PAGE_SIZE = 128
EPS = 1e-6
NEG = -1e30
FORCED_BONUS = 1e4
SCALE = HEAD_DIM ** -0.5

LANES = 128
Q_TILE = 128
KEY_TILE = 512
CHUNK_LANES = CMP_STRIDE * KV_WIDTH
HEAD_PERM = (0, 4, 1, 5, 2, 6, 3, 7)
VMEM_LIMIT = 56 * 1024 * 1024


def _low_half(shape):
    return lax.broadcasted_iota(jnp.int32, shape, len(shape) - 1) < HEAD_DIM


def _head_rms(xs, g):
    lo = _low_half(xs.shape)
    sq = xs * xs
    sa = jnp.sum(jnp.where(lo, sq, 0.0), axis=-1, keepdims=True)
    sb = jnp.sum(jnp.where(lo, 0.0, sq), axis=-1, keepdims=True)
    ms = jnp.where(lo, sa, sb) * (1.0 / HEAD_DIM)
    return xs * lax.rsqrt(ms + EPS) * g


def _rope(xs, cos, sin_up, sin_dn):
    return xs * cos + pltpu.roll(xs, ROT_DIM // 2, 1) * sin_up + pltpu.roll(xs, LANES - ROT_DIM // 2, 1) * sin_dn


def _rms_rows(x, g):
    return x * lax.rsqrt(jnp.mean(x * x, axis=-1, keepdims=True) + EPS) * g


def _in_proj_body(x_ref, g1_ref, w_ref, lng_ref, lnb_ref, gq_ref, gks_ref, gkw_ref,
                  cos_ref, sup_ref, sdn_ref, wmix_ref, bmix_ref,
                  oa_ref, vlast_ref, qn_ref, qr_ref, kc_ref, vc_ref, ks_ref, vs_ref, kw_ref, vw_ref,
                  ks16_ref, vs16_ref, kw16_ref, vw16_ref, gate_ref, *, tm, tiles_per_seg):
    i = pl.program_id(0)
    h = _rms_rows(x_ref[...], g1_ref[...]).astype(BF16)

    def proj(lo, hi):
        return jnp.dot(h, w_ref[:, lo:hi], preferred_element_type=F32)

    u = jax.nn.gelu(proj(0, A_WIDTH))
    v = jax.nn.gelu(proj(A_WIDTH, 2 * A_WIDTH))
    vc = v - jnp.mean(v, axis=-1, keepdims=True)
    vn = vc * lax.rsqrt(jnp.mean(vc * vc, axis=-1, keepdims=True) + EPS) * lng_ref[...] + lnb_ref[...]

    @pl.when(i % tiles_per_seg == tiles_per_seg - 1)
    def _():
        vlast_ref[0] = vn[tm - CHUNK:tm]

    lo = _low_half((CHUNK, LANES))
    for c in range(tm // CHUNK):
        r0, r1 = c * CHUNK, (c + 1) * CHUNK
        for j in range(A_WIDTH // LANES):
            slab = vn[r0:r1, j * LANES:(j + 1) * LANES]
            a = jnp.where(lo, slab, 0.0).astype(BF16)
            b = jnp.where(lo, 0.0, slab).astype(BF16)
            s = (jnp.dot(wmix_ref[2 * j], a, preferred_element_type=F32)
                 + jnp.dot(wmix_ref[2 * j + 1], b, preferred_element_type=F32)
                 + bmix_ref[:, j * LANES:(j + 1) * LANES])
            oa_ref[r0:r1, j * LANES:(j + 1) * LANES] = (u[r0:r1, j * LANES:(j + 1) * LANES] * s).astype(BF16)

    cos, sup, sdn = cos_ref[...], sup_ref[...], sdn_ref[...]
    q0 = 2 * A_WIDTH
    for j in range(B_WIDTH // LANES):
        qn = _head_rms(proj(q0 + j * LANES, q0 + (j + 1) * LANES), gq_ref[...])
        qn_ref[:, j * LANES:(j + 1) * LANES] = qn.astype(BF16)
        qr_ref[:, j * LANES:(j + 1) * LANES] = _rope(qn, cos, sup, sdn).astype(BF16)
    k0 = q0 + B_WIDTH
    kc_ref[...] = proj(k0, k0 + LANES)
    vc_ref[...] = proj(k0 + LANES, k0 + 2 * LANES)
    ks = _rope(_head_rms(proj(k0 + 2 * LANES, k0 + 3 * LANES), gks_ref[...]), cos, sup, sdn)
    ks_ref[...] = ks
    ks16_ref[...] = ks.astype(BF16)
    vs = proj(k0 + 3 * LANES, k0 + 4 * LANES)
    vs_ref[...] = vs
    vs16_ref[...] = vs.astype(BF16)
    kw = _rope(_head_rms(proj(k0 + 4 * LANES, k0 + 5 * LANES), gkw_ref[...]), cos, sup, sdn)
    kw_ref[...] = kw
    kw16_ref[...] = kw.astype(BF16)
    vw = proj(k0 + 5 * LANES, k0 + 6 * LANES)
    vw_ref[...] = vw
    vw16_ref[...] = vw.astype(BF16)
    gate_ref[...] = proj(k0 + 6 * LANES, k0 + 7 * LANES)


def _in_proj(x, wts, tabs, wmix, bmix, *, tm, seg_rows):
    n = x.shape[0]
    tiles_per_seg = seg_rows // tm
    n_seg = n // seg_rows
    row = lambda w: pl.BlockSpec((tm, w), lambda i: (i, 0))
    full = lambda a: pl.BlockSpec(a.shape, lambda i: (0,) * a.ndim)
    cos, sup, sdn = tabs
    ins = [x, wts['g1'], wts['w_a'], wts['ln_g'], wts['ln_b'], wts['g_q'], wts['g_ks'], wts['g_kw'],
           cos, sup, sdn, wmix, bmix]
    in_specs = [row(D_MODEL)] + [full(a) for a in ins[1:8]] + [row(LANES)] * 3 + [full(wmix), full(bmix)]
    f32o = lambda w: jax.ShapeDtypeStruct((n, w), F32)
    b16o = lambda w: jax.ShapeDtypeStruct((n, w), BF16)
    out_shape = ([b16o(A_WIDTH), jax.ShapeDtypeStruct((n_seg, CHUNK, A_WIDTH), F32), b16o(B_WIDTH), b16o(B_WIDTH)]
                 + [f32o(LANES)] * 6 + [b16o(LANES)] * 4 + [f32o(LANES)])
    out_specs = ([row(A_WIDTH), pl.BlockSpec((1, CHUNK, A_WIDTH), lambda i: (i // tiles_per_seg, 0, 0)),
                  row(B_WIDTH), row(B_WIDTH)] + [row(LANES)] * 11)
    return pl.pallas_call(
        functools.partial(_in_proj_body, tm=tm, tiles_per_seg=tiles_per_seg),
        grid=(n // tm,), in_specs=in_specs, out_specs=out_specs, out_shape=out_shape,
        compiler_params=pltpu.CompilerParams(dimension_semantics=("arbitrary",), vmem_limit_bytes=VMEM_LIMIT),
        name="in_proj")(*ins)


COMPRESS_ROWS = 256


def _compress_math(load_rows, m, w1ab_ref, pe_ref, w1d_ref, w2d_ref, g_ref, norm):
    step = min(COMPRESS_ROWS, m)
    f = jnp.concatenate(
        [jnp.dot(load_rows(r, r + step).astype(BF16), w1ab_ref[...], preferred_element_type=F32)
         for r in range(0, m, step)], axis=0)
    bias = jnp.dot(pe_ref[...], w1d_ref[...], preferred_element_type=F32)[0:1]
    second_next = pltpu.roll(f[:, LANES:], m - 1, 0)
    act = jax.nn.gelu(f[:, :LANES] + second_next + bias)
    out = jnp.dot(act.astype(BF16), w2d_ref[...], preferred_element_type=F32)
    if norm:
        out = _head_rms(out, g_ref[...])
    return out.astype(BF16)


def _compress_body(ch_ref, w1ab_ref, pe_ref, w1d_ref, w2d_ref, g_ref, out_ref, *, norm):
    out_ref[0] = _compress_math(lambda r0, r1: ch_ref[0, r0:r1], ch_ref.shape[1],
                                w1ab_ref, pe_ref, w1d_ref, w2d_ref, g_ref, norm)


def _compress(ch, cw, *, norm):
    nb, m, _ = ch.shape
    full = lambda a: pl.BlockSpec(a.shape, lambda b: (0,) * a.ndim)
    ws = [cw['w1ab'], cw['pe'], cw['w1d'], cw['w2d'], cw['g']]
    return pl.pallas_call(
        functools.partial(_compress_body, norm=norm),
        grid=(nb,),
        in_specs=[pl.BlockSpec((1, m, CHUNK_LANES), lambda b: (b, 0, 0))] + [full(a) for a in ws],
        out_specs=pl.BlockSpec((1, m, LANES), lambda b: (b, 0, 0)),
        out_shape=jax.ShapeDtypeStruct((nb, m, LANES), BF16),
        compiler_params=pltpu.CompilerParams(dimension_semantics=("arbitrary",), vmem_limit_bytes=VMEM_LIMIT),
        name="compress_k" if norm else "compress_v")(ch, *ws)


def _page_copy(pool_ref, page, buf_ref, slot, i, rows, sem_ref):
    return pltpu.make_async_copy(pool_ref.at[page], buf_ref.at[slot, pl.ds(i * rows, rows)], sem_ref.at[slot])


def _paged_compress_body(pt_ref, pool_ref, w1ab_ref, pe_ref, w1d_ref, w2d_ref, g_ref, out_ref, buf_ref, sem_ref,
                         *, n_pages, norm):
    b = pl.program_id(0)
    rows = PAGE_SIZE // CMP_STRIDE

    def fetch(bb, slot):
        def one(i, carry):
            _page_copy(pool_ref, pt_ref[bb * n_pages + i], buf_ref, slot, i, rows, sem_ref).start()
            return carry
        lax.fori_loop(0, n_pages, one, 0)

    @pl.when(b == 0)
    def _():
        fetch(0, 0)

    @pl.when(b + 1 < pl.num_programs(0))
    def _():
        fetch(b + 1, (b + 1) % 2)

    slot = b % 2

    def wait_one(i, carry):
        _page_copy(pool_ref, 0, buf_ref, slot, i, rows, sem_ref).wait()
        return carry
    lax.fori_loop(0, n_pages, wait_one, 0)
    out_ref[0] = _compress_math(lambda r0, r1: buf_ref[slot, r0:r1], buf_ref.shape[1],
                                w1ab_ref, pe_ref, w1d_ref, w2d_ref, g_ref, norm)


def _paged_compress(pt_flat, pool, cw, *, nb, n_pages, norm):
    m = n_pages * (PAGE_SIZE // CMP_STRIDE)
    full = lambda a: pl.BlockSpec(a.shape, lambda b, pt: (0,) * a.ndim)
    ws = [cw['w1ab'], cw['pe'], cw['w1d'], cw['w2d'], cw['g']]
    grid_spec = pltpu.PrefetchScalarGridSpec(
        num_scalar_prefetch=1, grid=(nb,),
        in_specs=[pl.BlockSpec(memory_space=pl.ANY)] + [full(a) for a in ws],
        out_specs=pl.BlockSpec((1, m, LANES), lambda b, pt: (b, 0, 0)),
        scratch_shapes=[pltpu.VMEM((2, m, CHUNK_LANES), F32), pltpu.SemaphoreType.DMA((2,))])
    return pl.pallas_call(
        functools.partial(_paged_compress_body, n_pages=n_pages, norm=norm),
        grid_spec=grid_spec, out_shape=jax.ShapeDtypeStruct((nb, m, LANES), BF16),
        compiler_params=pltpu.CompilerParams(dimension_semantics=("arbitrary",), vmem_limit_bytes=VMEM_LIMIT),
        name="paged_compress_k" if norm else "paged_compress_v")(pt_flat, pool, *ws)


def _split3(x):
    hi = x.astype(BF16)
    r1 = x - hi.astype(F32)
    mid = r1.astype(BF16)
    lo = (r1 - mid.astype(F32)).astype(BF16)
    return hi, mid, lo


def _dot_nt(a, b):
    return lax.dot_general(a, b, (((1,), (1,)), ((), ())), preferred_element_type=F32)


def _topk_mask_t(rank, n_pick):
    n_blk = rank.shape[0]
    jidx = lax.broadcasted_iota(jnp.int32, rank.shape, 0).astype(F32)

    def body(_, carry):
        r, sel = carry
        m = jnp.max(r, axis=0, keepdims=True)
        first = jnp.min(jnp.where(r == m, jidx, float(n_blk)), axis=0, keepdims=True)
        pick = jidx == first
        return jnp.where(pick, -jnp.inf, r), jnp.where(pick, 1.0, sel)

    _, sel = lax.fori_loop(0, n_pick, body, (rank, jnp.zeros_like(rank)))
    return sel


def _masked_softmax(s, mask):
    s = jnp.where(mask, s, NEG)
    e = jnp.exp(s - jnp.max(s, axis=-1, keepdims=True))
    return e / jnp.sum(e, axis=-1, keepdims=True) * mask.astype(F32)


def _gate_col(gt, head, branch):
    c = head * 3 + branch
    return gt[:, c:c + 1]


def _prompt_attn_body(qn_ref, qr_ref, gate_ref, kc_ref, vc_ref, ks_ref, vs_ref, kw_ref, vw_ref, e3_ref, at_ref,
                      o_ref, sel_ref, m_ref, l_ref, acc_ref, *, seq, n_cmp):
    i = pl.program_id(1)
    s0 = i * Q_TILE
    nc_pad = kc_ref.shape[1]
    n_sel = seq // SEL_BLOCK
    lo = _low_half((Q_TILE, LANES))
    pos_col = s0 + lax.broadcasted_iota(jnp.int32, (Q_TILE, 1), 0)

    def group_q(q_ref, kv):
        keep = lo if kv == 0 else jnp.logical_not(lo)
        parts = [jnp.where(keep, q_ref[0, :, g * LANES:(g + 1) * LANES], 0) for g in range(GROUP)]
        return jnp.concatenate(parts, axis=0)

    cidx = lax.broadcasted_iota(jnp.int32, (Q_TILE, nc_pad), 1)
    m_c = (cidx * CMP_STRIDE + (CMP_BLOCK - 1) <= pos_col) & (cidx < n_cmp)
    jrow = lax.broadcasted_iota(jnp.int32, (n_sel, Q_TILE), 0)
    pos_row = s0 + lax.broadcasted_iota(jnp.int32, (n_sel, Q_TILE), 1)
    cur = pos_row >> 6
    forced = (jrow == 0) | (jrow == cur) | (jrow == cur - 1)
    valid_b = jrow * SEL_BLOCK <= pos_row
    kc = kc_ref[0]
    vc = vc_ref[0]
    o_cmp = []
    for kv in range(N_KV):
        s = (_dot_nt(group_q(qn_ref, kv), kc) * SCALE).reshape(GROUP, Q_TILE, nc_pad)
        p = _masked_softmax(s, m_c[None])
        o_cmp.append(jnp.dot(p.reshape(GROUP * Q_TILE, nc_pad).astype(BF16), vc, preferred_element_type=F32))
        imp = jnp.sum(p, axis=0)
        blk_t = sum(_dot_nt(at_ref[...], part) for part in _split3(imp))
        rank = jnp.where(forced, FORCED_BONUS, jnp.where(valid_b, blk_t, -FORCED_BONUS))
        sel_t = _topk_mask_t(rank, min(N_SELECT, n_sel))
        sel_ref[kv] = sel_t.T.astype(BF16)

    m_ref[...] = jnp.full(m_ref.shape, NEG, F32)
    l_ref[...] = jnp.zeros(l_ref.shape, F32)
    acc_ref[...] = jnp.zeros(acc_ref.shape, F32)
    q_sel = [group_q(qr_ref, kv) for kv in range(N_KV)]

    def tile_body(tix, carry):
        k0 = pl.multiple_of(tix * KEY_TILE, KEY_TILE)
        kt = ks_ref[0, pl.ds(k0, KEY_TILE), :]
        vt = vs_ref[0, pl.ds(k0, KEY_TILE), :]
        et = e3_ref[tix]
        causal = k0 + lax.broadcasted_iota(jnp.int32, (Q_TILE, KEY_TILE), 1) <= pos_col
        for kv in range(N_KV):
            chosen = jnp.dot(sel_ref[kv], et, preferred_element_type=F32) > 0.5
            bias = jnp.where(chosen & causal, 0.0, NEG)
            s = (_dot_nt(q_sel[kv], kt) * SCALE).reshape(GROUP, Q_TILE, KEY_TILE) + bias[None]
            m_old = m_ref[kv]
            m_new = jnp.maximum(m_old, jnp.max(s, axis=-1, keepdims=True))
            alpha = jnp.exp(m_old - m_new)
            p = jnp.exp(s - m_new)
            l_ref[kv] = alpha * l_ref[kv] + jnp.sum(p, axis=-1, keepdims=True)
            pv = jnp.dot(p.reshape(GROUP * Q_TILE, KEY_TILE).astype(BF16), vt, preferred_element_type=F32)
            acc_ref[kv] = alpha * acc_ref[kv] + pv.reshape(GROUP, Q_TILE, LANES)
            m_ref[kv] = m_new
        return carry

    q_per_key = KEY_TILE // Q_TILE
    lax.fori_loop(0, lax.shift_right_logical(i + q_per_key, q_per_key.bit_length() - 1), tile_body, 0)

    n_win = WINDOW + Q_TILE
    w0 = pl.multiple_of(jnp.maximum(s0 - WINDOW, 0), Q_TILE)
    kwt = kw_ref[0, pl.ds(w0, n_win), :]
    vwt = vw_ref[0, pl.ds(w0, n_win), :]
    dpos = pos_col - (w0 + lax.broadcasted_iota(jnp.int32, (Q_TILE, n_win), 1))
    m_w = (dpos >= 0) & (dpos < WINDOW)
    gt = jax.nn.sigmoid(gate_ref[0])
    o_heads = []
    for kv in range(N_KV):
        s = (_dot_nt(q_sel[kv], kwt) * SCALE).reshape(GROUP, Q_TILE, n_win)
        p = _masked_softmax(s, m_w[None])
        o_win = jnp.dot(p.reshape(GROUP * Q_TILE, n_win).astype(BF16), vwt, preferred_element_type=F32)
        o_sel = acc_ref[kv] / l_ref[kv]
        for g in range(GROUP):
            hd = kv * GROUP + g
            rows = slice(g * Q_TILE, (g + 1) * Q_TILE)
            o_heads.append(_gate_col(gt, hd, 0) * o_cmp[kv][rows] + _gate_col(gt, hd, 1) * o_sel[g]
                           + _gate_col(gt, hd, 2) * o_win[rows])
    for j in range(GROUP):
        o_ref[0, :, j * LANES:(j + 1) * LANES] = jnp.where(lo, o_heads[j], o_heads[GROUP + j]).astype(BF16)


def _prompt_attn(qn, qr, gate, kc, vc, ks16, vs16, kw16, vw16, e3, a_t, *, n_cmp):
    nb, seq, _ = qn.shape
    n_sel = seq // SEL_BLOCK
    qspec = lambda w: pl.BlockSpec((1, Q_TILE, w), lambda b, i: (b, i, 0))
    seqspec = lambda a: pl.BlockSpec((1,) + a.shape[1:], lambda b, i: (b, 0, 0))
    full = lambda a: pl.BlockSpec(a.shape, lambda b, i: (0,) * a.ndim)
    return pl.pallas_call(
        functools.partial(_prompt_attn_body, seq=seq, n_cmp=n_cmp),
        grid=(nb, seq // Q_TILE),
        in_specs=[qspec(B_WIDTH), qspec(B_WIDTH), qspec(LANES), seqspec(kc), seqspec(vc),
                  seqspec(ks16), seqspec(vs16), seqspec(kw16), seqspec(vw16), full(e3), full(a_t)],
        out_specs=qspec(B_WIDTH),
        out_shape=jax.ShapeDtypeStruct((nb, seq, B_WIDTH), BF16),
        scratch_shapes=[pltpu.VMEM((N_KV, Q_TILE, n_sel), BF16),
                        pltpu.VMEM((N_KV, GROUP, Q_TILE, 1), F32),
                        pltpu.VMEM((N_KV, GROUP, Q_TILE, 1), F32),
                        pltpu.VMEM((N_KV, GROUP, Q_TILE, LANES), F32)],
        compiler_params=pltpu.CompilerParams(dimension_semantics=("arbitrary", "arbitrary"),
                                             vmem_limit_bytes=VMEM_LIMIT),
        name="prompt_attn")(qn, qr, gate, kc, vc, ks16, vs16, kw16, vw16, e3, a_t)


TOK_PAD = 8
ROWS = N_HEADS * TOK_PAD
ROWS_PAD = 128


def _sample_attn_body(pt_ref, qn_ref, qr_ref, gate_ref, kc_ref, vc_ref, kwc_ref, vwc_ref,
                      ksn_ref, vsn_ref, kwn_ref, vwn_ref, at_ref, gsum_ref, kpool_ref, vpool_ref,
                      o_ref, kbuf_ref, vbuf_ref, ksem_ref, vsem_ref, sel_ref, m_ref, l_ref, acc_ref, ocw_ref, q_ref,
                      *, n_chunks, pages_per_chunk, past, n_new, n_cmp):
    b = pl.program_id(0)
    c = pl.program_id(1)
    step = b * n_chunks + c
    n_steps = pl.num_programs(0) * n_chunks
    chunk_keys = pages_per_chunk * PAGE_SIZE
    n_past_blk = past // SEL_BLOCK
    blk_per_chunk = chunk_keys // SEL_BLOCK

    def fetch(st, slot):
        def one(i, carry):
            page = pt_ref[st * pages_per_chunk + i]
            _page_copy(kpool_ref, page, kbuf_ref, slot, i, PAGE_SIZE, ksem_ref).start()
            _page_copy(vpool_ref, page, vbuf_ref, slot, i, PAGE_SIZE, vsem_ref).start()
            return carry
        lax.fori_loop(0, pages_per_chunk, one, 0)

    @pl.when(step == 0)
    def _():
        fetch(0, 0)

    @pl.when(step + 1 < n_steps)
    def _():
        fetch(step + 1, (step + 1) % 2)

    lo = _low_half((TOK_PAD, LANES))
    tok_col = lax.broadcasted_iota(jnp.int32, (ROWS, 1), 0) % TOK_PAD
    pos_col = past + tok_col

    def rows_q(ref):
        parts = []
        for hd in range(N_HEADS):
            keep = lo if hd < GROUP else jnp.logical_not(lo)
            j = hd % GROUP
            parts.append(jnp.where(keep, ref[0, :, j * LANES:(j + 1) * LANES], 0))
        return jnp.concatenate(parts, axis=0)

    def gate_rows(gt, branch):
        return jnp.concatenate([_gate_col(gt, hd, branch) for hd in range(N_HEADS)], axis=0)

    def new_mask(width):
        k = lax.broadcasted_iota(jnp.int32, (ROWS, width), 1)
        return (k <= tok_col) & (k < n_new)

    @pl.when(c == 0)
    def _():
        gt = jax.nn.sigmoid(gate_ref[0])
        nc_pad = kc_ref.shape[1]
        cidx = lax.broadcasted_iota(jnp.int32, (ROWS, nc_pad), 1)
        m_c = (cidx * CMP_STRIDE + (CMP_BLOCK - 1) <= pos_col) & (cidx < n_cmp)
        p = _masked_softmax(_dot_nt(rows_q(qn_ref), kc_ref[0]) * SCALE, m_c)
        o_c = jnp.dot(p.astype(BF16), vc_ref[0], preferred_element_type=F32)
        imp = sum(jnp.dot(gsum_ref[...], part, preferred_element_type=F32) for part in _split3(p))
        blk_t = sum(_dot_nt(at_ref[...], part) for part in _split3(imp))
        jrow = lax.broadcasted_iota(jnp.int32, blk_t.shape, 0)
        rank = jnp.where((jrow == 0) | (jrow == n_past_blk - 1), FORCED_BONUS, blk_t)
        n_top = min(N_SELECT, n_past_blk + 1)
        sel_t = _topk_mask_t(rank, n_top - 1)
        sel = jnp.concatenate([sel_t[k * LANES:(k + 1) * LANES].T for k in range(n_past_blk // LANES)], axis=1)
        sel_ref[...] = sel[:ROWS].astype(BF16)
        q_r = rows_q(qr_ref)
        q_ref[...] = q_r
        n_buf = kwc_ref.shape[1]
        kidx = lax.broadcasted_iota(jnp.int32, (ROWS, n_buf), 1)
        dpos = pos_col - (past - n_buf + kidx)
        m_old = (dpos >= 0) & (dpos < WINDOW)
        m_new = new_mask(TOK_PAD)
        s_old = jnp.where(m_old, _dot_nt(q_r, kwc_ref[0].astype(BF16)) * SCALE, NEG)
        s_new = jnp.where(m_new, _dot_nt(q_r, kwn_ref[0]) * SCALE, NEG)
        mx = jnp.maximum(jnp.max(s_old, axis=-1, keepdims=True), jnp.max(s_new, axis=-1, keepdims=True))
        e_old = jnp.exp(s_old - mx)
        e_new = jnp.exp(s_new - mx)
        den = jnp.sum(e_old, axis=-1, keepdims=True) + jnp.sum(e_new, axis=-1, keepdims=True)
        p_old = e_old / den * m_old.astype(F32)
        p_new = e_new / den * m_new.astype(F32)
        o_w = (jnp.dot(p_old.astype(BF16), vwc_ref[0].astype(BF16), preferred_element_type=F32)
               + jnp.dot(p_new.astype(BF16), vwn_ref[0], preferred_element_type=F32))
        ocw_ref[...] = gate_rows(gt, 0) * o_c + gate_rows(gt, 2) * o_w
        s = jnp.where(m_new, _dot_nt(q_r, ksn_ref[0]) * SCALE, NEG)
        m0 = jnp.max(s, axis=-1, keepdims=True)
        e = jnp.exp(s - m0) * m_new.astype(F32)
        m_ref[...] = m0
        l_ref[...] = jnp.sum(e, axis=-1, keepdims=True)
        acc_ref[...] = jnp.dot(e.astype(BF16), vsn_ref[0], preferred_element_type=F32)

    slot = step % 2

    def wait_one(i, carry):
        _page_copy(kpool_ref, 0, kbuf_ref, slot, i, PAGE_SIZE, ksem_ref).wait()
        _page_copy(vpool_ref, 0, vbuf_ref, slot, i, PAGE_SIZE, vsem_ref).wait()
        return carry
    lax.fori_loop(0, pages_per_chunk, wait_one, 0)

    jj = lax.broadcasted_iota(jnp.int32, (n_past_blk, chunk_keys), 0)
    kk = lax.broadcasted_iota(jnp.int32, (n_past_blk, chunk_keys), 1)
    expand = jnp.where(jj == c * blk_per_chunk + (kk >> 6), 1.0, 0.0).astype(BF16)
    chosen = jnp.dot(sel_ref[...], expand, preferred_element_type=F32) > 0.5
    s = jnp.where(chosen, _dot_nt(q_ref[...], kbuf_ref[slot].astype(BF16)) * SCALE, NEG)
    m_old = m_ref[...]
    m_new = jnp.maximum(m_old, jnp.max(s, axis=-1, keepdims=True))
    alpha = jnp.exp(m_old - m_new)
    p = jnp.exp(s - m_new) * chosen.astype(F32)
    l_ref[...] = alpha * l_ref[...] + jnp.sum(p, axis=-1, keepdims=True)
    acc_ref[...] = alpha * acc_ref[...] + jnp.dot(p.astype(BF16), vbuf_ref[slot].astype(BF16),
                                                  preferred_element_type=F32)
    m_ref[...] = m_new

    @pl.when(c == n_chunks - 1)
    def _():
        gt = jax.nn.sigmoid(gate_ref[0])
        o = ocw_ref[...] + gate_rows(gt, 1) * (acc_ref[...] / l_ref[...])
        for j in range(GROUP):
            top = o[j * TOK_PAD:(j + 1) * TOK_PAD]
            bot = o[(GROUP + j) * TOK_PAD:(GROUP + j + 1) * TOK_PAD]
            o_ref[0, :, j * LANES:(j + 1) * LANES] = jnp.where(lo, top, bot).astype(BF16)


def _sample_attn(pt_flat, qn, qr, gate, kc, vc, kwc, vwc, ksn, vsn, kwn, vwn, a_t, gsum, kpool, vpool,
                 *, n_chunks, pages_per_chunk, past, n_new, n_cmp):
    nb = qn.shape[0]
    chunk_keys = pages_per_chunk * PAGE_SIZE
    bspec = lambda a: pl.BlockSpec((1,) + a.shape[1:], lambda b, c, pt: (b, 0, 0))
    full = lambda a: pl.BlockSpec(a.shape, lambda b, c, pt: (0,) * a.ndim)
    anyspec = pl.BlockSpec(memory_space=pl.ANY)
    blocked = [qn, qr, gate, kc, vc, kwc, vwc, ksn, vsn, kwn, vwn]
    grid_spec = pltpu.PrefetchScalarGridSpec(
        num_scalar_prefetch=1, grid=(nb, n_chunks),
        in_specs=[bspec(a) for a in blocked] + [full(a_t), full(gsum), anyspec, anyspec],
        out_specs=pl.BlockSpec((1, TOK_PAD, B_WIDTH), lambda b, c, pt: (b, 0, 0)),
        scratch_shapes=[pltpu.VMEM((2, chunk_keys, LANES), F32), pltpu.VMEM((2, chunk_keys, LANES), F32),
                        pltpu.SemaphoreType.DMA((2,)), pltpu.SemaphoreType.DMA((2,)),
                        pltpu.VMEM((ROWS, past // SEL_BLOCK), BF16),
                        pltpu.VMEM((ROWS, 1), F32), pltpu.VMEM((ROWS, 1), F32), pltpu.VMEM((ROWS, LANES), F32),
                        pltpu.VMEM((ROWS, LANES), F32), pltpu.VMEM((ROWS, LANES), BF16)])
    return pl.pallas_call(
        functools.partial(_sample_attn_body, n_chunks=n_chunks, pages_per_chunk=pages_per_chunk, past=past,
                          n_new=n_new, n_cmp=n_cmp),
        grid_spec=grid_spec, out_shape=jax.ShapeDtypeStruct((nb, TOK_PAD, B_WIDTH), BF16),
        compiler_params=pltpu.CompilerParams(dimension_semantics=("arbitrary", "arbitrary"),
                                             vmem_limit_bytes=VMEM_LIMIT),
        name="sample_attn")(pt_flat, *blocked, a_t, gsum, kpool, vpool)


FF_TILE = 1024


def _merge_ffn_body(x_ref, oa_ref, ob_ref, g1_ref, wmg_ref, wba_ref, wbb_ref, wout_ref, g2_ref, wup_ref, wdn_ref,
                    y_ref):
    x = x_ref[...]
    h = _rms_rows(x, g1_ref[...]).astype(BF16)
    y_a = jnp.dot(oa_ref[...], wba_ref[...], preferred_element_type=F32)
    y_b = jnp.dot(ob_ref[...], wbb_ref[...], preferred_element_type=F32)
    g_a = jax.nn.sigmoid(jnp.dot(h, wmg_ref[:, :D_MODEL], preferred_element_type=F32))
    g_b = jax.nn.sigmoid(jnp.dot(h, wmg_ref[:, D_MODEL:], preferred_element_type=F32))
    x1 = x + jnp.dot((g_a * y_a + g_b * y_b).astype(BF16), wout_ref[...], preferred_element_type=F32)
    h2 = _rms_rows(x1, g2_ref[...]).astype(BF16)
    y = x1
    for f in range(D_FF // FF_TILE):
        up = jnp.dot(h2, wup_ref[:, f * FF_TILE:(f + 1) * FF_TILE], preferred_element_type=F32)
        act = jnp.square(jnp.maximum(up, 0.0)).astype(BF16)
        y = y + jnp.dot(act, wdn_ref[f * FF_TILE:(f + 1) * FF_TILE, :], preferred_element_type=F32)
    y_ref[...] = y


def _merge_ffn(x, o_a, o_b, wts, *, tm):
    n = x.shape[0]
    row = lambda w: pl.BlockSpec((tm, w), lambda i: (i, 0))
    const = lambda a: pl.BlockSpec(a.shape, lambda i: (0,) * a.ndim, pipeline_mode=pl.Buffered(1))
    ws = [wts['g1'], wts['w_mg'], wts['w_br_a'], wts['w_br_b'], wts['w_out'], wts['g2'], wts['w_up'], wts['w_down']]
    return pl.pallas_call(
        _merge_ffn_body, grid=(n // tm,),
        in_specs=[row(D_MODEL), row(A_WIDTH), row(B_WIDTH)] + [const(a) for a in ws],
        out_specs=row(D_MODEL), out_shape=jax.ShapeDtypeStruct((n, D_MODEL), F32),
        compiler_params=pltpu.CompilerParams(dimension_semantics=("arbitrary",), vmem_limit_bytes=VMEM_LIMIT),
        name="merge_ffn")(x, o_a, o_b, *ws)


def _rope_tables(pos):
    inv = ROPE_THETA ** (-jnp.arange(0, ROT_DIM, 2, dtype=F32) / ROT_DIM)
    ang = pos.astype(F32)[:, None] * inv[None, :]
    cos, sin = jnp.cos(ang), jnp.sin(ang)
    half = ROT_DIM // 2
    n = pos.shape[0]
    pad = jnp.zeros((n, HEAD_DIM - ROT_DIM), F32)
    zero = jnp.zeros((n, half), F32)
    cos_h = jnp.concatenate([cos, cos, pad + 1.0], axis=1)
    sup_h = jnp.concatenate([zero, sin, pad], axis=1)
    sdn_h = jnp.concatenate([-sin, zero, pad], axis=1)
    two = lambda t: jnp.concatenate([t, t], axis=1)
    return two(cos_h), two(sup_h), two(sdn_h)


def _two(g):
    return jnp.concatenate([g, g])[None, :].astype(F32)


def _layer_weights(l, g_norm1, w_in, ln_v_g, ln_v_b, g_q, g_ks, g_kw, w_branch, w_out, g_norm2, w_up, w_down):
    w = w_in[l]
    q0, k0 = 2 * A_WIDTH, 2 * A_WIDTH + B_WIDTH
    g0 = k0 + 6 * KV_WIDTH
    perm = jnp.array(HEAD_PERM)
    q_cols = w[:, q0:k0].reshape(D_MODEL, N_HEADS, HEAD_DIM)[:, perm].reshape(D_MODEL, B_WIDTH)
    gate_cols = jnp.pad(w[:, g0:g0 + 3 * N_HEADS], ((0, 0), (0, LANES - 3 * N_HEADS)))
    w_a = jnp.concatenate([w[:, :q0], q_cols, w[:, k0:g0], gate_cols], axis=1).astype(BF16)
    wb = w_branch[l]
    w_br_b = wb[A_WIDTH:].reshape(N_HEADS, HEAD_DIM, D_MODEL)[perm].reshape(B_WIDTH, D_MODEL)
    return dict(
        g1=g_norm1[l][None, :], w_a=w_a, ln_g=ln_v_g[l][None, :], ln_b=ln_v_b[l][None, :],
        g_q=_two(g_q[l]), g_ks=_two(g_ks[l]), g_kw=_two(g_kw[l]),
        w_mg=w[:, g0 + 3 * N_HEADS:].astype(BF16), w_br_a=wb[:A_WIDTH].astype(BF16), w_br_b=w_br_b.astype(BF16),
        w_out=w_out[l].astype(BF16), g2=g_norm2[l][None, :], w_up=w_up[l].astype(BF16),
        w_down=w_down[l].astype(BF16))


def _compress_weights(w1, w2, pe, g):
    half = CMP_STRIDE * HEAD_DIM
    eye = jnp.eye(N_KV, dtype=F32)

    def spread(wh):
        return jnp.einsum('rde,kl->rkdle', wh.reshape(CMP_STRIDE, HEAD_DIM, HEAD_DIM), eye).reshape(
            CHUNK_LANES, KV_WIDTH)

    w1ab = jnp.concatenate([spread(w1[:half]), spread(w1[half:])], axis=1).astype(BF16)
    pe_rows = jnp.zeros((8, CMP_BLOCK * HEAD_DIM), F32).at[0].set(pe.reshape(-1)).astype(BF16)
    w1d = jnp.concatenate([w1, w1], axis=1).astype(BF16)
    w2d = jnp.einsum('de,kl->kdle', w2, eye).reshape(KV_WIDTH, KV_WIDTH).astype(BF16)
    return dict(w1ab=w1ab, pe=pe_rows, w1d=w1d, w2d=w2d, g=_two(g))


def _block_score_matrix(n_blk, n_chunk):
    j = jnp.arange(n_blk)[:, None]
    c = jnp.arange(n_chunk)[None, :]
    per = SEL_BLOCK // CMP_STRIDE
    a = ((c >= per * j) & (c <= per * j + per - 1)).astype(F32) + ((c >= per * j - 1) & (c <= per * j + per - 2))
    return a.astype(BF16)


def kernel(x_prompt, x_sample, cache_k_cmp, cache_v_cmp, cache_k_sel, cache_v_sel, cache_k_win, cache_v_win,
           page_table, g_norm1, w_in, ln_v_g, ln_v_b, w_s, b_s, g_q, g_kc, g_ks, g_kw, w_ck1, w_ck2, pe_k,
           w_cv1, w_cv2, pe_v, w_branch, w_out, g_norm2, w_up, w_down):
    nb, seq, _ = x_prompt.shape
    db, dseq, _ = x_sample.shape
    depth = w_in.shape[0]
    n_pages = page_table.shape[1]
    past = n_pages * PAGE_SIZE
    n_phys = cache_k_cmp.shape[1]
    w_buf = cache_k_win.shape[2]
    assert depth == 1 and seq % KEY_TILE == 0 and seq >= WINDOW + Q_TILE and dseq <= TOK_PAD
    assert past % (SEL_BLOCK * LANES) == 0 and (past + dseq) // CMP_STRIDE == past // CMP_STRIDE
    l = 0
    wts = _layer_weights(l, g_norm1, w_in, ln_v_g, ln_v_b, g_q, g_ks, g_kw, w_branch, w_out, g_norm2, w_up, w_down)
    cw_k = _compress_weights(w_ck1[l], w_ck2[l], pe_k[l], g_kc[l])
    cw_v = _compress_weights(w_cv1[l], w_cv2[l], pe_v[l], g_kc[l])
    causal = jnp.tril(jnp.ones((CHUNK, CHUNK), dtype=bool))
    ws_tril = jnp.where(causal[None], w_s[l], 0.0)
    heads = lambda a, lead: a.reshape(lead + (N_KV, HEAD_DIM))

    n_p = nb * seq
    xp = x_prompt.reshape(n_p, D_MODEL)
    tabs_p = _rope_tables(jnp.tile(jnp.arange(seq, dtype=jnp.int32), nb))
    bmix_p = jnp.repeat(b_s[l].T, A_WIDTH // A_GROUPS, axis=1)
    (oa_p, vlast_p, qn_p, qr_p, kc_p, vc_p, ks_p, vs_p, kw_p, vw_p, ks16, vs16, kw16, vw16, gate_p) = _in_proj(
        xp, wts, tabs_p, ws_tril.astype(BF16), bmix_p, tm=256, seg_rows=seq)
    n_chunk_p = seq // CMP_STRIDE
    kcmp_p = _compress(kc_p.reshape(nb, n_chunk_p, CHUNK_LANES), cw_k, norm=True)
    vcmp_p = _compress(vc_p.reshape(nb, n_chunk_p, CHUNK_LANES), cw_v, norm=False)
    n_sel_p = seq // SEL_BLOCK
    kblk = jnp.arange(seq) // SEL_BLOCK
    e3 = (jnp.arange(n_sel_p)[None, :, None] == kblk.reshape(seq // KEY_TILE, 1, KEY_TILE)).astype(BF16)
    a_t_p = _block_score_matrix(n_sel_p, n_chunk_p)
    b3 = lambda a: a.reshape(nb, seq, a.shape[-1])
    ob_p = _prompt_attn(b3(qn_p), b3(qr_p), b3(gate_p), kcmp_p, vcmp_p, b3(ks16), b3(vs16), b3(kw16), b3(vw16),
                        e3, a_t_p, n_cmp=n_chunk_p - 1)
    y_p = _merge_ffn(xp, oa_p, ob_p.reshape(n_p, B_WIDTH), wts, tm=512).reshape(nb, seq, D_MODEL)

    n_s = db * dseq
    xs = x_sample.reshape(n_s, D_MODEL)
    tabs_s = _rope_tables(jnp.tile(past + jnp.arange(dseq, dtype=jnp.int32), db))
    wmix_s = jnp.einsum('gpr,bc->gbpcr', ws_tril[:, :dseq, :dseq], jnp.eye(db, dtype=F32)).reshape(
        A_GROUPS, n_s, n_s)
    bmix_s = jnp.tile(bmix_p[:dseq], (db, 1))
    (oa_s, v_s, qn_s, qr_s, kc_s, vc_s, ks_s, vs_s, kw_s, vw_s, ks16s, vs16s, kw16s, vw16s, gate_s) = _in_proj(
        xs, wts, tabs_s, wmix_s.astype(BF16), bmix_s, tm=n_s, seg_rows=n_s)
    pt_flat = page_table.reshape(-1)
    chunks_per_page = PAGE_SIZE // CMP_STRIDE
    kcmp_s = _paged_compress(pt_flat, cache_k_cmp[l].reshape(n_phys, chunks_per_page, CHUNK_LANES), cw_k,
                             nb=db, n_pages=n_pages, norm=True)
    vcmp_s = _paged_compress(pt_flat, cache_v_cmp[l].reshape(n_phys, chunks_per_page, CHUNK_LANES), cw_v,
                             nb=db, n_pages=n_pages, norm=False)
    n_chunk_s = past // CMP_STRIDE
    tokpad = lambda a: jnp.pad(a.reshape(db, dseq, a.shape[-1]), ((0, 0), (0, TOK_PAD - dseq), (0, 0)))
    hh = jnp.arange(ROWS_PAD) // TOK_PAD
    tt = jnp.arange(ROWS_PAD) % TOK_PAD
    gsum = ((hh[:, None] // GROUP == hh[None, :ROWS] // GROUP) & (tt[:, None] == tt[None, :ROWS])
            & (hh[:, None] < N_HEADS)).astype(BF16)
    n_chunks = 4
    ob_s = _sample_attn(
        pt_flat, tokpad(qn_s), tokpad(qr_s), tokpad(gate_s), kcmp_s, vcmp_s,
        cache_k_win[l].reshape(db, w_buf, KV_WIDTH), cache_v_win[l].reshape(db, w_buf, KV_WIDTH),
        tokpad(ks16s), tokpad(vs16s), tokpad(kw16s), tokpad(vw16s),
        _block_score_matrix(past // SEL_BLOCK, n_chunk_s), gsum,
        cache_k_sel[l].reshape(n_phys, PAGE_SIZE, KV_WIDTH), cache_v_sel[l].reshape(n_phys, PAGE_SIZE, KV_WIDTH),
        n_chunks=n_chunks, pages_per_chunk=n_pages // n_chunks, past=past, n_new=dseq, n_cmp=n_chunk_s - 1)
    y_s = _merge_ffn(xs, oa_s, ob_s[:, :dseq].reshape(n_s, B_WIDTH), wts, tm=n_s).reshape(db, dseq, D_MODEL)

    w_p = min(WINDOW, seq)
    pk = lambda a: heads(a, (nb, seq))[None]
    sk = lambda a: heads(a, (db, dseq))[None]
    return (y_p, y_s, pk(kc_p), pk(vc_p), pk(ks_p), pk(vs_p), pk(kw_p)[:, :, seq - w_p:], pk(vw_p)[:, :, seq - w_p:],
            vlast_p[None], sk(kc_s), sk(vc_s), sk(ks_s), sk(vs_s), sk(kw_s), sk(vw_s),
            v_s.reshape(db, dseq, A_WIDTH)[None])
```

```python
import functools

import jax
import jax.numpy as jnp
from jax import lax
from jax.experimental import pallas as pl
from jax.experimental.pallas import tpu as pltpu

F32 = jnp.float32
BF16 = jnp.bfloat16

D_MODEL = 1024
A_WIDTH = 512
A_GROUPS = 8
CHUNK = 128
N_HEADS = 8
HEAD_DIM = 64
N_KV = 2
GROUP = N_HEADS // N_KV
B_WIDTH = N_HEADS * HEAD_DIM
KV_WIDTH = N_KV * HEAD_DIM
CMP_BLOCK = 32
CMP_STRIDE = 16
SEL_BLOCK = 64
N_SELECT = 16
WINDOW = 512
ROT_DIM = 16
ROPE_THETA = 500000.0
D_FF = 4096
PAGE_SIZE = 128
EPS = 1e-6
NEG = -1e30
FORCED_BONUS = 1e4
MASK_BIAS = -(2.0 ** 100)
SCALE = HEAD_DIM ** -0.5

LANES = 128
Q_TILE = 128
KEY_TILE = 2048
CHUNK_LANES = CMP_STRIDE * KV_WIDTH
HEAD_PERM = (0, 4, 1, 5, 2, 6, 3, 7)
VMEM_LIMIT = 56 * 1024 * 1024


def _low_half(shape):
    return lax.broadcasted_iota(jnp.int32, shape, len(shape) - 1) < HEAD_DIM


def _head_rms(xs, g):
    lo = _low_half(xs.shape)
    sq = xs * xs
    sa = jnp.sum(jnp.where(lo, sq, 0.0), axis=-1, keepdims=True)
    sb = jnp.sum(jnp.where(lo, 0.0, sq), axis=-1, keepdims=True)
    ms = jnp.where(lo, sa, sb) * (1.0 / HEAD_DIM)
    return xs * lax.rsqrt(ms + EPS) * g


def _rope(xs, cos, sin_up, sin_dn):
    return xs * cos + pltpu.roll(xs, ROT_DIM // 2, 1) * sin_up + pltpu.roll(xs, LANES - ROT_DIM // 2, 1) * sin_dn


def _rms_rows(x, g):
    return x * lax.rsqrt(jnp.mean(x * x, axis=-1, keepdims=True) + EPS) * g


def _in_proj_body(x_ref, g1_ref, w_ref, lng_ref, lnb_ref, gq_ref, gks_ref, gkw_ref,
                  cos_ref, sup_ref, sdn_ref, wmix_ref, bmix_ref,
                  oa_ref, vlast_ref, qn_ref, qr_ref, kc_ref, vc_ref, ks_ref, vs_ref, kw_ref, vw_ref,
                  ks16_ref, vs16_ref, kw16_ref, vw16_ref, gate_ref, *, tm, tiles_per_seg):
    i = pl.program_id(0)
    h = _rms_rows(x_ref[...], g1_ref[...]).astype(BF16)

    def proj(lo, hi):
        return jnp.dot(h, w_ref[:, lo:hi], preferred_element_type=F32)

    u = jax.nn.gelu(proj(0, A_WIDTH))
    v = jax.nn.gelu(proj(A_WIDTH, 2 * A_WIDTH))
    vc = v - jnp.mean(v, axis=-1, keepdims=True)
    vn = vc * lax.rsqrt(jnp.mean(vc * vc, axis=-1, keepdims=True) + EPS) * lng_ref[...] + lnb_ref[...]

    @pl.when(i % tiles_per_seg == tiles_per_seg - 1)
    def _():
        vlast_ref[0] = vn[tm - CHUNK:tm]

    lo = _low_half((CHUNK, LANES))
    for c in range(tm // CHUNK):
        r0, r1 = c * CHUNK, (c + 1) * CHUNK
        for j in range(A_WIDTH // LANES):
            slab = vn[r0:r1, j * LANES:(j + 1) * LANES]
            a = jnp.where(lo, slab, 0.0).astype(BF16)
            b = jnp.where(lo, 0.0, slab).astype(BF16)
            s = (jnp.dot(wmix_ref[2 * j], a, preferred_element_type=F32)
                 + jnp.dot(wmix_ref[2 * j + 1], b, preferred_element_type=F32)
                 + bmix_ref[:, j * LANES:(j + 1) * LANES])
            oa_ref[r0:r1, j * LANES:(j + 1) * LANES] = (u[r0:r1, j * LANES:(j + 1) * LANES] * s).astype(BF16)

    cos, sup, sdn = cos_ref[...], sup_ref[...], sdn_ref[...]
    q0 = 2 * A_WIDTH
    for j in range(B_WIDTH // LANES):
        qn = _head_rms(proj(q0 + j * LANES, q0 + (j + 1) * LANES), gq_ref[...])
        qn_ref[:, j * LANES:(j + 1) * LANES] = qn.astype(BF16)
        qr_ref[:, j * LANES:(j + 1) * LANES] = _rope(qn, cos, sup, sdn).astype(BF16)
    k0 = q0 + B_WIDTH
    kc_ref[...] = proj(k0, k0 + LANES)
    vc_ref[...] = proj(k0 + LANES, k0 + 2 * LANES)
    ks = _rope(_head_rms(proj(k0 + 2 * LANES, k0 + 3 * LANES), gks_ref[...]), cos, sup, sdn)
    ks_ref[...] = ks
    ks16_ref[...] = ks.astype(BF16)
    vs = proj(k0 + 3 * LANES, k0 + 4 * LANES)
    vs_ref[...] = vs
    vs16_ref[...] = vs.astype(BF16)
    kw = _rope(_head_rms(proj(k0 + 4 * LANES, k0 + 5 * LANES), gkw_ref[...]), cos, sup, sdn)
    kw_ref[...] = kw
    kw16_ref[...] = kw.astype(BF16)
    vw = proj(k0 + 5 * LANES, k0 + 6 * LANES)
    vw_ref[...] = vw
    vw16_ref[...] = vw.astype(BF16)
    gate_ref[...] = proj(k0 + 6 * LANES, k0 + 7 * LANES)


def _in_proj(x, wts, tabs, wmix, bmix, *, tm, seg_rows):
    n = x.shape[0]
    tiles_per_seg = seg_rows // tm
    n_seg = n // seg_rows
    row = lambda w: pl.BlockSpec((tm, w), lambda i: (i, 0))
    full = lambda a: pl.BlockSpec(a.shape, lambda i: (0,) * a.ndim)
    cos, sup, sdn = tabs
    ins = [x, wts['g1'], wts['w_a'], wts['ln_g'], wts['ln_b'], wts['g_q'], wts['g_ks'], wts['g_kw'],
           cos, sup, sdn, wmix, bmix]
    in_specs = [row(D_MODEL)] + [full(a) for a in ins[1:8]] + [row(LANES)] * 3 + [full(wmix), full(bmix)]
    f32o = lambda w: jax.ShapeDtypeStruct((n, w), F32)
    b16o = lambda w: jax.ShapeDtypeStruct((n, w), BF16)
    out_shape = ([b16o(A_WIDTH), jax.ShapeDtypeStruct((n_seg, CHUNK, A_WIDTH), F32), b16o(B_WIDTH), b16o(B_WIDTH)]
                 + [f32o(LANES)] * 6 + [b16o(LANES)] * 4 + [f32o(LANES)])
    out_specs = ([row(A_WIDTH), pl.BlockSpec((1, CHUNK, A_WIDTH), lambda i: (i // tiles_per_seg, 0, 0)),
                  row(B_WIDTH), row(B_WIDTH)] + [row(LANES)] * 11)
    return pl.pallas_call(
        functools.partial(_in_proj_body, tm=tm, tiles_per_seg=tiles_per_seg),
        grid=(n // tm,), in_specs=in_specs, out_specs=out_specs, out_shape=out_shape,
        compiler_params=pltpu.CompilerParams(dimension_semantics=("arbitrary",), vmem_limit_bytes=VMEM_LIMIT),
        name="in_proj")(*ins)


COMPRESS_ROWS = 256


def _compress_math(load_rows, m, w1ab_ref, pe_ref, w1d_ref, w2d_ref, g_ref, norm):
    step = min(COMPRESS_ROWS, m)
    f = jnp.concatenate(
        [jnp.dot(load_rows(r, r + step).astype(BF16), w1ab_ref[...], preferred_element_type=F32)
         for r in range(0, m, step)], axis=0)
    bias = jnp.dot(pe_ref[...], w1d_ref[...], preferred_element_type=F32)[0:1]
    second_next = pltpu.roll(f[:, LANES:], m - 1, 0)
    act = jax.nn.gelu(f[:, :LANES] + second_next + bias)
    out = jnp.dot(act.astype(BF16), w2d_ref[...], preferred_element_type=F32)
    if norm:
        out = _head_rms(out, g_ref[...])
    return out.astype(BF16)


def _compress_body(ch_ref, w1ab_ref, pe_ref, w1d_ref, w2d_ref, g_ref, out_ref, *, norm):
    out_ref[0] = _compress_math(lambda r0, r1: ch_ref[0, r0:r1], ch_ref.shape[1],
                                w1ab_ref, pe_ref, w1d_ref, w2d_ref, g_ref, norm)


def _compress(ch, cw, *, norm):
    nb, m, _ = ch.shape
    full = lambda a: pl.BlockSpec(a.shape, lambda b: (0,) * a.ndim)
    ws = [cw['w1ab'], cw['pe'], cw['w1d'], cw['w2d'], cw['g']]
    return pl.pallas_call(
        functools.partial(_compress_body, norm=norm),
        grid=(nb,),
        in_specs=[pl.BlockSpec((1, m, CHUNK_LANES), lambda b: (b, 0, 0))] + [full(a) for a in ws],
        out_specs=pl.BlockSpec((1, m, LANES), lambda b: (b, 0, 0)),
        out_shape=jax.ShapeDtypeStruct((nb, m, LANES), BF16),
        compiler_params=pltpu.CompilerParams(dimension_semantics=("arbitrary",), vmem_limit_bytes=VMEM_LIMIT),
        name="compress_k" if norm else "compress_v")(ch, *ws)


def _page_copy(pool_ref, page, buf_ref, slot, i, sem_ref):
    return pltpu.make_async_copy(pool_ref.at[page], buf_ref.at[slot, i], sem_ref.at[slot])


def _paged_compress_body(pt_ref, pool_ref, w1ab_ref, pe_ref, w1d_ref, w2d_ref, g_ref, out_ref,
                         buf_ref, sem_ref, x_ref, ch_ref, *, n_pages, norm):
    b = pl.program_id(0)

    def fetch(bb, slot):
        def one(i, carry):
            _page_copy(pool_ref, pt_ref[bb * n_pages + i], buf_ref, slot, i, sem_ref).start()
            return carry
        lax.fori_loop(0, n_pages, one, 0)

    @pl.when(b == 0)
    def _():
        fetch(0, 0)

    @pl.when(b + 1 < pl.num_programs(0))
    def _():
        fetch(b + 1, (b + 1) % 2)

    slot = b % 2

    def wait_one(i, carry):
        _page_copy(pool_ref, 0, buf_ref, slot, i, sem_ref).wait()
        return carry
    lax.fori_loop(0, n_pages, wait_one, 0)

    def to_rows(p, carry):
        x_ref[pl.ds(pl.multiple_of(p * PAGE_SIZE, PAGE_SIZE), PAGE_SIZE), :] = buf_ref[slot, p].T
        return carry
    lax.fori_loop(0, n_pages, to_rows, 0, unroll=8)
    m = n_pages * (PAGE_SIZE // CMP_STRIDE)
    for r in range(CMP_STRIDE):
        ch_ref[:, r * KV_WIDTH:(r + 1) * KV_WIDTH] = x_ref[pl.ds(r, m, stride=CMP_STRIDE), :].astype(BF16)
    out_ref[0] = _compress_math(lambda r0, r1: ch_ref[r0:r1], m, w1ab_ref, pe_ref, w1d_ref, w2d_ref, g_ref, norm)


def _paged_compress(pt_flat, pool, cw, *, nb, n_pages, norm):
    m = n_pages * (PAGE_SIZE // CMP_STRIDE)
    full = lambda a: pl.BlockSpec(a.shape, lambda b, pt: (0,) * a.ndim)
    ws = [cw['w1ab'], cw['pe'], cw['w1d'], cw['w2d'], cw['g']]
    grid_spec = pltpu.PrefetchScalarGridSpec(
        num_scalar_prefetch=1, grid=(nb,),
        in_specs=[pl.BlockSpec(memory_space=pl.ANY)] + [full(a) for a in ws],
        out_specs=pl.BlockSpec((1, m, LANES), lambda b, pt: (b, 0, 0)),
        scratch_shapes=[pltpu.VMEM((2, n_pages, KV_WIDTH, PAGE_SIZE), F32), pltpu.SemaphoreType.DMA((2,)),
                        pltpu.VMEM((n_pages * PAGE_SIZE, KV_WIDTH), F32), pltpu.VMEM((m, CHUNK_LANES), BF16)])
    return pl.pallas_call(
        functools.partial(_paged_compress_body, n_pages=n_pages, norm=norm),
        grid_spec=grid_spec, out_shape=jax.ShapeDtypeStruct((nb, m, LANES), BF16),
        compiler_params=pltpu.CompilerParams(dimension_semantics=("arbitrary",), vmem_limit_bytes=VMEM_LIMIT),
        name="paged_compress_k" if norm else "paged_compress_v")(pt_flat, pool, *ws)


def _split3(x):
    hi = x.astype(BF16)
    r1 = x - hi.astype(F32)
    mid = r1.astype(BF16)
    lo = (r1 - mid.astype(F32)).astype(BF16)
    return hi, mid, lo


def _dot_nt(a, b):
    return lax.dot_general(a, b, (((1,), (1,)), ((), ())), preferred_element_type=F32)


def _topk_mask_t(rank, n_pick):
    n_blk = rank.shape[0]
    jidx = lax.broadcasted_iota(jnp.int32, rank.shape, 0).astype(F32)

    def body(_, carry):
        r, sel = carry
        m = jnp.max(r, axis=0, keepdims=True)
        first = jnp.min(jnp.where(r == m, jidx, float(n_blk)), axis=0, keepdims=True)
        pick = jidx == first
        return jnp.where(pick, -jnp.inf, r), jnp.where(pick, 1.0, sel)

    _, sel = lax.fori_loop(0, n_pick, body, (rank, jnp.zeros_like(rank)))
    return sel


def _masked_softmax(s, mask):
    s = jnp.where(mask, s, NEG)
    e = jnp.exp(s - jnp.max(s, axis=-1, keepdims=True))
    return e / jnp.sum(e, axis=-1, keepdims=True) * mask.astype(F32)


def _gate_col(gt, head, branch):
    c = head * 3 + branch
    return gt[:, c:c + 1]


def _prompt_attn_body(qn_ref, qr_ref, gate_ref, kc_ref, vc_ref, ks_ref, vs_ref, kw_ref, vw_ref, eb_ref, at_ref,
                      o_ref, notsel_ref, m_ref, l_ref, acc_ref, *, seq, n_cmp):
    i = pl.program_id(1)
    s0 = i * Q_TILE
    nc_pad = kc_ref.shape[1]
    n_sel = seq // SEL_BLOCK
    lo = _low_half((Q_TILE, LANES))
    pos_col = s0 + lax.broadcasted_iota(jnp.int32, (Q_TILE, 1), 0)

    def group_q(q_ref, kv):
        keep = lo if kv == 0 else jnp.logical_not(lo)
        parts = [jnp.where(keep, q_ref[0, :, g * LANES:(g + 1) * LANES], 0) for g in range(GROUP)]
        return jnp.concatenate(parts, axis=0)

    cidx = lax.broadcasted_iota(jnp.int32, (Q_TILE, nc_pad), 1)
    m_c = (cidx * CMP_STRIDE + (CMP_BLOCK - 1) <= pos_col) & (cidx < n_cmp)
    jrow = lax.broadcasted_iota(jnp.int32, (n_sel, Q_TILE), 0)
    pos_row = s0 + lax.broadcasted_iota(jnp.int32, (n_sel, Q_TILE), 1)
    cur = pos_row >> 6
    forced = (jrow == 0) | (jrow == cur) | (jrow == cur - 1)
    valid_b = jrow * SEL_BLOCK <= pos_row
    kc = kc_ref[0]
    vc = vc_ref[0]
    o_cmp = []
    for kv in range(N_KV):
        s = (_dot_nt(group_q(qn_ref, kv), kc) * SCALE).reshape(GROUP, Q_TILE, nc_pad)
        p = _masked_softmax(s, m_c[None])
        o_cmp.append(jnp.dot(p.reshape(GROUP * Q_TILE, nc_pad).astype(BF16), vc, preferred_element_type=F32))
        imp = jnp.sum(p, axis=0)
        blk_t = sum(_dot_nt(at_ref[...], part) for part in _split3(imp))
        rank = jnp.where(forced, FORCED_BONUS, jnp.where(valid_b, blk_t, -FORCED_BONUS))
        sel_t = _topk_mask_t(rank, min(N_SELECT, n_sel))
        notsel_ref[kv] = (1.0 - sel_t.T).astype(BF16)

    m_ref[...] = jnp.full(m_ref.shape, NEG, F32)
    l_ref[...] = jnp.zeros(l_ref.shape, F32)
    acc_ref[...] = jnp.zeros(acc_ref.shape, F32)
    q_sel = [group_q(qr_ref, kv) * SCALE for kv in range(N_KV)]
    q_aug = [jnp.concatenate([q_sel[kv], jnp.concatenate([notsel_ref[kv]] * GROUP, axis=0)], axis=1)
             for kv in range(N_KV)]

    def tile_step(tix, diagonal):
        k0 = pl.multiple_of(tix * KEY_TILE, KEY_TILE)
        kt = jnp.concatenate([ks_ref[0, pl.ds(k0, KEY_TILE), :], eb_ref[pl.ds(k0, KEY_TILE), :]], axis=1)
        vt = vs_ref[0, pl.ds(k0, KEY_TILE), :]
        if diagonal:
            tok = lax.broadcasted_iota(jnp.int32, (GROUP * Q_TILE, KEY_TILE), 0) & (Q_TILE - 1)
            key = k0 + lax.broadcasted_iota(jnp.int32, (GROUP * Q_TILE, KEY_TILE), 1)
            future = jnp.where(key <= s0 + tok, 0.0, NEG)
        for kv in range(N_KV):
            s = _dot_nt(q_aug[kv], kt)
            if diagonal:
                s = s + future
            m_old = m_ref[kv]
            m_new = jnp.maximum(m_old, jnp.max(s, axis=-1, keepdims=True))
            alpha = jnp.exp(m_old - m_new)
            p = jnp.exp(s - m_new)
            l_ref[kv] = alpha * l_ref[kv] + jnp.sum(p, axis=-1, keepdims=True)
            acc_ref[kv] = alpha * acc_ref[kv] + jnp.dot(p.astype(BF16), vt, preferred_element_type=F32)
            m_ref[kv] = m_new

    def tile_body(tix, carry):
        tile_step(tix, False)
        return carry

    q_per_key = KEY_TILE // Q_TILE
    last_tile = lax.shift_right_logical(i, q_per_key.bit_length() - 1)
    lax.fori_loop(0, last_tile, tile_body, 0)
    tile_step(last_tile, True)

    n_win = WINDOW + Q_TILE
    w0 = pl.multiple_of(jnp.maximum(s0 - WINDOW, 0), Q_TILE)
    kwt = kw_ref[0, pl.ds(w0, n_win), :]
    vwt = vw_ref[0, pl.ds(w0, n_win), :]
    dpos = pos_col - (w0 + lax.broadcasted_iota(jnp.int32, (Q_TILE, n_win), 1))
    m_w = (dpos >= 0) & (dpos < WINDOW)
    gt = jax.nn.sigmoid(gate_ref[0])
    o_heads = []
    for kv in range(N_KV):
        s = _dot_nt(q_sel[kv], kwt).reshape(GROUP, Q_TILE, n_win)
        p = _masked_softmax(s, m_w[None])
        o_win = jnp.dot(p.reshape(GROUP * Q_TILE, n_win).astype(BF16), vwt, preferred_element_type=F32)
        o_sel = acc_ref[kv] / l_ref[kv]
        for g in range(GROUP):
            hd = kv * GROUP + g
            rows = slice(g * Q_TILE, (g + 1) * Q_TILE)
            o_heads.append(_gate_col(gt, hd, 0) * o_cmp[kv][rows] + _gate_col(gt, hd, 1) * o_sel[rows]
                           + _gate_col(gt, hd, 2) * o_win[rows])
    for j in range(GROUP):
        o_ref[0, :, j * LANES:(j + 1) * LANES] = jnp.where(lo, o_heads[j], o_heads[GROUP + j]).astype(BF16)


def _prompt_attn(qn, qr, gate, kc, vc, ks16, vs16, kw16, vw16, eb, a_t, *, n_cmp):
    nb, seq, _ = qn.shape
    n_sel = seq // SEL_BLOCK
    qspec = lambda w: pl.BlockSpec((1, Q_TILE, w), lambda b, i: (b, i, 0))
    seqspec = lambda a: pl.BlockSpec((1,) + a.shape[1:], lambda b, i: (b, 0, 0))
    full = lambda a: pl.BlockSpec(a.shape, lambda b, i: (0,) * a.ndim)
    return pl.pallas_call(
        functools.partial(_prompt_attn_body, seq=seq, n_cmp=n_cmp),
        grid=(nb, seq // Q_TILE),
        in_specs=[qspec(B_WIDTH), qspec(B_WIDTH), qspec(LANES), seqspec(kc), seqspec(vc),
                  seqspec(ks16), seqspec(vs16), seqspec(kw16), seqspec(vw16), full(eb), full(a_t)],
        out_specs=qspec(B_WIDTH),
        out_shape=jax.ShapeDtypeStruct((nb, seq, B_WIDTH), BF16),
        scratch_shapes=[pltpu.VMEM((N_KV, Q_TILE, n_sel), BF16),
                        pltpu.VMEM((N_KV, GROUP * Q_TILE, 1), F32),
                        pltpu.VMEM((N_KV, GROUP * Q_TILE, 1), F32),
                        pltpu.VMEM((N_KV, GROUP * Q_TILE, LANES), F32)],
        compiler_params=pltpu.CompilerParams(dimension_semantics=("arbitrary", "arbitrary"),
                                             vmem_limit_bytes=VMEM_LIMIT),
        name="prompt_attn")(qn, qr, gate, kc, vc, ks16, vs16, kw16, vw16, eb, a_t)


TOK_PAD = 8
ROWS = N_HEADS * TOK_PAD
ROWS_PAD = 128


def _page_copy_lanes(pool_ref, page, buf_ref, slot, i, sem_ref):
    return pltpu.make_async_copy(pool_ref.at[page], buf_ref.at[slot, :, pl.ds(i * PAGE_SIZE, PAGE_SIZE)],
                                 sem_ref.at[slot])


def _sample_attn_body(pt_ref, qn_ref, qr_ref, gate_ref, kc_ref, vc_ref, kwc_ref, vwc_ref,
                      ksn_ref, vsn_ref, kwn_ref, vwn_ref, at_ref, gsum_ref, ebt_ref, kpool_ref, vpool_ref,
                      o_ref, kbuf_ref, vbuf_ref, ksem_ref, vsem_ref, notsel_ref, m_ref, l_ref, acc_ref, ocw_ref,
                      q_ref, *, n_chunks, pages_per_chunk, past, n_new, n_cmp):
    b = pl.program_id(0)
    c = pl.program_id(1)
    step = b * n_chunks + c
    n_steps = pl.num_programs(0) * n_chunks
    n_past_blk = past // SEL_BLOCK
    blk_per_chunk = n_past_blk // n_chunks

    def fetch(st, slot):
        def one(i, carry):
            page = pt_ref[st * pages_per_chunk + i]
            _page_copy_lanes(kpool_ref, page, kbuf_ref, slot, i, ksem_ref).start()
            _page_copy_lanes(vpool_ref, page, vbuf_ref, slot, i, vsem_ref).start()
            return carry
        lax.fori_loop(0, pages_per_chunk, one, 0)

    @pl.when(step == 0)
    def _():
        fetch(0, 0)

    @pl.when(step + 1 < n_steps)
    def _():
        fetch(step + 1, (step + 1) % 2)

    lo = _low_half((TOK_PAD, LANES))
    tok_col = lax.broadcasted_iota(jnp.int32, (ROWS, 1), 0) % TOK_PAD
    pos_col = past + tok_col

    def rows_q(ref):
        parts = []
        for hd in range(N_HEADS):
            keep = lo if hd < GROUP else jnp.logical_not(lo)
            j = hd % GROUP
            parts.append(jnp.where(keep, ref[0, :, j * LANES:(j + 1) * LANES], 0))
        return jnp.concatenate(parts, axis=0) * SCALE

    def gate_rows(gt, branch):
        return jnp.concatenate([_gate_col(gt, hd, branch) for hd in range(N_HEADS)], axis=0)

    @pl.when(c == 0)
    def _():
        gt = jax.nn.sigmoid(gate_ref[0])
        nc_pad = kc_ref.shape[1]
        cidx = lax.broadcasted_iota(jnp.int32, (ROWS, nc_pad), 1)
        m_c = (cidx * CMP_STRIDE + (CMP_BLOCK - 1) <= pos_col) & (cidx < n_cmp)
        p = _masked_softmax(_dot_nt(rows_q(qn_ref), kc_ref[0]), m_c)
        o_c = jnp.dot(p.astype(BF16), vc_ref[0], preferred_element_type=F32)
        imp = sum(jnp.dot(gsum_ref[...], part, preferred_element_type=F32) for part in _split3(p))
        blk_t = sum(_dot_nt(at_ref[...], part) for part in _split3(imp))
        jrow = lax.broadcasted_iota(jnp.int32, blk_t.shape, 0)
        rank = jnp.where((jrow == 0) | (jrow == n_past_blk - 1), FORCED_BONUS, blk_t)
        n_top = min(N_SELECT, n_past_blk + 1)
        sel_t = _topk_mask_t(rank, n_top - 1)
        for cc in range(n_chunks):
            blk = sel_t[cc * blk_per_chunk:(cc + 1) * blk_per_chunk]
            notsel_ref[cc] = (1.0 - blk.T[:ROWS]).astype(BF16)
        q_r = rows_q(qr_ref)
        q_ref[...] = q_r
        n_buf = kwc_ref.shape[2]
        kidx = lax.broadcasted_iota(jnp.int32, (ROWS, n_buf), 1)
        dpos = pos_col - (past - n_buf + kidx)
        m_old = (dpos >= 0) & (dpos < WINDOW)
        knew = lax.broadcasted_iota(jnp.int32, (ROWS, TOK_PAD), 1)
        m_new = (knew <= tok_col) & (knew < n_new)
        s_old = jnp.where(m_old, jnp.dot(q_r, kwc_ref[0].astype(BF16), preferred_element_type=F32), NEG)
        s_new = jnp.where(m_new, _dot_nt(q_r, kwn_ref[0]), NEG)
        mx = jnp.maximum(jnp.max(s_old, axis=-1, keepdims=True), jnp.max(s_new, axis=-1, keepdims=True))
        e_old = jnp.exp(s_old - mx)
        e_new = jnp.exp(s_new - mx)
        den = jnp.sum(e_old, axis=-1, keepdims=True) + jnp.sum(e_new, axis=-1, keepdims=True)
        p_old = e_old / den * m_old.astype(F32)
        p_new = e_new / den * m_new.astype(F32)
        o_w = (_dot_nt(p_old.astype(BF16), vwc_ref[0].astype(BF16))
               + jnp.dot(p_new.astype(BF16), vwn_ref[0], preferred_element_type=F32))
        ocw_ref[...] = gate_rows(gt, 0) * o_c + gate_rows(gt, 2) * o_w
        s = jnp.where(m_new, _dot_nt(q_r, ksn_ref[0]), NEG)
        m0 = jnp.max(s, axis=-1, keepdims=True)
        e = jnp.exp(s - m0) * m_new.astype(F32)
        m_ref[...] = m0
        l_ref[...] = jnp.sum(e, axis=-1, keepdims=True)
        acc_ref[...] = jnp.dot(e.astype(BF16), vsn_ref[0], preferred_element_type=F32)

    slot = step % 2

    def wait_one(i, carry):
        _page_copy_lanes(kpool_ref, 0, kbuf_ref, slot, i, ksem_ref).wait()
        _page_copy_lanes(vpool_ref, 0, vbuf_ref, slot, i, vsem_ref).wait()
        return carry
    lax.fori_loop(0, pages_per_chunk, wait_one, 0)

    q_aug = jnp.concatenate([q_ref[...], notsel_ref[c]], axis=1)
    k_aug = jnp.concatenate([kbuf_ref[slot].astype(BF16), ebt_ref[...]], axis=0)
    s = jnp.dot(q_aug, k_aug, preferred_element_type=F32)
    m_old = m_ref[...]
    m_new = jnp.maximum(m_old, jnp.max(s, axis=-1, keepdims=True))
    alpha = jnp.exp(m_old - m_new)
    p = jnp.exp(s - m_new)
    l_ref[...] = alpha * l_ref[...] + jnp.sum(p, axis=-1, keepdims=True)
    acc_ref[...] = alpha * acc_ref[...] + _dot_nt(p.astype(BF16), vbuf_ref[slot].astype(BF16))
    m_ref[...] = m_new

    @pl.when(c == n_chunks - 1)
    def _():
        gt = jax.nn.sigmoid(gate_ref[0])
        o = ocw_ref[...] + gate_rows(gt, 1) * (acc_ref[...] / l_ref[...])
        for j in range(GROUP):
            top = o[j * TOK_PAD:(j + 1) * TOK_PAD]
            bot = o[(GROUP + j) * TOK_PAD:(GROUP + j + 1) * TOK_PAD]
            o_ref[0, :, j * LANES:(j + 1) * LANES] = jnp.where(lo, top, bot).astype(BF16)


def _sample_attn(pt_flat, qn, qr, gate, kc, vc, kwc, vwc, ksn, vsn, kwn, vwn, a_t, gsum, ebt, kpool, vpool,
                 *, n_chunks, pages_per_chunk, past, n_new, n_cmp):
    nb = qn.shape[0]
    chunk_keys = pages_per_chunk * PAGE_SIZE
    blk_per_chunk = chunk_keys // SEL_BLOCK
    bspec = lambda a: pl.BlockSpec((1,) + a.shape[1:], lambda b, c, pt: (b, 0, 0))
    full = lambda a: pl.BlockSpec(a.shape, lambda b, c, pt: (0,) * a.ndim)
    anyspec = pl.BlockSpec(memory_space=pl.ANY)
    blocked = [qn, qr, gate, kc, vc, kwc, vwc, ksn, vsn, kwn, vwn]
    grid_spec = pltpu.PrefetchScalarGridSpec(
        num_scalar_prefetch=1, grid=(nb, n_chunks),
        in_specs=[bspec(a) for a in blocked] + [full(a_t), full(gsum), full(ebt), anyspec, anyspec],
        out_specs=pl.BlockSpec((1, TOK_PAD, B_WIDTH), lambda b, c, pt: (b, 0, 0)),
        scratch_shapes=[pltpu.VMEM((2, KV_WIDTH, chunk_keys), F32), pltpu.VMEM((2, KV_WIDTH, chunk_keys), F32),
                        pltpu.SemaphoreType.DMA((2,)), pltpu.SemaphoreType.DMA((2,)),
                        pltpu.VMEM((n_chunks, ROWS, blk_per_chunk), BF16),
                        pltpu.VMEM((ROWS, 1), F32), pltpu.VMEM((ROWS, 1), F32), pltpu.VMEM((ROWS, LANES), F32),
                        pltpu.VMEM((ROWS, LANES), F32), pltpu.VMEM((ROWS, LANES), BF16)])
    return pl.pallas_call(
        functools.partial(_sample_attn_body, n_chunks=n_chunks, pages_per_chunk=pages_per_chunk, past=past,
                          n_new=n_new, n_cmp=n_cmp),
        grid_spec=grid_spec, out_shape=jax.ShapeDtypeStruct((nb, TOK_PAD, B_WIDTH), BF16),
        compiler_params=pltpu.CompilerParams(dimension_semantics=("arbitrary", "arbitrary"),
                                             vmem_limit_bytes=VMEM_LIMIT),
        name="sample_attn")(pt_flat, *blocked, a_t, gsum, ebt, kpool, vpool)


FF_TILE = 1024


def _merge_ffn_body(x_ref, oa_ref, ob_ref, g1_ref, wmg_ref, wba_ref, wbb_ref, wout_ref, g2_ref, wup_ref, wdn_ref,
                    y_ref):
    x = x_ref[...]
    h = _rms_rows(x, g1_ref[...]).astype(BF16)
    y_a = jnp.dot(oa_ref[...], wba_ref[...], preferred_element_type=F32)
    y_b = jnp.dot(ob_ref[...], wbb_ref[...], preferred_element_type=F32)
    g_a = jax.nn.sigmoid(jnp.dot(h, wmg_ref[:, :D_MODEL], preferred_element_type=F32))
    g_b = jax.nn.sigmoid(jnp.dot(h, wmg_ref[:, D_MODEL:], preferred_element_type=F32))
    x1 = x + jnp.dot((g_a * y_a + g_b * y_b).astype(BF16), wout_ref[...], preferred_element_type=F32)
    h2 = _rms_rows(x1, g2_ref[...]).astype(BF16)
    y = x1
    for f in range(D_FF // FF_TILE):
        up = jnp.dot(h2, wup_ref[:, f * FF_TILE:(f + 1) * FF_TILE], preferred_element_type=F32)
        act = jnp.square(jnp.maximum(up, 0.0)).astype(BF16)
        y = y + jnp.dot(act, wdn_ref[f * FF_TILE:(f + 1) * FF_TILE, :], preferred_element_type=F32)
    y_ref[...] = y


def _merge_ffn(x, o_a, o_b, wts, *, tm):
    n = x.shape[0]
    row = lambda w: pl.BlockSpec((tm, w), lambda i: (i, 0))
    const = lambda a: pl.BlockSpec(a.shape, lambda i: (0,) * a.ndim, pipeline_mode=pl.Buffered(1))
    ws = [wts['g1'], wts['w_mg'], wts['w_br_a'], wts['w_br_b'], wts['w_out'], wts['g2'], wts['w_up'], wts['w_down']]
    return pl.pallas_call(
        _merge_ffn_body, grid=(n // tm,),
        in_specs=[row(D_MODEL), row(A_WIDTH), row(B_WIDTH)] + [const(a) for a in ws],
        out_specs=row(D_MODEL), out_shape=jax.ShapeDtypeStruct((n, D_MODEL), F32),
        compiler_params=pltpu.CompilerParams(dimension_semantics=("arbitrary",), vmem_limit_bytes=VMEM_LIMIT),
        name="merge_ffn")(x, o_a, o_b, *ws)


def _rope_tables(pos):
    inv = ROPE_THETA ** (-jnp.arange(0, ROT_DIM, 2, dtype=F32) / ROT_DIM)
    ang = pos.astype(F32)[:, None] * inv[None, :]
    cos, sin = jnp.cos(ang), jnp.sin(ang)
    half = ROT_DIM // 2
    n = pos.shape[0]
    pad = jnp.zeros((n, HEAD_DIM - ROT_DIM), F32)
    zero = jnp.zeros((n, half), F32)
    cos_h = jnp.concatenate([cos, cos, pad + 1.0], axis=1)
    sup_h = jnp.concatenate([zero, sin, pad], axis=1)
    sdn_h = jnp.concatenate([-sin, zero, pad], axis=1)
    two = lambda t: jnp.concatenate([t, t], axis=1)
    return two(cos_h), two(sup_h), two(sdn_h)


def _two(g):
    return jnp.concatenate([g, g])[None, :].astype(F32)


def _layer_weights(l, g_norm1, w_in, ln_v_g, ln_v_b, g_q, g_ks, g_kw, w_branch, w_out, g_norm2, w_up, w_down):
    w = w_in[l]
    q0, k0 = 2 * A_WIDTH, 2 * A_WIDTH + B_WIDTH
    g0 = k0 + 6 * KV_WIDTH
    perm = jnp.array(HEAD_PERM)
    q_cols = w[:, q0:k0].reshape(D_MODEL, N_HEADS, HEAD_DIM)[:, perm].reshape(D_MODEL, B_WIDTH)
    gate_cols = jnp.pad(w[:, g0:g0 + 3 * N_HEADS], ((0, 0), (0, LANES - 3 * N_HEADS)))
    w_a = jnp.concatenate([w[:, :q0], q_cols, w[:, k0:g0], gate_cols], axis=1).astype(BF16)
    wb = w_branch[l]
    w_br_b = wb[A_WIDTH:].reshape(N_HEADS, HEAD_DIM, D_MODEL)[perm].reshape(B_WIDTH, D_MODEL)
    return dict(
        g1=g_norm1[l][None, :], w_a=w_a, ln_g=ln_v_g[l][None, :], ln_b=ln_v_b[l][None, :],
        g_q=_two(g_q[l]), g_ks=_two(g_ks[l]), g_kw=_two(g_kw[l]),
        w_mg=w[:, g0 + 3 * N_HEADS:].astype(BF16), w_br_a=wb[:A_WIDTH].astype(BF16), w_br_b=w_br_b.astype(BF16),
        w_out=w_out[l].astype(BF16), g2=g_norm2[l][None, :], w_up=w_up[l].astype(BF16),
        w_down=w_down[l].astype(BF16))


def _compress_weights(w1, w2, pe, g):
    half = CMP_STRIDE * HEAD_DIM
    eye = jnp.eye(N_KV, dtype=F32)

    def spread(wh):
        return jnp.einsum('rde,kl->rkdle', wh.reshape(CMP_STRIDE, HEAD_DIM, HEAD_DIM), eye).reshape(
            CHUNK_LANES, KV_WIDTH)

    w1ab = jnp.concatenate([spread(w1[:half]), spread(w1[half:])], axis=1).astype(BF16)
    pe_rows = jnp.zeros((8, CMP_BLOCK * HEAD_DIM), F32).at[0].set(pe.reshape(-1)).astype(BF16)
    w1d = jnp.concatenate([w1, w1], axis=1).astype(BF16)
    w2d = jnp.einsum('de,kl->kdle', w2, eye).reshape(KV_WIDTH, KV_WIDTH).astype(BF16)
    return dict(w1ab=w1ab, pe=pe_rows, w1d=w1d, w2d=w2d, g=_two(g))


def _block_score_matrix(n_blk, n_chunk):
    j = jnp.arange(n_blk)[:, None]
    c = jnp.arange(n_chunk)[None, :]
    per = SEL_BLOCK // CMP_STRIDE
    a = ((c >= per * j) & (c <= per * j + per - 1)).astype(F32) + ((c >= per * j - 1) & (c <= per * j + per - 2))
    return a.astype(BF16)


def kernel(x_prompt, x_sample, cache_k_cmp, cache_v_cmp, cache_k_sel, cache_v_sel, cache_k_win, cache_v_win,
           page_table, g_norm1, w_in, ln_v_g, ln_v_b, w_s, b_s, g_q, g_kc, g_ks, g_kw, w_ck1, w_ck2, pe_k,
           w_cv1, w_cv2, pe_v, w_branch, w_out, g_norm2, w_up, w_down):
    nb, seq, _ = x_prompt.shape
    db, dseq, _ = x_sample.shape
    depth = w_in.shape[0]
    n_pages = page_table.shape[1]
    past = n_pages * PAGE_SIZE
    assert depth == 1 and seq % KEY_TILE == 0 and seq >= WINDOW + Q_TILE and dseq <= TOK_PAD
    assert past % (SEL_BLOCK * LANES) == 0 and (past + dseq) // CMP_STRIDE == past // CMP_STRIDE
    l = 0
    wts = _layer_weights(l, g_norm1, w_in, ln_v_g, ln_v_b, g_q, g_ks, g_kw, w_branch, w_out, g_norm2, w_up, w_down)
    cw_k = _compress_weights(w_ck1[l], w_ck2[l], pe_k[l], g_kc[l])
    cw_v = _compress_weights(w_cv1[l], w_cv2[l], pe_v[l], g_kc[l])
    causal = jnp.tril(jnp.ones((CHUNK, CHUNK), dtype=bool))
    ws_tril = jnp.where(causal[None], w_s[l], 0.0)
    heads = lambda a, lead: a.reshape(lead + (N_KV, HEAD_DIM))

    n_p = nb * seq
    xp = x_prompt.reshape(n_p, D_MODEL)
    tabs_p = _rope_tables(jnp.tile(jnp.arange(seq, dtype=jnp.int32), nb))
    bmix_p = jnp.repeat(b_s[l].T, A_WIDTH // A_GROUPS, axis=1)
    (oa_p, vlast_p, qn_p, qr_p, kc_p, vc_p, ks_p, vs_p, kw_p, vw_p, ks16, vs16, kw16, vw16, gate_p) = _in_proj(
        xp, wts, tabs_p, ws_tril.astype(BF16), bmix_p, tm=256, seg_rows=seq)
    n_chunk_p = seq // CMP_STRIDE
    kcmp_p = _compress(kc_p.reshape(nb, n_chunk_p, CHUNK_LANES), cw_k, norm=True)
    vcmp_p = _compress(vc_p.reshape(nb, n_chunk_p, CHUNK_LANES), cw_v, norm=False)
    n_sel_p = seq // SEL_BLOCK
    kblk = jnp.arange(seq) // SEL_BLOCK
    eb = jnp.where(kblk[:, None] == jnp.arange(n_sel_p)[None, :], MASK_BIAS, 0.0).astype(BF16)
    a_t_p = _block_score_matrix(n_sel_p, n_chunk_p)
    b3 = lambda a: a.reshape(nb, seq, a.shape[-1])
    ob_p = _prompt_attn(b3(qn_p), b3(qr_p), b3(gate_p), kcmp_p, vcmp_p, b3(ks16), b3(vs16), b3(kw16), b3(vw16),
                        eb, a_t_p, n_cmp=n_chunk_p - 1)
    y_p = _merge_ffn(xp, oa_p, ob_p.reshape(n_p, B_WIDTH), wts, tm=512).reshape(nb, seq, D_MODEL)

    n_s = db * dseq
    xs = x_sample.reshape(n_s, D_MODEL)
    tabs_s = _rope_tables(jnp.tile(past + jnp.arange(dseq, dtype=jnp.int32), db))
    wmix_s = jnp.einsum('gpr,bc->gbpcr', ws_tril[:, :dseq, :dseq], jnp.eye(db, dtype=F32)).reshape(
        A_GROUPS, n_s, n_s)
    bmix_s = jnp.tile(bmix_p[:dseq], (db, 1))
    (oa_s, v_s, qn_s, qr_s, kc_s, vc_s, ks_s, vs_s, kw_s, vw_s, ks16s, vs16s, kw16s, vw16s, gate_s) = _in_proj(
        xs, wts, tabs_s, wmix_s.astype(BF16), bmix_s, tm=n_s, seg_rows=n_s)
    pt_flat = page_table.reshape(-1)
    keys_last = lambda a: a.transpose(0, 2, 3, 1).reshape(a.shape[0], KV_WIDTH, a.shape[1])
    kcmp_s = _paged_compress(pt_flat, keys_last(cache_k_cmp[l]), cw_k, nb=db, n_pages=n_pages, norm=True)
    vcmp_s = _paged_compress(pt_flat, keys_last(cache_v_cmp[l]), cw_v, nb=db, n_pages=n_pages, norm=False)
    n_chunk_s = past // CMP_STRIDE
    tokpad = lambda a: jnp.pad(a.reshape(db, dseq, a.shape[-1]), ((0, 0), (0, TOK_PAD - dseq), (0, 0)))
    hh = jnp.arange(ROWS_PAD) // TOK_PAD
    tt = jnp.arange(ROWS_PAD) % TOK_PAD
    gsum = ((hh[:, None] // GROUP == hh[None, :ROWS] // GROUP) & (tt[:, None] == tt[None, :ROWS])
            & (hh[:, None] < N_HEADS)).astype(BF16)
    n_chunks = 2
    chunk_keys = past // n_chunks
    ebt = jnp.where(jnp.arange(chunk_keys)[None, :] // SEL_BLOCK == jnp.arange(chunk_keys // SEL_BLOCK)[:, None],
                    MASK_BIAS, 0.0).astype(BF16)
    ob_s = _sample_attn(
        pt_flat, tokpad(qn_s), tokpad(qr_s), tokpad(gate_s), kcmp_s, vcmp_s,
        keys_last(cache_k_win[l]), keys_last(cache_v_win[l]),
        tokpad(ks16s), tokpad(vs16s), tokpad(kw16s), tokpad(vw16s),
        _block_score_matrix(past // SEL_BLOCK, n_chunk_s), gsum, ebt,
        keys_last(cache_k_sel[l]), keys_last(cache_v_sel[l]),
        n_chunks=n_chunks, pages_per_chunk=n_pages // n_chunks, past=past, n_new=dseq, n_cmp=n_chunk_s - 1)
    y_s = _merge_ffn(xs, oa_s, ob_s[:, :dseq].reshape(n_s, B_WIDTH), wts, tm=n_s).reshape(db, dseq, D_MODEL)

    w_p = min(WINDOW, seq)
    pk = lambda a: heads(a, (nb, seq))[None]
    sk = lambda a: heads(a, (db, dseq))[None]
    return (y_p, y_s, pk(kc_p), pk(vc_p), pk(ks_p), pk(vs_p), pk(kw_p)[:, :, seq - w_p:], pk(vw_p)[:, :, seq - w_p:],
            vlast_p[None], sk(kc_s), sk(vc_s), sk(ks_s), sk(vs_s), sk(kw_s), sk(vw_s),
            v_s.reshape(db, dseq, A_WIDTH)[None])
```

```python
import functools

import jax
import jax.numpy as jnp
from jax import lax
from jax.experimental import pallas as pl
from jax.experimental.pallas import tpu as pltpu

F32 = jnp.float32
BF16 = jnp.bfloat16

D_MODEL = 1024
A_WIDTH = 512
A_GROUPS = 8
CHUNK = 128
N_HEADS = 8
HEAD_DIM = 64
N_KV = 2
GROUP = N_HEADS // N_KV
B_WIDTH = N_HEADS * HEAD_DIM
KV_WIDTH = N_KV * HEAD_DIM
CMP_BLOCK = 32
CMP_STRIDE = 16
SEL_BLOCK = 64
N_SELECT = 16
WINDOW = 512
ROT_DIM = 16
ROPE_THETA = 500000.0
D_FF = 4096
PAGE_SIZE = 128
EPS = 1e-6
NEG = -1e30
FORCED_BONUS = 1e4
MASK_BIAS = -(2.0 ** 100)
SCALE = HEAD_DIM ** -0.5

LANES = 128
Q_TILE = 128
KEY_TILE = 2048
CHUNK_LANES = CMP_STRIDE * KV_WIDTH
HEAD_PERM = (0, 4, 1, 5, 2, 6, 3, 7)
VMEM_LIMIT = 56 * 1024 * 1024


def _low_half(shape):
    return lax.broadcasted_iota(jnp.int32, shape, len(shape) - 1) < HEAD_DIM


def _head_rms(xs, g):
    lo = _low_half(xs.shape)
    sq = xs * xs
    sa = jnp.sum(jnp.where(lo, sq, 0.0), axis=-1, keepdims=True)
    sb = jnp.sum(jnp.where(lo, 0.0, sq), axis=-1, keepdims=True)
    ms = jnp.where(lo, sa, sb) * (1.0 / HEAD_DIM)
    return xs * lax.rsqrt(ms + EPS) * g


def _rope(xs, cos, sin_up, sin_dn):
    return xs * cos + pltpu.roll(xs, ROT_DIM // 2, 1) * sin_up + pltpu.roll(xs, LANES - ROT_DIM // 2, 1) * sin_dn


def _rms_rows(x, g):
    return x * lax.rsqrt(jnp.mean(x * x, axis=-1, keepdims=True) + EPS) * g


def _in_proj_body(x_ref, g1_ref, w_ref, lng_ref, lnb_ref, gq_ref, gks_ref, gkw_ref,
                  cos_ref, sup_ref, sdn_ref, wmix_ref, bmix_ref,
                  oa_ref, vlast_ref, qn_ref, qr_ref, kc_ref, vc_ref, ks_ref, vs_ref, kw_ref, vw_ref,
                  ks16_ref, vs16_ref, kw16_ref, vw16_ref, gate_ref, *, tm, tiles_per_seg):
    i = pl.program_id(0)
    h = _rms_rows(x_ref[...], g1_ref[...]).astype(BF16)

    def proj(lo, hi):
        return jnp.dot(h, w_ref[:, lo:hi], preferred_element_type=F32)

    u = jax.nn.gelu(proj(0, A_WIDTH))
    v = jax.nn.gelu(proj(A_WIDTH, 2 * A_WIDTH))
    vc = v - jnp.mean(v, axis=-1, keepdims=True)
    vn = vc * lax.rsqrt(jnp.mean(vc * vc, axis=-1, keepdims=True) + EPS) * lng_ref[...] + lnb_ref[...]

    @pl.when(i % tiles_per_seg == tiles_per_seg - 1)
    def _():
        vlast_ref[0] = vn[tm - CHUNK:tm]

    lo = _low_half((CHUNK, LANES))
    for c in range(tm // CHUNK):
        r0, r1 = c * CHUNK, (c + 1) * CHUNK
        for j in range(A_WIDTH // LANES):
            slab = vn[r0:r1, j * LANES:(j + 1) * LANES]
            a = jnp.where(lo, slab, 0.0).astype(BF16)
            b = jnp.where(lo, 0.0, slab).astype(BF16)
            s = (jnp.dot(wmix_ref[2 * j], a, preferred_element_type=F32)
                 + jnp.dot(wmix_ref[2 * j + 1], b, preferred_element_type=F32)
                 + bmix_ref[:, j * LANES:(j + 1) * LANES])
            oa_ref[r0:r1, j * LANES:(j + 1) * LANES] = (u[r0:r1, j * LANES:(j + 1) * LANES] * s).astype(BF16)

    cos, sup, sdn = cos_ref[...], sup_ref[...], sdn_ref[...]
    q0 = 2 * A_WIDTH
    for j in range(B_WIDTH // LANES):
        qn = _head_rms(proj(q0 + j * LANES, q0 + (j + 1) * LANES), gq_ref[...])
        qn_ref[:, j * LANES:(j + 1) * LANES] = qn.astype(BF16)
        qr_ref[:, j * LANES:(j + 1) * LANES] = _rope(qn, cos, sup, sdn).astype(BF16)
    k0 = q0 + B_WIDTH
    kc_ref[...] = proj(k0, k0 + LANES)
    vc_ref[...] = proj(k0 + LANES, k0 + 2 * LANES)
    ks = _rope(_head_rms(proj(k0 + 2 * LANES, k0 + 3 * LANES), gks_ref[...]), cos, sup, sdn)
    ks_ref[...] = ks
    ks16_ref[...] = ks.astype(BF16)
    vs = proj(k0 + 3 * LANES, k0 + 4 * LANES)
    vs_ref[...] = vs
    vs16_ref[...] = vs.astype(BF16)
    kw = _rope(_head_rms(proj(k0 + 4 * LANES, k0 + 5 * LANES), gkw_ref[...]), cos, sup, sdn)
    kw_ref[...] = kw
    kw16_ref[...] = kw.astype(BF16)
    vw = proj(k0 + 5 * LANES, k0 + 6 * LANES)
    vw_ref[...] = vw
    vw16_ref[...] = vw.astype(BF16)
    gate_ref[...] = proj(k0 + 6 * LANES, k0 + 7 * LANES)


def _in_proj(x, wts, tabs, wmix, bmix, *, tm, seg_rows):
    n = x.shape[0]
    tiles_per_seg = seg_rows // tm
    n_seg = n // seg_rows
    row = lambda w: pl.BlockSpec((tm, w), lambda i: (i, 0))
    full = lambda a: pl.BlockSpec(a.shape, lambda i: (0,) * a.ndim)
    cos, sup, sdn = tabs
    ins = [x, wts['g1'], wts['w_a'], wts['ln_g'], wts['ln_b'], wts['g_q'], wts['g_ks'], wts['g_kw'],
           cos, sup, sdn, wmix, bmix]
    tab = pl.BlockSpec((tm, LANES), lambda i: (i % tiles_per_seg, 0))
    in_specs = [row(D_MODEL)] + [full(a) for a in ins[1:8]] + [tab] * 3 + [full(wmix), full(bmix)]
    f32o = lambda w: jax.ShapeDtypeStruct((n, w), F32)
    b16o = lambda w: jax.ShapeDtypeStruct((n, w), BF16)
    out_shape = ([b16o(A_WIDTH), jax.ShapeDtypeStruct((n_seg, CHUNK, A_WIDTH), F32), b16o(B_WIDTH), b16o(B_WIDTH)]
                 + [f32o(LANES)] * 6 + [b16o(LANES)] * 4 + [f32o(LANES)])
    out_specs = ([row(A_WIDTH), pl.BlockSpec((1, CHUNK, A_WIDTH), lambda i: (i // tiles_per_seg, 0, 0)),
                  row(B_WIDTH), row(B_WIDTH)] + [row(LANES)] * 11)
    return pl.pallas_call(
        functools.partial(_in_proj_body, tm=tm, tiles_per_seg=tiles_per_seg),
        grid=(n // tm,), in_specs=in_specs, out_specs=out_specs, out_shape=out_shape,
        compiler_params=pltpu.CompilerParams(dimension_semantics=("arbitrary",), vmem_limit_bytes=VMEM_LIMIT),
        name="in_proj")(*ins)


COMPRESS_ROWS = 256


def _compress_math(load_rows, m, w1ab_ref, pe_ref, w1d_ref, w2d_ref, g_ref, norm):
    step = min(COMPRESS_ROWS, m)
    f = jnp.concatenate(
        [jnp.dot(load_rows(r, r + step).astype(BF16), w1ab_ref[...], preferred_element_type=F32)
         for r in range(0, m, step)], axis=0)
    bias = jnp.dot(pe_ref[...], w1d_ref[...], preferred_element_type=F32)[0:1]
    second_next = pltpu.roll(f[:, LANES:], m - 1, 0)
    act = jax.nn.gelu(f[:, :LANES] + second_next + bias)
    out = jnp.dot(act.astype(BF16), w2d_ref[...], preferred_element_type=F32)
    if norm:
        out = _head_rms(out, g_ref[...])
    return out.astype(BF16)


def _compress_body(ch_ref, w1ab_ref, pe_ref, w1d_ref, w2d_ref, g_ref, out_ref, *, norm):
    out_ref[0] = _compress_math(lambda r0, r1: ch_ref[0, r0:r1], ch_ref.shape[1],
                                w1ab_ref, pe_ref, w1d_ref, w2d_ref, g_ref, norm)


def _compress(ch, cw, *, norm):
    nb, m, _ = ch.shape
    full = lambda a: pl.BlockSpec(a.shape, lambda b: (0,) * a.ndim)
    ws = [cw['w1ab'], cw['pe'], cw['w1d'], cw['w2d'], cw['g']]
    return pl.pallas_call(
        functools.partial(_compress_body, norm=norm),
        grid=(nb,),
        in_specs=[pl.BlockSpec((1, m, CHUNK_LANES), lambda b: (b, 0, 0))] + [full(a) for a in ws],
        out_specs=pl.BlockSpec((1, m, LANES), lambda b: (b, 0, 0)),
        out_shape=jax.ShapeDtypeStruct((nb, m, LANES), BF16),
        compiler_params=pltpu.CompilerParams(dimension_semantics=("arbitrary",), vmem_limit_bytes=VMEM_LIMIT),
        name="compress_k" if norm else "compress_v")(ch, *ws)


def _page_copy(pool_ref, page, buf_ref, slot, i, sem_ref):
    return pltpu.make_async_copy(pool_ref.at[page], buf_ref.at[slot, i], sem_ref.at[slot])


def _paged_compress_body(pt_ref, pool_ref, perm_ref, w1ab_ref, pe_ref, w1d_ref, w2d_ref, g_ref, out_ref,
                         buf_ref, sem_ref, ch_ref, *, n_pages, norm):
    b = pl.program_id(0)

    def fetch(bb, slot):
        def one(i, carry):
            _page_copy(pool_ref, pt_ref[bb * n_pages + i], buf_ref, slot, i, sem_ref).start()
            return carry
        lax.fori_loop(0, n_pages, one, 0)

    @pl.when(b == 0)
    def _():
        fetch(0, 0)

    @pl.when(b + 1 < pl.num_programs(0))
    def _():
        fetch(b + 1, (b + 1) % 2)

    slot = b % 2

    def wait_one(i, carry):
        _page_copy(pool_ref, 0, buf_ref, slot, i, sem_ref).wait()
        return carry
    lax.fori_loop(0, n_pages, wait_one, 0)

    chunks = PAGE_SIZE // CMP_STRIDE

    def regroup(q, carry):
        xt = jnp.concatenate([buf_ref[slot, 2 * q], buf_ref[slot, 2 * q + 1]], axis=1).astype(BF16)
        y = jnp.dot(xt, perm_ref[...], preferred_element_type=F32)
        t0, t1 = y[:, :PAGE_SIZE].T, y[:, PAGE_SIZE:].T
        rows = pl.ds(pl.multiple_of(q * 2 * chunks, 2 * chunks), 2 * chunks)
        for r in range(CMP_STRIDE):
            piece = jnp.concatenate([t0[r * chunks:(r + 1) * chunks], t1[r * chunks:(r + 1) * chunks]], axis=0)
            ch_ref[rows, r * KV_WIDTH:(r + 1) * KV_WIDTH] = piece.astype(BF16)
        return carry
    lax.fori_loop(0, n_pages // 2, regroup, 0, unroll=16)
    m = n_pages * chunks
    out_ref[0] = _compress_math(lambda r0, r1: ch_ref[r0:r1], m, w1ab_ref, pe_ref, w1d_ref, w2d_ref, g_ref, norm)


def _paged_compress(pt_flat, pool, cw, *, nb, n_pages, norm):
    m = n_pages * (PAGE_SIZE // CMP_STRIDE)
    full = lambda a: pl.BlockSpec(a.shape, lambda b, pt: (0,) * a.ndim)
    lane = jnp.arange(2 * PAGE_SIZE)
    page, chunk, offset = lane // PAGE_SIZE, (lane % PAGE_SIZE) // CMP_STRIDE, lane % CMP_STRIDE
    dest = page * PAGE_SIZE + offset * (PAGE_SIZE // CMP_STRIDE) + chunk
    perm = (dest[:, None] == lane[None, :]).astype(BF16)
    ws = [perm, cw['w1ab'], cw['pe'], cw['w1d'], cw['w2d'], cw['g']]
    grid_spec = pltpu.PrefetchScalarGridSpec(
        num_scalar_prefetch=1, grid=(nb,),
        in_specs=[pl.BlockSpec(memory_space=pl.ANY)] + [full(a) for a in ws],
        out_specs=pl.BlockSpec((1, m, LANES), lambda b, pt: (b, 0, 0)),
        scratch_shapes=[pltpu.VMEM((2, n_pages, KV_WIDTH, PAGE_SIZE), F32), pltpu.SemaphoreType.DMA((2,)),
                        pltpu.VMEM((m, CHUNK_LANES), BF16)])
    return pl.pallas_call(
        functools.partial(_paged_compress_body, n_pages=n_pages, norm=norm),
        grid_spec=grid_spec, out_shape=jax.ShapeDtypeStruct((nb, m, LANES), BF16),
        compiler_params=pltpu.CompilerParams(dimension_semantics=("arbitrary",), vmem_limit_bytes=VMEM_LIMIT),
        name="paged_compress_k" if norm else "paged_compress_v")(pt_flat, pool, *ws)


def _split3(x):
    hi = x.astype(BF16)
    r1 = x - hi.astype(F32)
    mid = r1.astype(BF16)
    lo = (r1 - mid.astype(F32)).astype(BF16)
    return hi, mid, lo


def _dot_nt(a, b):
    return lax.dot_general(a, b, (((1,), (1,)), ((), ())), preferred_element_type=F32)


def _topk_mask_t(rank, n_pick):
    n_blk = rank.shape[0]
    jidx = lax.broadcasted_iota(jnp.int32, rank.shape, 0).astype(F32)

    def body(_, carry):
        r, sel = carry
        m = jnp.max(r, axis=0, keepdims=True)
        first = jnp.min(jnp.where(r == m, jidx, float(n_blk)), axis=0, keepdims=True)
        pick = jidx == first
        return jnp.where(pick, -jnp.inf, r), jnp.where(pick, 1.0, sel)

    _, sel = lax.fori_loop(0, n_pick, body, (rank, jnp.zeros_like(rank)))
    return sel


def _masked_softmax(s, mask):
    s = jnp.where(mask, s, NEG)
    e = jnp.exp(s - jnp.max(s, axis=-1, keepdims=True))
    return e / jnp.sum(e, axis=-1, keepdims=True) * mask.astype(F32)


def _gate_col(gt, head, branch):
    c = head * 3 + branch
    return gt[:, c:c + 1]


def _prompt_attn_body(qn_ref, qr_ref, gate_ref, kc_ref, vc_ref, kst_ref, vs_ref, kw_ref, vw_ref, ebt_ref, at_ref,
                      o_ref, m_ref, l_ref, acc_ref, *, seq, n_cmp):
    i = pl.program_id(1)
    s0 = i * Q_TILE
    nc_pad = kc_ref.shape[1]
    n_sel = seq // SEL_BLOCK
    lo = _low_half((Q_TILE, LANES))
    pos_col = s0 + lax.broadcasted_iota(jnp.int32, (Q_TILE, 1), 0)

    def head_rows(q_ref):
        parts = []
        for hd in range(N_HEADS):
            keep = lo if hd < GROUP else jnp.logical_not(lo)
            j = hd % GROUP
            parts.append(jnp.where(keep, q_ref[0, :, j * LANES:(j + 1) * LANES], 0))
        return jnp.concatenate(parts, axis=0) * SCALE

    cidx = lax.broadcasted_iota(jnp.int32, (Q_TILE, nc_pad), 1)
    m_c = (cidx * CMP_STRIDE + (CMP_BLOCK - 1) <= pos_col) & (cidx < n_cmp)
    jrow = lax.broadcasted_iota(jnp.int32, (n_sel, Q_TILE), 0)
    pos_row = s0 + lax.broadcasted_iota(jnp.int32, (n_sel, Q_TILE), 1)
    cur = pos_row >> 6
    forced = (jrow == 0) | (jrow == cur) | (jrow == cur - 1)
    valid_b = jrow * SEL_BLOCK <= pos_row
    p = _masked_softmax(_dot_nt(head_rows(qn_ref), kc_ref[0]).reshape(N_HEADS, Q_TILE, nc_pad), m_c[None])
    o_cmp = jnp.dot(p.reshape(N_HEADS * Q_TILE, nc_pad).astype(BF16), vc_ref[0], preferred_element_type=F32)
    notsel = []
    for kv in range(N_KV):
        imp = jnp.sum(p[kv * GROUP:(kv + 1) * GROUP], axis=0)
        blk_t = sum(_dot_nt(at_ref[...], part) for part in _split3(imp))
        rank = jnp.where(forced, FORCED_BONUS, jnp.where(valid_b, blk_t, -FORCED_BONUS))
        sel_t = _topk_mask_t(rank, min(N_SELECT, n_sel))
        notsel += [(1.0 - sel_t.T).astype(BF16)] * GROUP

    m_ref[...] = jnp.full(m_ref.shape, NEG, F32)
    l_ref[...] = jnp.zeros(l_ref.shape, F32)
    acc_ref[...] = jnp.zeros(acc_ref.shape, F32)
    q_sel = head_rows(qr_ref)
    q_aug = jnp.concatenate([q_sel, jnp.concatenate(notsel, axis=0)], axis=1)

    def tile_step(tix, diagonal):
        k_aug = jnp.concatenate([kst_ref[0, tix], ebt_ref[tix]], axis=0)
        vt = vs_ref[0, pl.ds(pl.multiple_of(tix * KEY_TILE, KEY_TILE), KEY_TILE), :]
        s = jnp.dot(q_aug, k_aug, preferred_element_type=F32)
        if diagonal:
            key = tix * KEY_TILE + lax.broadcasted_iota(jnp.int32, (Q_TILE, KEY_TILE), 1)
            future = jnp.where(key <= pos_col, 0.0, NEG)
            s = (s.reshape(N_HEADS, Q_TILE, KEY_TILE) + future[None]).reshape(N_HEADS * Q_TILE, KEY_TILE)
        m_old = m_ref[...]
        m_new = jnp.maximum(m_old, jnp.max(s, axis=-1, keepdims=True))
        alpha = jnp.exp(m_old - m_new)
        p_t = jnp.exp(s - m_new)
        l_ref[...] = alpha * l_ref[...] + jnp.sum(p_t, axis=-1, keepdims=True)
        acc_ref[...] = alpha * acc_ref[...] + jnp.dot(p_t.astype(BF16), vt, preferred_element_type=F32)
        m_ref[...] = m_new

    def tile_body(tix, carry):
        tile_step(tix, False)
        return carry

    q_per_key = KEY_TILE // Q_TILE
    last_tile = lax.shift_right_logical(i, q_per_key.bit_length() - 1)
    lax.fori_loop(0, last_tile, tile_body, 0)
    tile_step(last_tile, True)

    n_win = WINDOW + Q_TILE
    w0 = pl.multiple_of(jnp.maximum(s0 - WINDOW, 0), Q_TILE)
    dpos = pos_col - (w0 + lax.broadcasted_iota(jnp.int32, (Q_TILE, n_win), 1))
    m_w = (dpos >= 0) & (dpos < WINDOW)
    s = _dot_nt(q_sel, kw_ref[0, pl.ds(w0, n_win), :]).reshape(N_HEADS, Q_TILE, n_win)
    p = _masked_softmax(s, m_w[None])
    o_win = jnp.dot(p.reshape(N_HEADS * Q_TILE, n_win).astype(BF16), vw_ref[0, pl.ds(w0, n_win), :],
                    preferred_element_type=F32)
    o_sel = acc_ref[...] / l_ref[...]
    gt = jax.nn.sigmoid(gate_ref[0])
    o_heads = []
    for hd in range(N_HEADS):
        rows = slice(hd * Q_TILE, (hd + 1) * Q_TILE)
        o_heads.append(_gate_col(gt, hd, 0) * o_cmp[rows] + _gate_col(gt, hd, 1) * o_sel[rows]
                       + _gate_col(gt, hd, 2) * o_win[rows])
    for j in range(GROUP):
        o_ref[0, :, j * LANES:(j + 1) * LANES] = jnp.where(lo, o_heads[j], o_heads[GROUP + j]).astype(BF16)


def _prompt_attn(qn, qr, gate, kc, vc, kst, vs16, kw16, vw16, ebt, a_t, *, n_cmp):
    nb, seq, _ = qn.shape
    qspec = lambda w: pl.BlockSpec((1, Q_TILE, w), lambda b, i: (b, i, 0))
    seqspec = lambda a: pl.BlockSpec((1,) + a.shape[1:], lambda b, i: (b,) + (0,) * (a.ndim - 1))
    full = lambda a: pl.BlockSpec(a.shape, lambda b, i: (0,) * a.ndim)
    rows = N_HEADS * Q_TILE
    return pl.pallas_call(
        functools.partial(_prompt_attn_body, seq=seq, n_cmp=n_cmp),
        grid=(nb, seq // Q_TILE),
        in_specs=[qspec(B_WIDTH), qspec(B_WIDTH), qspec(LANES), seqspec(kc), seqspec(vc),
                  seqspec(kst), seqspec(vs16), seqspec(kw16), seqspec(vw16), full(ebt), full(a_t)],
        out_specs=qspec(B_WIDTH),
        out_shape=jax.ShapeDtypeStruct((nb, seq, B_WIDTH), BF16),
        scratch_shapes=[pltpu.VMEM((rows, 1), F32), pltpu.VMEM((rows, 1), F32), pltpu.VMEM((rows, LANES), F32)],
        compiler_params=pltpu.CompilerParams(dimension_semantics=("arbitrary", "arbitrary"),
                                             vmem_limit_bytes=VMEM_LIMIT),
        name="prompt_attn")(qn, qr, gate, kc, vc, kst, vs16, kw16, vw16, ebt, a_t)


TOK_PAD = 8
ROWS = N_HEADS * TOK_PAD
ROWS_PAD = 128


def _page_copy_lanes(pool_ref, page, buf_ref, slot, i, sem_ref):
    return pltpu.make_async_copy(pool_ref.at[page], buf_ref.at[slot, :, pl.ds(i * PAGE_SIZE, PAGE_SIZE)],
                                 sem_ref.at[slot])


def _sample_attn_body(pt_ref, qn_ref, qr_ref, gate_ref, kc_ref, vc_ref, kwc_ref, vwc_ref,
                      ksn_ref, vsn_ref, kwn_ref, vwn_ref, at_ref, gsum_ref, ebt_ref, kpool_ref, vpool_ref,
                      o_ref, kbuf_ref, vbuf_ref, ksem_ref, vsem_ref, notsel_ref, m_ref, l_ref, acc_ref, ocw_ref,
                      q_ref, *, n_chunks, pages_per_chunk, past, n_new, n_cmp):
    b = pl.program_id(0)
    c = pl.program_id(1)
    step = b * n_chunks + c
    n_steps = pl.num_programs(0) * n_chunks
    n_past_blk = past // SEL_BLOCK
    blk_per_chunk = n_past_blk // n_chunks

    def fetch(st, slot):
        def one(i, carry):
            page = pt_ref[st * pages_per_chunk + i]
            _page_copy_lanes(kpool_ref, page, kbuf_ref, slot, i, ksem_ref).start()
            _page_copy_lanes(vpool_ref, page, vbuf_ref, slot, i, vsem_ref).start()
            return carry
        lax.fori_loop(0, pages_per_chunk, one, 0)

    @pl.when(step == 0)
    def _():
        fetch(0, 0)

    @pl.when(step + 1 < n_steps)
    def _():
        fetch(step + 1, (step + 1) % 2)

    lo = _low_half((TOK_PAD, LANES))
    tok_col = lax.broadcasted_iota(jnp.int32, (ROWS, 1), 0) % TOK_PAD
    pos_col = past + tok_col

    def rows_q(ref):
        parts = []
        for hd in range(N_HEADS):
            keep = lo if hd < GROUP else jnp.logical_not(lo)
            j = hd % GROUP
            parts.append(jnp.where(keep, ref[0, :, j * LANES:(j + 1) * LANES], 0))
        return jnp.concatenate(parts, axis=0) * SCALE

    def gate_rows(gt, branch):
        return jnp.concatenate([_gate_col(gt, hd, branch) for hd in range(N_HEADS)], axis=0)

    @pl.when(c == 0)
    def _():
        gt = jax.nn.sigmoid(gate_ref[0])
        nc_pad = kc_ref.shape[1]
        cidx = lax.broadcasted_iota(jnp.int32, (ROWS, nc_pad), 1)
        m_c = (cidx * CMP_STRIDE + (CMP_BLOCK - 1) <= pos_col) & (cidx < n_cmp)
        p = _masked_softmax(_dot_nt(rows_q(qn_ref), kc_ref[0]), m_c)
        o_c = jnp.dot(p.astype(BF16), vc_ref[0], preferred_element_type=F32)
        imp = sum(jnp.dot(gsum_ref[...], part, preferred_element_type=F32) for part in _split3(p))
        blk_t = sum(_dot_nt(at_ref[...], part) for part in _split3(imp))
        jrow = lax.broadcasted_iota(jnp.int32, blk_t.shape, 0)
        rank = jnp.where((jrow == 0) | (jrow == n_past_blk - 1), FORCED_BONUS, blk_t)
        n_top = min(N_SELECT, n_past_blk + 1)
        sel_t = _topk_mask_t(rank, n_top - 1)
        for cc in range(n_chunks):
            blk = sel_t[cc * blk_per_chunk:(cc + 1) * blk_per_chunk]
            notsel_ref[cc] = (1.0 - blk.T[:ROWS]).astype(BF16)
        q_r = rows_q(qr_ref)
        q_ref[...] = q_r
        n_buf = kwc_ref.shape[2]
        kidx = lax.broadcasted_iota(jnp.int32, (ROWS, n_buf), 1)
        dpos = pos_col - (past - n_buf + kidx)
        m_old = (dpos >= 0) & (dpos < WINDOW)
        knew = lax.broadcasted_iota(jnp.int32, (ROWS, TOK_PAD), 1)
        m_new = (knew <= tok_col) & (knew < n_new)
        s_old = jnp.where(m_old, jnp.dot(q_r, kwc_ref[0].astype(BF16), preferred_element_type=F32), NEG)
        s_new = jnp.where(m_new, _dot_nt(q_r, kwn_ref[0]), NEG)
        mx = jnp.maximum(jnp.max(s_old, axis=-1, keepdims=True), jnp.max(s_new, axis=-1, keepdims=True))
        e_old = jnp.exp(s_old - mx)
        e_new = jnp.exp(s_new - mx)
        den = jnp.sum(e_old, axis=-1, keepdims=True) + jnp.sum(e_new, axis=-1, keepdims=True)
        p_old = e_old / den * m_old.astype(F32)
        p_new = e_new / den * m_new.astype(F32)
        o_w = (_dot_nt(p_old.astype(BF16), vwc_ref[0].astype(BF16))
               + jnp.dot(p_new.astype(BF16), vwn_ref[0], preferred_element_type=F32))
        ocw_ref[...] = gate_rows(gt, 0) * o_c + gate_rows(gt, 2) * o_w
        s = jnp.where(m_new, _dot_nt(q_r, ksn_ref[0]), NEG)
        m0 = jnp.max(s, axis=-1, keepdims=True)
        e = jnp.exp(s - m0) * m_new.astype(F32)
        m_ref[...] = m0
        l_ref[...] = jnp.sum(e, axis=-1, keepdims=True)
        acc_ref[...] = jnp.dot(e.astype(BF16), vsn_ref[0], preferred_element_type=F32)

    slot = step % 2

    def wait_one(i, carry):
        _page_copy_lanes(kpool_ref, 0, kbuf_ref, slot, i, ksem_ref).wait()
        _page_copy_lanes(vpool_ref, 0, vbuf_ref, slot, i, vsem_ref).wait()
        return carry
    lax.fori_loop(0, pages_per_chunk, wait_one, 0)

    q_aug = jnp.concatenate([q_ref[...], notsel_ref[c]], axis=1)
    k_aug = jnp.concatenate([kbuf_ref[slot].astype(BF16), ebt_ref[...]], axis=0)
    s = jnp.dot(q_aug, k_aug, preferred_element_type=F32)
    m_old = m_ref[...]
    m_new = jnp.maximum(m_old, jnp.max(s, axis=-1, keepdims=True))
    alpha = jnp.exp(m_old - m_new)
    p = jnp.exp(s - m_new)
    l_ref[...] = alpha * l_ref[...] + jnp.sum(p, axis=-1, keepdims=True)
    acc_ref[...] = alpha * acc_ref[...] + _dot_nt(p.astype(BF16), vbuf_ref[slot].astype(BF16))
    m_ref[...] = m_new

    @pl.when(c == n_chunks - 1)
    def _():
        gt = jax.nn.sigmoid(gate_ref[0])
        o = ocw_ref[...] + gate_rows(gt, 1) * (acc_ref[...] / l_ref[...])
        for j in range(GROUP):
            top = o[j * TOK_PAD:(j + 1) * TOK_PAD]
            bot = o[(GROUP + j) * TOK_PAD:(GROUP + j + 1) * TOK_PAD]
            o_ref[0, :, j * LANES:(j + 1) * LANES] = jnp.where(lo, top, bot).astype(BF16)


def _sample_attn(pt_flat, qn, qr, gate, kc, vc, kwc, vwc, ksn, vsn, kwn, vwn, a_t, gsum, ebt, kpool, vpool,
                 *, n_chunks, pages_per_chunk, past, n_new, n_cmp):
    nb = qn.shape[0]
    chunk_keys = pages_per_chunk * PAGE_SIZE
    blk_per_chunk = chunk_keys // SEL_BLOCK
    bspec = lambda a: pl.BlockSpec((1,) + a.shape[1:], lambda b, c, pt: (b, 0, 0))
    full = lambda a: pl.BlockSpec(a.shape, lambda b, c, pt: (0,) * a.ndim)
    anyspec = pl.BlockSpec(memory_space=pl.ANY)
    blocked = [qn, qr, gate, kc, vc, kwc, vwc, ksn, vsn, kwn, vwn]
    grid_spec = pltpu.PrefetchScalarGridSpec(
        num_scalar_prefetch=1, grid=(nb, n_chunks),
        in_specs=[bspec(a) for a in blocked] + [full(a_t), full(gsum), full(ebt), anyspec, anyspec],
        out_specs=pl.BlockSpec((1, TOK_PAD, B_WIDTH), lambda b, c, pt: (b, 0, 0)),
        scratch_shapes=[pltpu.VMEM((2, KV_WIDTH, chunk_keys), F32), pltpu.VMEM((2, KV_WIDTH, chunk_keys), F32),
                        pltpu.SemaphoreType.DMA((2,)), pltpu.SemaphoreType.DMA((2,)),
                        pltpu.VMEM((n_chunks, ROWS, blk_per_chunk), BF16),
                        pltpu.VMEM((ROWS, 1), F32), pltpu.VMEM((ROWS, 1), F32), pltpu.VMEM((ROWS, LANES), F32),
                        pltpu.VMEM((ROWS, LANES), F32), pltpu.VMEM((ROWS, LANES), BF16)])
    return pl.pallas_call(
        functools.partial(_sample_attn_body, n_chunks=n_chunks, pages_per_chunk=pages_per_chunk, past=past,
                          n_new=n_new, n_cmp=n_cmp),
        grid_spec=grid_spec, out_shape=jax.ShapeDtypeStruct((nb, TOK_PAD, B_WIDTH), BF16),
        compiler_params=pltpu.CompilerParams(dimension_semantics=("arbitrary", "arbitrary"),
                                             vmem_limit_bytes=VMEM_LIMIT),
        name="sample_attn")(pt_flat, *blocked, a_t, gsum, ebt, kpool, vpool)


FF_TILE = 1024


def _merge_ffn_body(x_ref, oa_ref, ob_ref, g1_ref, wmg_ref, wba_ref, wbb_ref, wout_ref, g2_ref, wup_ref, wdn_ref,
                    y_ref):
    x = x_ref[...]
    h = _rms_rows(x, g1_ref[...]).astype(BF16)
    y_a = jnp.dot(oa_ref[...], wba_ref[...], preferred_element_type=F32)
    y_b = jnp.dot(ob_ref[...], wbb_ref[...], preferred_element_type=F32)
    g_a = jax.nn.sigmoid(jnp.dot(h, wmg_ref[:, :D_MODEL], preferred_element_type=F32))
    g_b = jax.nn.sigmoid(jnp.dot(h, wmg_ref[:, D_MODEL:], preferred_element_type=F32))
    x1 = x + jnp.dot((g_a * y_a + g_b * y_b).astype(BF16), wout_ref[...], preferred_element_type=F32)
    h2 = _rms_rows(x1, g2_ref[...]).astype(BF16)
    y = x1
    for f in range(D_FF // FF_TILE):
        up = jnp.dot(h2, wup_ref[:, f * FF_TILE:(f + 1) * FF_TILE], preferred_element_type=F32)
        act = jnp.square(jnp.maximum(up, 0.0)).astype(BF16)
        y = y + jnp.dot(act, wdn_ref[f * FF_TILE:(f + 1) * FF_TILE, :], preferred_element_type=F32)
    y_ref[...] = y


def _merge_ffn(x, o_a, o_b, wts, *, tm):
    n = x.shape[0]
    row = lambda w: pl.BlockSpec((tm, w), lambda i: (i, 0))
    const = lambda a: pl.BlockSpec(a.shape, lambda i: (0,) * a.ndim, pipeline_mode=pl.Buffered(1))
    ws = [wts['g1'], wts['w_mg'], wts['w_br_a'], wts['w_br_b'], wts['w_out'], wts['g2'], wts['w_up'], wts['w_down']]
    return pl.pallas_call(
        _merge_ffn_body, grid=(n // tm,),
        in_specs=[row(D_MODEL), row(A_WIDTH), row(B_WIDTH)] + [const(a) for a in ws],
        out_specs=row(D_MODEL), out_shape=jax.ShapeDtypeStruct((n, D_MODEL), F32),
        compiler_params=pltpu.CompilerParams(dimension_semantics=("arbitrary",), vmem_limit_bytes=VMEM_LIMIT),
        name="merge_ffn")(x, o_a, o_b, *ws)


def _rope_tables(pos):
    half = ROT_DIM // 2
    lane = jnp.arange(LANES) % HEAD_DIM
    inv = ROPE_THETA ** (-(2 * (lane % half)).astype(F32) / ROT_DIM)
    ang = pos.astype(F32)[:, None] * inv[None, :]
    cos, sin = jnp.cos(ang), jnp.sin(ang)
    cos_t = jnp.where(lane < ROT_DIM, cos, 1.0)
    sin_up = jnp.where((lane >= half) & (lane < ROT_DIM), sin, 0.0)
    sin_dn = jnp.where(lane < half, -sin, 0.0)
    return cos_t, sin_up, sin_dn


def _two(g):
    return jnp.concatenate([g, g])[None, :].astype(F32)


def _layer_weights(l, g_norm1, w_in, ln_v_g, ln_v_b, g_q, g_ks, g_kw, w_branch, w_out, g_norm2, w_up, w_down):
    w = w_in[l]
    q0, k0 = 2 * A_WIDTH, 2 * A_WIDTH + B_WIDTH
    g0 = k0 + 6 * KV_WIDTH
    perm = jnp.array(HEAD_PERM)
    q_cols = w[:, q0:k0].reshape(D_MODEL, N_HEADS, HEAD_DIM)[:, perm].reshape(D_MODEL, B_WIDTH)
    gate_cols = jnp.pad(w[:, g0:g0 + 3 * N_HEADS], ((0, 0), (0, LANES - 3 * N_HEADS)))
    w_a = jnp.concatenate([w[:, :q0], q_cols, w[:, k0:g0], gate_cols], axis=1).astype(BF16)
    wb = w_branch[l]
    w_br_b = wb[A_WIDTH:].reshape(N_HEADS, HEAD_DIM, D_MODEL)[perm].reshape(B_WIDTH, D_MODEL)
    return dict(
        g1=g_norm1[l][None, :], w_a=w_a, ln_g=ln_v_g[l][None, :], ln_b=ln_v_b[l][None, :],
        g_q=_two(g_q[l]), g_ks=_two(g_ks[l]), g_kw=_two(g_kw[l]),
        w_mg=w[:, g0 + 3 * N_HEADS:].astype(BF16), w_br_a=wb[:A_WIDTH].astype(BF16), w_br_b=w_br_b.astype(BF16),
        w_out=w_out[l].astype(BF16), g2=g_norm2[l][None, :], w_up=w_up[l].astype(BF16),
        w_down=w_down[l].astype(BF16))


def _compress_weights(w1, w2, pe, g):
    half = CMP_STRIDE * HEAD_DIM
    eye = jnp.eye(N_KV, dtype=F32)

    def spread(wh):
        return jnp.einsum('rde,kl->rkdle', wh.reshape(CMP_STRIDE, HEAD_DIM, HEAD_DIM), eye).reshape(
            CHUNK_LANES, KV_WIDTH)

    w1ab = jnp.concatenate([spread(w1[:half]), spread(w1[half:])], axis=1).astype(BF16)
    pe_rows = jnp.zeros((8, CMP_BLOCK * HEAD_DIM), F32).at[0].set(pe.reshape(-1)).astype(BF16)
    w1d = jnp.concatenate([w1, w1], axis=1).astype(BF16)
    w2d = jnp.einsum('de,kl->kdle', w2, eye).reshape(KV_WIDTH, KV_WIDTH).astype(BF16)
    return dict(w1ab=w1ab, pe=pe_rows, w1d=w1d, w2d=w2d, g=_two(g))


def _block_score_matrix(n_blk, n_chunk):
    j = jnp.arange(n_blk)[:, None]
    c = jnp.arange(n_chunk)[None, :]
    per = SEL_BLOCK // CMP_STRIDE
    a = ((c >= per * j) & (c <= per * j + per - 1)).astype(F32) + ((c >= per * j - 1) & (c <= per * j + per - 2))
    return a.astype(BF16)


def kernel(x_prompt, x_sample, cache_k_cmp, cache_v_cmp, cache_k_sel, cache_v_sel, cache_k_win, cache_v_win,
           page_table, g_norm1, w_in, ln_v_g, ln_v_b, w_s, b_s, g_q, g_kc, g_ks, g_kw, w_ck1, w_ck2, pe_k,
           w_cv1, w_cv2, pe_v, w_branch, w_out, g_norm2, w_up, w_down):
    nb, seq, _ = x_prompt.shape
    db, dseq, _ = x_sample.shape
    depth = w_in.shape[0]
    n_pages = page_table.shape[1]
    past = n_pages * PAGE_SIZE
    assert depth == 1 and seq % KEY_TILE == 0 and seq >= WINDOW + Q_TILE and dseq <= TOK_PAD
    assert past % (SEL_BLOCK * LANES) == 0 and (past + dseq) // CMP_STRIDE == past // CMP_STRIDE
    l = 0
    wts = _layer_weights(l, g_norm1, w_in, ln_v_g, ln_v_b, g_q, g_ks, g_kw, w_branch, w_out, g_norm2, w_up, w_down)
    cw_k = _compress_weights(w_ck1[l], w_ck2[l], pe_k[l], g_kc[l])
    cw_v = _compress_weights(w_cv1[l], w_cv2[l], pe_v[l], g_kc[l])
    causal = jnp.tril(jnp.ones((CHUNK, CHUNK), dtype=bool))
    ws_tril = jnp.where(causal[None], w_s[l], 0.0)
    heads = lambda a, lead: a.reshape(lead + (N_KV, HEAD_DIM))

    n_p = nb * seq
    xp = x_prompt.reshape(n_p, D_MODEL)
    tabs_p = _rope_tables(jnp.arange(seq, dtype=jnp.int32))
    bmix_p = jnp.repeat(b_s[l].T, A_WIDTH // A_GROUPS, axis=1)
    (oa_p, vlast_p, qn_p, qr_p, kc_p, vc_p, ks_p, vs_p, kw_p, vw_p, ks16, vs16, kw16, vw16, gate_p) = _in_proj(
        xp, wts, tabs_p, ws_tril.astype(BF16), bmix_p, tm=256, seg_rows=seq)
    n_chunk_p = seq // CMP_STRIDE
    kcmp_p = _compress(kc_p.reshape(nb, n_chunk_p, CHUNK_LANES), cw_k, norm=True)
    vcmp_p = _compress(vc_p.reshape(nb, n_chunk_p, CHUNK_LANES), cw_v, norm=False)
    n_sel_p = seq // SEL_BLOCK
    n_tiles = seq // KEY_TILE
    kblk = (jnp.arange(seq) // SEL_BLOCK).reshape(n_tiles, 1, KEY_TILE)
    ebt = jnp.where(kblk == jnp.arange(n_sel_p)[None, :, None], MASK_BIAS, 0.0).astype(BF16)
    a_t_p = _block_score_matrix(n_sel_p, n_chunk_p)
    b3 = lambda a: a.reshape(nb, seq, a.shape[-1])
    kst = ks16.reshape(nb, n_tiles, KEY_TILE, KV_WIDTH).transpose(0, 1, 3, 2)
    ob_p = _prompt_attn(b3(qn_p), b3(qr_p), b3(gate_p), kcmp_p, vcmp_p, kst, b3(vs16), b3(kw16), b3(vw16),
                        ebt, a_t_p, n_cmp=n_chunk_p - 1)
    y_p = _merge_ffn(xp, oa_p, ob_p.reshape(n_p, B_WIDTH), wts, tm=512).reshape(nb, seq, D_MODEL)

    n_s = db * dseq
    xs = x_sample.reshape(n_s, D_MODEL)
    tabs_s = _rope_tables(jnp.tile(past + jnp.arange(dseq, dtype=jnp.int32), db))
    wmix_s = jnp.einsum('gpr,bc->gbpcr', ws_tril[:, :dseq, :dseq], jnp.eye(db, dtype=F32)).reshape(
        A_GROUPS, n_s, n_s)
    bmix_s = jnp.tile(bmix_p[:dseq], (db, 1))
    (oa_s, v_s, qn_s, qr_s, kc_s, vc_s, ks_s, vs_s, kw_s, vw_s, ks16s, vs16s, kw16s, vw16s, gate_s) = _in_proj(
        xs, wts, tabs_s, wmix_s.astype(BF16), bmix_s, tm=n_s, seg_rows=n_s)
    pt_flat = page_table.reshape(-1)
    keys_last = lambda a: a.transpose(0, 2, 3, 1).reshape(a.shape[0], KV_WIDTH, a.shape[1])
    kcmp_s = _paged_compress(pt_flat, keys_last(cache_k_cmp[l]), cw_k, nb=db, n_pages=n_pages, norm=True)
    vcmp_s = _paged_compress(pt_flat, keys_last(cache_v_cmp[l]), cw_v, nb=db, n_pages=n_pages, norm=False)
    n_chunk_s = past // CMP_STRIDE
    tokpad = lambda a: jnp.pad(a.reshape(db, dseq, a.shape[-1]), ((0, 0), (0, TOK_PAD - dseq), (0, 0)))
    hh = jnp.arange(ROWS_PAD) // TOK_PAD
    tt = jnp.arange(ROWS_PAD) % TOK_PAD
    gsum = ((hh[:, None] // GROUP == hh[None, :ROWS] // GROUP) & (tt[:, None] == tt[None, :ROWS])
            & (hh[:, None] < N_HEADS)).astype(BF16)
    n_chunks = 2
    chunk_keys = past // n_chunks
    ebt = jnp.where(jnp.arange(chunk_keys)[None, :] // SEL_BLOCK == jnp.arange(chunk_keys // SEL_BLOCK)[:, None],
                    MASK_BIAS, 0.0).astype(BF16)
    ob_s = _sample_attn(
        pt_flat, tokpad(qn_s), tokpad(qr_s), tokpad(gate_s), kcmp_s, vcmp_s,
        keys_last(cache_k_win[l]), keys_last(cache_v_win[l]),
        tokpad(ks16s), tokpad(vs16s), tokpad(kw16s), tokpad(vw16s),
        _block_score_matrix(past // SEL_BLOCK, n_chunk_s), gsum, ebt,
        keys_last(cache_k_sel[l]), keys_last(cache_v_sel[l]),
        n_chunks=n_chunks, pages_per_chunk=n_pages // n_chunks, past=past, n_new=dseq, n_cmp=n_chunk_s - 1)
    y_s = _merge_ffn(xs, oa_s, ob_s[:, :dseq].reshape(n_s, B_WIDTH), wts, tm=n_s).reshape(db, dseq, D_MODEL)

    w_p = min(WINDOW, seq)
    pk = lambda a: heads(a, (nb, seq))[None]
    sk = lambda a: heads(a, (db, dseq))[None]
    return (y_p, y_s, pk(kc_p), pk(vc_p), pk(ks_p), pk(vs_p), pk(kw_p)[:, :, seq - w_p:], pk(vw_p)[:, :, seq - w_p:],
            vlast_p[None], sk(kc_s), sk(vc_s), sk(ks_s), sk(vs_s), sk(kw_s), sk(vw_s),
            v_s.reshape(db, dseq, A_WIDTH)[None])
```

```python
import functools

import jax
import jax.numpy as jnp
from jax import lax
from jax.experimental import pallas as pl
from jax.experimental.pallas import tpu as pltpu

F32 = jnp.float32
BF16 = jnp.bfloat16

D_MODEL = 1024
A_WIDTH = 512
A_GROUPS = 8
CHUNK = 128
N_HEADS = 8
HEAD_DIM = 64
N_KV = 2
GROUP = N_HEADS // N_KV
B_WIDTH = N_HEADS * HEAD_DIM
KV_WIDTH = N_KV * HEAD_DIM
CMP_BLOCK = 32
CMP_STRIDE = 16
SEL_BLOCK = 64
N_SELECT = 16
WINDOW = 512
ROT_DIM = 16
ROPE_THETA = 500000.0
D_FF = 4096
PAGE_SIZE = 128
EPS = 1e-6
NEG = -1e30
FORCED_BONUS = 1e4
MASK_BIAS = -(2.0 ** 100)
SCALE = HEAD_DIM ** -0.5

LANES = 128
Q_TILE = 128
KEY_TILE = 2048
CHUNK_LANES = CMP_STRIDE * KV_WIDTH
HEAD_PERM = (0, 4, 1, 5, 2, 6, 3, 7)
VMEM_LIMIT = 56 * 1024 * 1024


def _low_half(shape):
    return lax.broadcasted_iota(jnp.int32, shape, len(shape) - 1) < HEAD_DIM


def _head_rms(xs, g):
    lo = _low_half(xs.shape)
    sq = xs * xs
    sa = jnp.sum(jnp.where(lo, sq, 0.0), axis=-1, keepdims=True)
    sb = jnp.sum(jnp.where(lo, 0.0, sq), axis=-1, keepdims=True)
    ms = jnp.where(lo, sa, sb) * (1.0 / HEAD_DIM)
    return xs * lax.rsqrt(ms + EPS) * g


def _rope(xs, cos, sin_up, sin_dn):
    return xs * cos + pltpu.roll(xs, ROT_DIM // 2, 1) * sin_up + pltpu.roll(xs, LANES - ROT_DIM // 2, 1) * sin_dn


def _rms_rows(x, g):
    return x * lax.rsqrt(jnp.mean(x * x, axis=-1, keepdims=True) + EPS) * g


def _in_proj_body(x_ref, g1_ref, w_ref, lng_ref, lnb_ref, gq_ref, gks_ref, gkw_ref,
                  cos_ref, sup_ref, sdn_ref, wmix_ref, bmix_ref,
                  oa_ref, vlast_ref, qn_ref, qr_ref, kc_ref, vc_ref, ks_ref, vs_ref, kw_ref, vw_ref,
                  ks16_ref, vs16_ref, kw16_ref, vw16_ref, gate_ref, *, tm, tiles_per_seg):
    i = pl.program_id(0)
    h = _rms_rows(x_ref[...], g1_ref[...]).astype(BF16)

    def proj(lo, hi):
        return jnp.dot(h, w_ref[:, lo:hi], preferred_element_type=F32)

    u = jax.nn.gelu(proj(0, A_WIDTH))
    v = jax.nn.gelu(proj(A_WIDTH, 2 * A_WIDTH))
    vc = v - jnp.mean(v, axis=-1, keepdims=True)
    vn = vc * lax.rsqrt(jnp.mean(vc * vc, axis=-1, keepdims=True) + EPS) * lng_ref[...] + lnb_ref[...]

    @pl.when(i % tiles_per_seg == tiles_per_seg - 1)
    def _():
        vlast_ref[0] = vn[tm - CHUNK:tm]

    lo = _low_half((CHUNK, LANES))
    for c in range(tm // CHUNK):
        r0, r1 = c * CHUNK, (c + 1) * CHUNK
        for j in range(A_WIDTH // LANES):
            slab = vn[r0:r1, j * LANES:(j + 1) * LANES]
            a = jnp.where(lo, slab, 0.0).astype(BF16)
            b = jnp.where(lo, 0.0, slab).astype(BF16)
            s = (jnp.dot(wmix_ref[2 * j], a, preferred_element_type=F32)
                 + jnp.dot(wmix_ref[2 * j + 1], b, preferred_element_type=F32)
                 + bmix_ref[:, j * LANES:(j + 1) * LANES])
            oa_ref[r0:r1, j * LANES:(j + 1) * LANES] = (u[r0:r1, j * LANES:(j + 1) * LANES] * s).astype(BF16)

    cos, sup, sdn = cos_ref[...], sup_ref[...], sdn_ref[...]
    q0 = 2 * A_WIDTH
    for j in range(B_WIDTH // LANES):
        qn = _head_rms(proj(q0 + j * LANES, q0 + (j + 1) * LANES), gq_ref[...])
        qn_ref[:, j * LANES:(j + 1) * LANES] = qn.astype(BF16)
        qr_ref[:, j * LANES:(j + 1) * LANES] = _rope(qn, cos, sup, sdn).astype(BF16)
    k0 = q0 + B_WIDTH
    kc_ref[...] = proj(k0, k0 + LANES)
    vc_ref[...] = proj(k0 + LANES, k0 + 2 * LANES)
    ks = _rope(_head_rms(proj(k0 + 2 * LANES, k0 + 3 * LANES), gks_ref[...]), cos, sup, sdn)
    ks_ref[...] = ks
    ks16_ref[...] = ks.astype(BF16)
    vs = proj(k0 + 3 * LANES, k0 + 4 * LANES)
    vs_ref[...] = vs
    vs16_ref[...] = vs.astype(BF16)
    kw = _rope(_head_rms(proj(k0 + 4 * LANES, k0 + 5 * LANES), gkw_ref[...]), cos, sup, sdn)
    kw_ref[...] = kw
    kw16_ref[...] = kw.astype(BF16)
    vw = proj(k0 + 5 * LANES, k0 + 6 * LANES)
    vw_ref[...] = vw
    vw16_ref[...] = vw.astype(BF16)
    gate_ref[...] = proj(k0 + 6 * LANES, k0 + 7 * LANES)


def _in_proj(x, wts, tabs, wmix, bmix, *, tm, seg_rows):
    n = x.shape[0]
    tiles_per_seg = seg_rows // tm
    n_seg = n // seg_rows
    row = lambda w: pl.BlockSpec((tm, w), lambda i: (i, 0))
    full = lambda a: pl.BlockSpec(a.shape, lambda i: (0,) * a.ndim)
    cos, sup, sdn = tabs
    ins = [x, wts['g1'], wts['w_a'], wts['ln_g'], wts['ln_b'], wts['g_q'], wts['g_ks'], wts['g_kw'],
           cos, sup, sdn, wmix, bmix]
    tab = pl.BlockSpec((tm, LANES), lambda i: (i % tiles_per_seg, 0))
    in_specs = [row(D_MODEL)] + [full(a) for a in ins[1:8]] + [tab] * 3 + [full(wmix), full(bmix)]
    f32o = lambda w: jax.ShapeDtypeStruct((n, w), F32)
    b16o = lambda w: jax.ShapeDtypeStruct((n, w), BF16)
    out_shape = ([b16o(A_WIDTH), jax.ShapeDtypeStruct((n_seg, CHUNK, A_WIDTH), F32), b16o(B_WIDTH), b16o(B_WIDTH)]
                 + [f32o(LANES)] * 6 + [b16o(LANES)] * 4 + [f32o(LANES)])
    out_specs = ([row(A_WIDTH), pl.BlockSpec((1, CHUNK, A_WIDTH), lambda i: (i // tiles_per_seg, 0, 0)),
                  row(B_WIDTH), row(B_WIDTH)] + [row(LANES)] * 11)
    return pl.pallas_call(
        functools.partial(_in_proj_body, tm=tm, tiles_per_seg=tiles_per_seg),
        grid=(n // tm,), in_specs=in_specs, out_specs=out_specs, out_shape=out_shape,
        compiler_params=pltpu.CompilerParams(dimension_semantics=("arbitrary",), vmem_limit_bytes=VMEM_LIMIT),
        name="in_proj")(*ins)


COMPRESS_ROWS = 256


def _compress_math(load_rows, m, w1ab_ref, pe_ref, w1d_ref, w2d_ref, g_ref, norm):
    step = min(COMPRESS_ROWS, m)
    f = jnp.concatenate(
        [jnp.dot(load_rows(r, r + step).astype(BF16), w1ab_ref[...], preferred_element_type=F32)
         for r in range(0, m, step)], axis=0)
    bias = jnp.dot(pe_ref[...], w1d_ref[...], preferred_element_type=F32)[0:1]
    second_next = pltpu.roll(f[:, LANES:], m - 1, 0)
    act = jax.nn.gelu(f[:, :LANES] + second_next + bias)
    out = jnp.dot(act.astype(BF16), w2d_ref[...], preferred_element_type=F32)
    if norm:
        out = _head_rms(out, g_ref[...])
    return out.astype(BF16)


def _compress_body(ch_ref, w1ab_ref, pe_ref, w1d_ref, w2d_ref, g_ref, out_ref, *, norm):
    out_ref[0] = _compress_math(lambda r0, r1: ch_ref[0, r0:r1], ch_ref.shape[1],
                                w1ab_ref, pe_ref, w1d_ref, w2d_ref, g_ref, norm)


def _compress(ch, cw, *, norm):
    nb, m, _ = ch.shape
    full = lambda a: pl.BlockSpec(a.shape, lambda b: (0,) * a.ndim)
    ws = [cw['w1ab'], cw['pe'], cw['w1d'], cw['w2d'], cw['g']]
    return pl.pallas_call(
        functools.partial(_compress_body, norm=norm),
        grid=(nb,),
        in_specs=[pl.BlockSpec((1, m, CHUNK_LANES), lambda b: (b, 0, 0))] + [full(a) for a in ws],
        out_specs=pl.BlockSpec((1, m, LANES), lambda b: (b, 0, 0)),
        out_shape=jax.ShapeDtypeStruct((nb, m, LANES), BF16),
        compiler_params=pltpu.CompilerParams(dimension_semantics=("arbitrary",), vmem_limit_bytes=VMEM_LIMIT),
        name="compress_k" if norm else "compress_v")(ch, *ws)


def _page_copy(pool_ref, page, buf_ref, slot, i, sem_ref):
    return pltpu.make_async_copy(pool_ref.at[page], buf_ref.at[slot, i], sem_ref.at[slot])


def _paged_compress_body(pt_ref, pool_ref, perm_ref, w1ab_ref, pe_ref, w1d_ref, w2d_ref, g_ref, out_ref,
                         buf_ref, sem_ref, ch_ref, *, n_pages, norm):
    b = pl.program_id(0)

    def fetch(bb, slot):
        def one(i, carry):
            _page_copy(pool_ref, pt_ref[bb * n_pages + i], buf_ref, slot, i, sem_ref).start()
            return carry
        lax.fori_loop(0, n_pages, one, 0, unroll=8)

    @pl.when(b == 0)
    def _():
        fetch(0, 0)

    @pl.when(b + 1 < pl.num_programs(0))
    def _():
        fetch(b + 1, (b + 1) % 2)

    slot = b % 2

    pltpu.make_async_copy(pool_ref.at[pl.ds(0, n_pages)], buf_ref.at[slot], sem_ref.at[slot]).wait()

    chunks = PAGE_SIZE // CMP_STRIDE

    def regroup(q, carry):
        xt = jnp.concatenate([buf_ref[slot, 2 * q], buf_ref[slot, 2 * q + 1]], axis=1).astype(BF16)
        y = jnp.dot(xt, perm_ref[...], preferred_element_type=F32)
        t0, t1 = y[:, :PAGE_SIZE].T, y[:, PAGE_SIZE:].T
        rows = pl.ds(pl.multiple_of(q * 2 * chunks, 2 * chunks), 2 * chunks)
        for r in range(CMP_STRIDE):
            piece = jnp.concatenate([t0[r * chunks:(r + 1) * chunks], t1[r * chunks:(r + 1) * chunks]], axis=0)
            ch_ref[rows, r * KV_WIDTH:(r + 1) * KV_WIDTH] = piece.astype(BF16)
        return carry
    lax.fori_loop(0, n_pages // 2, regroup, 0, unroll=16)
    m = n_pages * chunks
    out_ref[0] = _compress_math(lambda r0, r1: ch_ref[r0:r1], m, w1ab_ref, pe_ref, w1d_ref, w2d_ref, g_ref, norm)


def _paged_compress(pt_flat, pool, cw, *, nb, n_pages, norm):
    m = n_pages * (PAGE_SIZE // CMP_STRIDE)
    full = lambda a: pl.BlockSpec(a.shape, lambda b, pt: (0,) * a.ndim)
    lane = jnp.arange(2 * PAGE_SIZE)
    page, chunk, offset = lane // PAGE_SIZE, (lane % PAGE_SIZE) // CMP_STRIDE, lane % CMP_STRIDE
    dest = page * PAGE_SIZE + offset * (PAGE_SIZE // CMP_STRIDE) + chunk
    perm = (dest[:, None] == lane[None, :]).astype(BF16)
    ws = [perm, cw['w1ab'], cw['pe'], cw['w1d'], cw['w2d'], cw['g']]
    grid_spec = pltpu.PrefetchScalarGridSpec(
        num_scalar_prefetch=1, grid=(nb,),
        in_specs=[pl.BlockSpec(memory_space=pl.ANY)] + [full(a) for a in ws],
        out_specs=pl.BlockSpec((1, m, LANES), lambda b, pt: (b, 0, 0)),
        scratch_shapes=[pltpu.VMEM((2, n_pages, KV_WIDTH, PAGE_SIZE), F32), pltpu.SemaphoreType.DMA((2,)),
                        pltpu.VMEM((m, CHUNK_LANES), BF16)])
    return pl.pallas_call(
        functools.partial(_paged_compress_body, n_pages=n_pages, norm=norm),
        grid_spec=grid_spec, out_shape=jax.ShapeDtypeStruct((nb, m, LANES), BF16),
        compiler_params=pltpu.CompilerParams(dimension_semantics=("arbitrary",), vmem_limit_bytes=VMEM_LIMIT),
        name="paged_compress_k" if norm else "paged_compress_v")(pt_flat, pool, *ws)


def _split3(x):
    hi = x.astype(BF16)
    r1 = x - hi.astype(F32)
    mid = r1.astype(BF16)
    lo = (r1 - mid.astype(F32)).astype(BF16)
    return hi, mid, lo


def _dot_nt(a, b):
    return lax.dot_general(a, b, (((1,), (1,)), ((), ())), preferred_element_type=F32)


def _topk_mask_t(rank, n_pick):
    n_blk = rank.shape[0]
    jidx = lax.broadcasted_iota(jnp.int32, rank.shape, 0).astype(F32)

    def body(_, carry):
        r, sel = carry
        m = jnp.max(r, axis=0, keepdims=True)
        first = jnp.min(jnp.where(r == m, jidx, float(n_blk)), axis=0, keepdims=True)
        pick = jidx == first
        return jnp.where(pick, -jnp.inf, r), jnp.where(pick, 1.0, sel)

    _, sel = lax.fori_loop(0, n_pick, body, (rank, jnp.zeros_like(rank)), unroll=True)
    return sel


def _masked_softmax(s, mask):
    s = jnp.where(mask, s, NEG)
    e = jnp.exp(s - jnp.max(s, axis=-1, keepdims=True))
    return e / jnp.sum(e, axis=-1, keepdims=True) * mask.astype(F32)


def _gate_col(gt, head, branch):
    c = head * 3 + branch
    return gt[:, c:c + 1]


def _prompt_attn_body(qn_ref, qr_ref, gate_ref, kc_ref, vc_ref, kst_ref, vs_ref, kw_ref, vw_ref, ebt_ref, at_ref,
                      o_ref, m_ref, l_ref, acc_ref, *, seq, n_cmp):
    i = pl.program_id(1)
    s0 = i * Q_TILE
    nc_pad = kc_ref.shape[1]
    n_sel = seq // SEL_BLOCK
    lo = _low_half((Q_TILE, LANES))
    pos_col = s0 + lax.broadcasted_iota(jnp.int32, (Q_TILE, 1), 0)

    def head_rows(q_ref):
        parts = []
        for hd in range(N_HEADS):
            keep = lo if hd < GROUP else jnp.logical_not(lo)
            j = hd % GROUP
            parts.append(jnp.where(keep, q_ref[0, :, j * LANES:(j + 1) * LANES], 0))
        return jnp.concatenate(parts, axis=0) * SCALE

    cidx = lax.broadcasted_iota(jnp.int32, (Q_TILE, nc_pad), 1)
    m_c = (cidx * CMP_STRIDE + (CMP_BLOCK - 1) <= pos_col) & (cidx < n_cmp)
    jrow = lax.broadcasted_iota(jnp.int32, (n_sel, Q_TILE), 0)
    pos_row = s0 + lax.broadcasted_iota(jnp.int32, (n_sel, Q_TILE), 1)
    cur = pos_row >> 6
    forced = (jrow == 0) | (jrow == cur) | (jrow == cur - 1)
    valid_b = jrow * SEL_BLOCK <= pos_row
    p = _masked_softmax(_dot_nt(head_rows(qn_ref), kc_ref[0]).reshape(N_HEADS, Q_TILE, nc_pad), m_c[None])
    o_cmp = jnp.dot(p.reshape(N_HEADS * Q_TILE, nc_pad).astype(BF16), vc_ref[0], preferred_element_type=F32)
    imps = [jnp.sum(p[kv * GROUP:(kv + 1) * GROUP], axis=0) for kv in range(N_KV)]

    q_sel = head_rows(qr_ref)
    n_win = WINDOW + Q_TILE
    w0 = pl.multiple_of(jnp.maximum(s0 - WINDOW, 0), Q_TILE)
    dpos = pos_col - (w0 + lax.broadcasted_iota(jnp.int32, (Q_TILE, n_win), 1))
    m_w = (dpos >= 0) & (dpos < WINDOW)
    s_w = _dot_nt(q_sel, kw_ref[0, pl.ds(w0, n_win), :]).reshape(N_HEADS, Q_TILE, n_win)
    p_w = _masked_softmax(s_w, m_w[None])
    o_win = jnp.dot(p_w.reshape(N_HEADS * Q_TILE, n_win).astype(BF16), vw_ref[0, pl.ds(w0, n_win), :],
                    preferred_element_type=F32)

    notsel = []
    for kv in range(N_KV):
        imp = imps[kv]
        blk_t = sum(_dot_nt(at_ref[...], part) for part in _split3(imp))
        rank = jnp.where(forced, FORCED_BONUS, jnp.where(valid_b, blk_t, -FORCED_BONUS))
        sel_t = _topk_mask_t(rank, min(N_SELECT, n_sel))
        notsel += [(1.0 - sel_t.T).astype(BF16)] * GROUP

    m_ref[...] = jnp.full(m_ref.shape, NEG, F32)
    l_ref[...] = jnp.zeros(l_ref.shape, F32)
    acc_ref[...] = jnp.zeros(acc_ref.shape, F32)
    q_aug =jnp.concatenate([q_sel, jnp.concatenate(notsel, axis=0)], axis=1)

    def tile_step(tix, diagonal):
        k_aug = jnp.concatenate([kst_ref[0, tix], ebt_ref[tix]], axis=0)
        vt = vs_ref[0, pl.ds(pl.multiple_of(tix * KEY_TILE, KEY_TILE), KEY_TILE), :]
        s = jnp.dot(q_aug, k_aug, preferred_element_type=F32)
        if diagonal:
            key = tix * KEY_TILE + lax.broadcasted_iota(jnp.int32, (Q_TILE, KEY_TILE), 1)
            future = jnp.where(key <= pos_col, 0.0, NEG)
            s = (s.reshape(N_HEADS, Q_TILE, KEY_TILE) + future[None]).reshape(N_HEADS * Q_TILE, KEY_TILE)
        m_old = m_ref[...]
        m_new = jnp.maximum(m_old, jnp.max(s, axis=-1, keepdims=True))
        alpha = jnp.exp(m_old - m_new)
        p_t = jnp.exp(s - m_new)
        l_ref[...] = alpha * l_ref[...] + jnp.sum(p_t, axis=-1, keepdims=True)
        acc_ref[...] = alpha * acc_ref[...] + jnp.dot(p_t.astype(BF16), vt, preferred_element_type=F32)
        m_ref[...] = m_new

    def tile_body(tix, carry):
        tile_step(tix, False)
        return carry

    q_per_key = KEY_TILE // Q_TILE
    last_tile = lax.shift_right_logical(i, q_per_key.bit_length() - 1)
    lax.fori_loop(0, last_tile, tile_body, 0)
    tile_step(last_tile, True)

    o_sel = acc_ref[...] / l_ref[...]
    gt = jax.nn.sigmoid(gate_ref[0])
    o_heads = []
    for hd in range(N_HEADS):
        rows = slice(hd * Q_TILE, (hd + 1) * Q_TILE)
        o_heads.append(_gate_col(gt, hd, 0) * o_cmp[rows] + _gate_col(gt, hd, 1) * o_sel[rows]
                       + _gate_col(gt, hd, 2) * o_win[rows])
    for j in range(GROUP):
        o_ref[0, :, j * LANES:(j + 1) * LANES] = jnp.where(lo, o_heads[j], o_heads[GROUP + j]).astype(BF16)


def _prompt_attn(qn, qr, gate, kc, vc, kst, vs16, kw16, vw16, ebt, a_t, *, n_cmp):
    nb, seq, _ = qn.shape
    qspec = lambda w: pl.BlockSpec((1, Q_TILE, w), lambda b, i: (b, i, 0))
    seqspec = lambda a: pl.BlockSpec((1,) + a.shape[1:], lambda b, i: (b,) + (0,) * (a.ndim - 1))
    full = lambda a: pl.BlockSpec(a.shape, lambda b, i: (0,) * a.ndim)
    rows = N_HEADS * Q_TILE
    return pl.pallas_call(
        functools.partial(_prompt_attn_body, seq=seq, n_cmp=n_cmp),
        grid=(nb, seq // Q_TILE),
        in_specs=[qspec(B_WIDTH), qspec(B_WIDTH), qspec(LANES), seqspec(kc), seqspec(vc),
                  seqspec(kst), seqspec(vs16), seqspec(kw16), seqspec(vw16), full(ebt), full(a_t)],
        out_specs=qspec(B_WIDTH),
        out_shape=jax.ShapeDtypeStruct((nb, seq, B_WIDTH), BF16),
        scratch_shapes=[pltpu.VMEM((rows, 1), F32), pltpu.VMEM((rows, 1), F32), pltpu.VMEM((rows, LANES), F32)],
        compiler_params=pltpu.CompilerParams(dimension_semantics=("arbitrary", "arbitrary"),
                                             vmem_limit_bytes=VMEM_LIMIT),
        name="prompt_attn")(qn, qr, gate, kc, vc, kst, vs16, kw16, vw16, ebt, a_t)


TOK_PAD = 8
ROWS = N_HEADS * TOK_PAD
ROWS_PAD = 128


def _page_copy_lanes(pool_ref, page, buf_ref, slot, i, sem_ref):
    return pltpu.make_async_copy(pool_ref.at[page], buf_ref.at[slot, :, pl.ds(i * PAGE_SIZE, PAGE_SIZE)],
                                 sem_ref.at[slot])


def _sample_attn_body(pt_ref, qn_ref, qr_ref, gate_ref, kc_ref, vc_ref, kwc_ref, vwc_ref,
                      ksn_ref, vsn_ref, kwn_ref, vwn_ref, at_ref, gsum_ref, ebt_ref, kpool_ref, vpool_ref,
                      o_ref, kbuf_ref, vbuf_ref, ksem_ref, vsem_ref, notsel_ref, m_ref, l_ref, acc_ref, ocw_ref,
                      q_ref, *, n_chunks, pages_per_chunk, past, n_new, n_cmp):
    b = pl.program_id(0)
    c = pl.program_id(1)
    step = b * n_chunks + c
    n_steps = pl.num_programs(0) * n_chunks
    n_past_blk = past // SEL_BLOCK
    blk_per_chunk = n_past_blk // n_chunks

    def fetch(st, slot):
        def one(i, carry):
            page = pt_ref[st * pages_per_chunk + i]
            _page_copy_lanes(kpool_ref, page, kbuf_ref, slot, i, ksem_ref).start()
            _page_copy_lanes(vpool_ref, page, vbuf_ref, slot, i, vsem_ref).start()
            return carry
        lax.fori_loop(0, pages_per_chunk, one, 0, unroll=8)

    @pl.when(step == 0)
    def _():
        fetch(0, 0)

    @pl.when(step + 1 < n_steps)
    def _():
        fetch(step + 1, (step + 1) % 2)

    lo = _low_half((TOK_PAD, LANES))
    tok_col = lax.broadcasted_iota(jnp.int32, (ROWS, 1), 0) % TOK_PAD
    pos_col = past + tok_col

    def rows_q(ref):
        parts = []
        for hd in range(N_HEADS):
            keep = lo if hd < GROUP else jnp.logical_not(lo)
            j = hd % GROUP
            parts.append(jnp.where(keep, ref[0, :, j * LANES:(j + 1) * LANES], 0))
        return jnp.concatenate(parts, axis=0) * SCALE

    def gate_rows(gt, branch):
        return jnp.concatenate([_gate_col(gt, hd, branch) for hd in range(N_HEADS)], axis=0)

    @pl.when(c == 0)
    def _():
        gt = jax.nn.sigmoid(gate_ref[0])
        nc_pad = kc_ref.shape[1]
        cidx = lax.broadcasted_iota(jnp.int32, (ROWS, nc_pad), 1)
        m_c = (cidx * CMP_STRIDE + (CMP_BLOCK - 1) <= pos_col) & (cidx < n_cmp)
        p = _masked_softmax(_dot_nt(rows_q(qn_ref), kc_ref[0]), m_c)
        o_c = jnp.dot(p.astype(BF16), vc_ref[0], preferred_element_type=F32)
        imp = sum(jnp.dot(gsum_ref[...], part, preferred_element_type=F32) for part in _split3(p))
        blk_t = sum(_dot_nt(at_ref[...], part) for part in _split3(imp))
        jrow = lax.broadcasted_iota(jnp.int32, blk_t.shape, 0)
        rank = jnp.where((jrow == 0) | (jrow == n_past_blk - 1), FORCED_BONUS, blk_t)
        n_top = min(N_SELECT, n_past_blk + 1)
        sel_t = _topk_mask_t(rank, n_top - 1)
        for cc in range(n_chunks):
            blk = sel_t[cc * blk_per_chunk:(cc + 1) * blk_per_chunk]
            notsel_ref[cc] = (1.0 - blk.T[:ROWS]).astype(BF16)
        q_r = rows_q(qr_ref)
        q_ref[...] = q_r
        n_buf = kwc_ref.shape[2]
        kidx = lax.broadcasted_iota(jnp.int32, (ROWS, n_buf), 1)
        dpos = pos_col - (past - n_buf + kidx)
        m_old = (dpos >= 0) & (dpos < WINDOW)
        knew = lax.broadcasted_iota(jnp.int32, (ROWS, TOK_PAD), 1)
        m_new = (knew <= tok_col) & (knew < n_new)
        s_old = jnp.where(m_old, jnp.dot(q_r, kwc_ref[0].astype(BF16), preferred_element_type=F32), NEG)
        s_new = jnp.where(m_new, _dot_nt(q_r, kwn_ref[0]), NEG)
        mx = jnp.maximum(jnp.max(s_old, axis=-1, keepdims=True), jnp.max(s_new, axis=-1, keepdims=True))
        e_old = jnp.exp(s_old - mx)
        e_new = jnp.exp(s_new - mx)
        den = jnp.sum(e_old, axis=-1, keepdims=True) + jnp.sum(e_new, axis=-1, keepdims=True)
        p_old = e_old / den * m_old.astype(F32)
        p_new = e_new / den * m_new.astype(F32)
        o_w = (_dot_nt(p_old.astype(BF16), vwc_ref[0].astype(BF16))
               + jnp.dot(p_new.astype(BF16), vwn_ref[0], preferred_element_type=F32))
        ocw_ref[...] = gate_rows(gt, 0) * o_c + gate_rows(gt, 2) * o_w
        s = jnp.where(m_new, _dot_nt(q_r, ksn_ref[0]), NEG)
        m0 = jnp.max(s, axis=-1, keepdims=True)
        e = jnp.exp(s - m0) * m_new.astype(F32)
        m_ref[...] = m0
        l_ref[...] = jnp.sum(e, axis=-1, keepdims=True)
        acc_ref[...] = jnp.dot(e.astype(BF16), vsn_ref[0], preferred_element_type=F32)

    slot = step % 2

    pltpu.make_async_copy(kbuf_ref.at[slot], kbuf_ref.at[slot], ksem_ref.at[slot]).wait()
    pltpu.make_async_copy(vbuf_ref.at[slot], vbuf_ref.at[slot], vsem_ref.at[slot]).wait()

    q_aug = jnp.concatenate([q_ref[...], notsel_ref[c]], axis=1)
    k_aug = jnp.concatenate([kbuf_ref[slot].astype(BF16), ebt_ref[...]], axis=0)
    s = jnp.dot(q_aug, k_aug, preferred_element_type=F32)
    m_old = m_ref[...]
    m_new = jnp.maximum(m_old, jnp.max(s, axis=-1, keepdims=True))
    alpha = jnp.exp(m_old - m_new)
    p = jnp.exp(s - m_new)
    l_ref[...] = alpha * l_ref[...] + jnp.sum(p, axis=-1, keepdims=True)
    acc_ref[...] = alpha * acc_ref[...] + _dot_nt(p.astype(BF16), vbuf_ref[slot].astype(BF16))
    m_ref[...] = m_new

    @pl.when(c == n_chunks - 1)
    def _():
        gt = jax.nn.sigmoid(gate_ref[0])
        o = ocw_ref[...] + gate_rows(gt, 1) * (acc_ref[...] / l_ref[...])
        for j in range(GROUP):
            top = o[j * TOK_PAD:(j + 1) * TOK_PAD]
            bot = o[(GROUP + j) * TOK_PAD:(GROUP + j + 1) * TOK_PAD]
            o_ref[0, :, j * LANES:(j + 1) * LANES] = jnp.where(lo, top, bot).astype(BF16)


def _sample_attn(pt_flat, qn, qr, gate, kc, vc, kwc, vwc, ksn, vsn, kwn, vwn, a_t, gsum, ebt, kpool, vpool,
                 *, n_chunks, pages_per_chunk, past, n_new, n_cmp):
    nb = qn.shape[0]
    chunk_keys = pages_per_chunk * PAGE_SIZE
    blk_per_chunk = chunk_keys // SEL_BLOCK
    bspec = lambda a: pl.BlockSpec((1,) + a.shape[1:], lambda b, c, pt: (b, 0, 0))
    full = lambda a: pl.BlockSpec(a.shape, lambda b, c, pt: (0,) * a.ndim)
    anyspec = pl.BlockSpec(memory_space=pl.ANY)
    blocked = [qn, qr, gate, kc, vc, kwc, vwc, ksn, vsn, kwn, vwn]
    grid_spec = pltpu.PrefetchScalarGridSpec(
        num_scalar_prefetch=1, grid=(nb, n_chunks),
        in_specs=[bspec(a) for a in blocked] + [full(a_t), full(gsum), full(ebt), anyspec, anyspec],
        out_specs=pl.BlockSpec((1, TOK_PAD, B_WIDTH), lambda b, c, pt: (b, 0, 0)),
        scratch_shapes=[pltpu.VMEM((2, KV_WIDTH, chunk_keys), F32), pltpu.VMEM((2, KV_WIDTH, chunk_keys), F32),
                        pltpu.SemaphoreType.DMA((2,)), pltpu.SemaphoreType.DMA((2,)),
                        pltpu.VMEM((n_chunks, ROWS, blk_per_chunk), BF16),
                        pltpu.VMEM((ROWS, 1), F32), pltpu.VMEM((ROWS, 1), F32), pltpu.VMEM((ROWS, LANES), F32),
                        pltpu.VMEM((ROWS, LANES), F32), pltpu.VMEM((ROWS, LANES), BF16)])
    return pl.pallas_call(
        functools.partial(_sample_attn_body, n_chunks=n_chunks, pages_per_chunk=pages_per_chunk, past=past,
                          n_new=n_new, n_cmp=n_cmp),
        grid_spec=grid_spec, out_shape=jax.ShapeDtypeStruct((nb, TOK_PAD, B_WIDTH), BF16),
        compiler_params=pltpu.CompilerParams(dimension_semantics=("arbitrary", "arbitrary"),
                                             vmem_limit_bytes=VMEM_LIMIT),
        name="sample_attn")(pt_flat, *blocked, a_t, gsum, ebt, kpool, vpool)


FF_TILE = 1024


def _merge_ffn_body(x_ref, oa_ref, ob_ref, g1_ref, wmg_ref, wba_ref, wbb_ref, wout_ref, g2_ref, wup_ref, wdn_ref,
                    y_ref):
    x = x_ref[...]
    h = _rms_rows(x, g1_ref[...]).astype(BF16)
    y_a = jnp.dot(oa_ref[...], wba_ref[...], preferred_element_type=F32)
    y_b = jnp.dot(ob_ref[...], wbb_ref[...], preferred_element_type=F32)
    g_a = jax.nn.sigmoid(jnp.dot(h, wmg_ref[:, :D_MODEL], preferred_element_type=F32))
    g_b = jax.nn.sigmoid(jnp.dot(h, wmg_ref[:, D_MODEL:], preferred_element_type=F32))
    x1 = x + jnp.dot((g_a * y_a + g_b * y_b).astype(BF16), wout_ref[...], preferred_element_type=F32)
    h2 = _rms_rows(x1, g2_ref[...]).astype(BF16)
    y = x1
    for f in range(D_FF // FF_TILE):
        up = jnp.dot(h2, wup_ref[:, f * FF_TILE:(f + 1) * FF_TILE], preferred_element_type=F32)
        act = jnp.square(jnp.maximum(up, 0.0)).astype(BF16)
        y = y + jnp.dot(act, wdn_ref[f * FF_TILE:(f + 1) * FF_TILE, :], preferred_element_type=F32)
    y_ref[...] = y


def _merge_ffn(x, o_a, o_b, wts, *, tm):
    n = x.shape[0]
    row = lambda w: pl.BlockSpec((tm, w), lambda i: (i, 0))
    const = lambda a: pl.BlockSpec(a.shape, lambda i: (0,) * a.ndim, pipeline_mode=pl.Buffered(1))
    ws = [wts['g1'], wts['w_mg'], wts['w_br_a'], wts['w_br_b'], wts['w_out'], wts['g2'], wts['w_up'], wts['w_down']]
    return pl.pallas_call(
        _merge_ffn_body, grid=(n // tm,),
        in_specs=[row(D_MODEL), row(A_WIDTH), row(B_WIDTH)] + [const(a) for a in ws],
        out_specs=row(D_MODEL), out_shape=jax.ShapeDtypeStruct((n, D_MODEL), F32),
        compiler_params=pltpu.CompilerParams(dimension_semantics=("arbitrary",), vmem_limit_bytes=VMEM_LIMIT),
        name="merge_ffn")(x, o_a, o_b, *ws)


def _rope_tables(pos):
    half = ROT_DIM // 2
    lane = jnp.arange(LANES) % HEAD_DIM
    inv = ROPE_THETA ** (-(2 * (lane % half)).astype(F32) / ROT_DIM)
    ang = pos.astype(F32)[:, None] * inv[None, :]
    cos, sin = jnp.cos(ang), jnp.sin(ang)
    cos_t = jnp.where(lane < ROT_DIM, cos, 1.0)
    sin_up = jnp.where((lane >= half) & (lane < ROT_DIM), sin, 0.0)
    sin_dn = jnp.where(lane < half, -sin, 0.0)
    return cos_t, sin_up, sin_dn


def _two(g):
    return jnp.concatenate([g, g])[None, :].astype(F32)


def _layer_weights(l, g_norm1, w_in, ln_v_g, ln_v_b, g_q, g_ks, g_kw, w_branch, w_out, g_norm2, w_up, w_down):
    w = w_in[l]
    q0, k0 = 2 * A_WIDTH, 2 * A_WIDTH + B_WIDTH
    g0 = k0 + 6 * KV_WIDTH
    perm = jnp.array(HEAD_PERM)
    q_cols = w[:, q0:k0].reshape(D_MODEL, N_HEADS, HEAD_DIM)[:, perm].reshape(D_MODEL, B_WIDTH)
    gate_cols = jnp.pad(w[:, g0:g0 + 3 * N_HEADS], ((0, 0), (0, LANES - 3 * N_HEADS)))
    w_a = jnp.concatenate([w[:, :q0], q_cols, w[:, k0:g0], gate_cols], axis=1).astype(BF16)
    wb = w_branch[l]
    w_br_b = wb[A_WIDTH:].reshape(N_HEADS, HEAD_DIM, D_MODEL)[perm].reshape(B_WIDTH, D_MODEL)
    return dict(
        g1=g_norm1[l][None, :], w_a=w_a, ln_g=ln_v_g[l][None, :], ln_b=ln_v_b[l][None, :],
        g_q=_two(g_q[l]), g_ks=_two(g_ks[l]), g_kw=_two(g_kw[l]),
        w_mg=w[:, g0 + 3 * N_HEADS:].astype(BF16), w_br_a=wb[:A_WIDTH].astype(BF16), w_br_b=w_br_b.astype(BF16),
        w_out=w_out[l].astype(BF16), g2=g_norm2[l][None, :], w_up=w_up[l].astype(BF16),
        w_down=w_down[l].astype(BF16))


def _compress_weights(w1, w2, pe, g):
    half = CMP_STRIDE * HEAD_DIM
    eye = jnp.eye(N_KV, dtype=F32)

    def spread(wh):
        return jnp.einsum('rde,kl->rkdle', wh.reshape(CMP_STRIDE, HEAD_DIM, HEAD_DIM), eye).reshape(
            CHUNK_LANES, KV_WIDTH)

    w1ab = jnp.concatenate([spread(w1[:half]), spread(w1[half:])], axis=1).astype(BF16)
    pe_rows = jnp.zeros((8, CMP_BLOCK * HEAD_DIM), F32).at[0].set(pe.reshape(-1)).astype(BF16)
    w1d = jnp.concatenate([w1, w1], axis=1).astype(BF16)
    w2d = jnp.einsum('de,kl->kdle', w2, eye).reshape(KV_WIDTH, KV_WIDTH).astype(BF16)
    return dict(w1ab=w1ab, pe=pe_rows, w1d=w1d, w2d=w2d, g=_two(g))


def _block_score_matrix(n_blk, n_chunk):
    j = jnp.arange(n_blk)[:, None]
    c = jnp.arange(n_chunk)[None, :]
    per = SEL_BLOCK // CMP_STRIDE
    a = ((c >= per * j) & (c <= per * j + per - 1)).astype(F32) + ((c >= per * j - 1) & (c <= per * j + per - 2))
    return a.astype(BF16)


def kernel(x_prompt, x_sample, cache_k_cmp, cache_v_cmp, cache_k_sel, cache_v_sel, cache_k_win, cache_v_win,
           page_table, g_norm1, w_in, ln_v_g, ln_v_b, w_s, b_s, g_q, g_kc, g_ks, g_kw, w_ck1, w_ck2, pe_k,
           w_cv1, w_cv2, pe_v, w_branch, w_out, g_norm2, w_up, w_down):
    nb, seq, _ = x_prompt.shape
    db, dseq, _ = x_sample.shape
    depth = w_in.shape[0]
    n_pages = page_table.shape[1]
    past = n_pages * PAGE_SIZE
    assert depth == 1 and seq % KEY_TILE == 0 and seq >= WINDOW + Q_TILE and dseq <= TOK_PAD
    assert past % (SEL_BLOCK * LANES) == 0 and (past + dseq) // CMP_STRIDE == past // CMP_STRIDE
    l = 0
    wts = _layer_weights(l, g_norm1, w_in, ln_v_g, ln_v_b, g_q, g_ks, g_kw, w_branch, w_out, g_norm2, w_up, w_down)
    cw_k = _compress_weights(w_ck1[l], w_ck2[l], pe_k[l], g_kc[l])
    cw_v = _compress_weights(w_cv1[l], w_cv2[l], pe_v[l], g_kc[l])
    causal = jnp.tril(jnp.ones((CHUNK, CHUNK), dtype=bool))
    ws_tril = jnp.where(causal[None], w_s[l], 0.0)
    heads = lambda a, lead: a.reshape(lead + (N_KV, HEAD_DIM))

    n_p = nb * seq
    xp = x_prompt.reshape(n_p, D_MODEL)
    tabs_p = _rope_tables(jnp.arange(seq, dtype=jnp.int32))
    bmix_p = jnp.repeat(b_s[l].T, A_WIDTH // A_GROUPS, axis=1)
    (oa_p, vlast_p, qn_p, qr_p, kc_p, vc_p, ks_p, vs_p, kw_p, vw_p, ks16, vs16, kw16, vw16, gate_p) = _in_proj(
        xp, wts, tabs_p, ws_tril.astype(BF16), bmix_p, tm=512, seg_rows=seq)
    n_chunk_p = seq // CMP_STRIDE
    kcmp_p = _compress(kc_p.reshape(nb, n_chunk_p, CHUNK_LANES), cw_k, norm=True)
    vcmp_p = _compress(vc_p.reshape(nb, n_chunk_p, CHUNK_LANES), cw_v, norm=False)
    n_sel_p = seq // SEL_BLOCK
    n_tiles = seq // KEY_TILE
    kblk = (jnp.arange(seq) // SEL_BLOCK).reshape(n_tiles, 1, KEY_TILE)
    ebt = jnp.where(kblk == jnp.arange(n_sel_p)[None, :, None], MASK_BIAS, 0.0).astype(BF16)
    a_t_p = _block_score_matrix(n_sel_p, n_chunk_p)
    b3 = lambda a: a.reshape(nb, seq, a.shape[-1])
    kst = ks16.reshape(nb, n_tiles, KEY_TILE, KV_WIDTH).transpose(0, 1, 3, 2)
    ob_p = _prompt_attn(b3(qn_p), b3(qr_p), b3(gate_p), kcmp_p, vcmp_p, kst, b3(vs16), b3(kw16), b3(vw16),
                        ebt, a_t_p, n_cmp=n_chunk_p - 1)
    y_p = _merge_ffn(xp, oa_p, ob_p.reshape(n_p, B_WIDTH), wts, tm=512).reshape(nb, seq, D_MODEL)

    n_s = db * dseq
    xs = x_sample.reshape(n_s, D_MODEL)
    tabs_s = _rope_tables(jnp.tile(past + jnp.arange(dseq, dtype=jnp.int32), db))
    wmix_s = jnp.einsum('gpr,bc->gbpcr', ws_tril[:, :dseq, :dseq], jnp.eye(db, dtype=F32)).reshape(
        A_GROUPS, n_s, n_s)
    bmix_s = jnp.tile(bmix_p[:dseq], (db, 1))
    (oa_s, v_s, qn_s, qr_s, kc_s, vc_s, ks_s, vs_s, kw_s, vw_s, ks16s, vs16s, kw16s, vw16s, gate_s) = _in_proj(
        xs, wts, tabs_s, wmix_s.astype(BF16), bmix_s, tm=n_s, seg_rows=n_s)
    pt_flat = page_table.reshape(-1)
    keys_last = lambda a: a.transpose(0, 2, 3, 1).reshape(a.shape[0], KV_WIDTH, a.shape[1])
    kcmp_s = _paged_compress(pt_flat, keys_last(cache_k_cmp[l]), cw_k, nb=db, n_pages=n_pages, norm=True)
    vcmp_s = _paged_compress(pt_flat, keys_last(cache_v_cmp[l]), cw_v, nb=db, n_pages=n_pages, norm=False)
    n_chunk_s = past // CMP_STRIDE
    tokpad = lambda a: jnp.pad(a.reshape(db, dseq, a.shape[-1]), ((0, 0), (0, TOK_PAD - dseq), (0, 0)))
    hh = jnp.arange(ROWS_PAD) // TOK_PAD
    tt = jnp.arange(ROWS_PAD) % TOK_PAD
    gsum = ((hh[:, None] // GROUP == hh[None, :ROWS] // GROUP) & (tt[:, None] == tt[None, :ROWS])
            & (hh[:, None] < N_HEADS)).astype(BF16)
    n_chunks = 2
    chunk_keys = past // n_chunks
    ebt = jnp.where(jnp.arange(chunk_keys)[None, :] // SEL_BLOCK == jnp.arange(chunk_keys // SEL_BLOCK)[:, None],
                    MASK_BIAS, 0.0).astype(BF16)
    ob_s = _sample_attn(
        pt_flat, tokpad(qn_s), tokpad(qr_s), tokpad(gate_s), kcmp_s, vcmp_s,
        keys_last(cache_k_win[l]), keys_last(cache_v_win[l]),
        tokpad(ks16s), tokpad(vs16s), tokpad(kw16s), tokpad(vw16s),
        _block_score_matrix(past // SEL_BLOCK, n_chunk_s), gsum, ebt,
        keys_last(cache_k_sel[l]), keys_last(cache_v_sel[l]),
        n_chunks=n_chunks, pages_per_chunk=n_pages // n_chunks, past=past, n_new=dseq, n_cmp=n_chunk_s - 1)
    y_s = _merge_ffn(xs, oa_s, ob_s[:, :dseq].reshape(n_s, B_WIDTH), wts, tm=n_s).reshape(db, dseq, D_MODEL)

    w_p = min(WINDOW, seq)
    pk = lambda a: heads(a, (nb, seq))[None]
    sk = lambda a: heads(a, (db, dseq))[None]
    return (y_p, y_s, pk(kc_p), pk(vc_p), pk(ks_p), pk(vs_p), pk(kw_p)[:, :, seq - w_p:], pk(vw_p)[:, :, seq - w_p:],
            vlast_p[None], sk(kc_s), sk(vc_s), sk(ks_s), sk(vs_s), sk(kw_s), sk(vw_s),
            v_s.reshape(db, dseq, A_WIDTH)[None])
```

```python
import functools

import jax
import jax.numpy as jnp
from jax import lax
from jax.experimental import pallas as pl
from jax.experimental.pallas import tpu as pltpu

F32 = jnp.float32
BF16 = jnp.bfloat16

D_MODEL = 1024
A_WIDTH = 512
A_GROUPS = 8
CHUNK = 128
N_HEADS = 8
HEAD_DIM = 64
N_KV = 2
GROUP = N_HEADS // N_KV
B_WIDTH = N_HEADS * HEAD_DIM
KV_WIDTH = N_KV * HEAD_DIM
CMP_BLOCK = 32
CMP_STRIDE = 16
SEL_BLOCK = 64
N_SELECT = 16
WINDOW = 512
ROT_DIM = 16
ROPE_THETA = 500000.0
D_FF = 4096
PAGE_SIZE = 128
EPS = 1e-6
NEG = -1e30
FORCED_BONUS = 1e4
MASK_BIAS = -(2.0 ** 100)
SCALE = HEAD_DIM ** -0.5
Q_SCALE = SCALE * 1.4426950408889634

LANES = 128
Q_TILE = 128
KEY_TILE = 2048
CHUNK_LANES = CMP_STRIDE * KV_WIDTH
HEAD_PERM = (0, 4, 1, 5, 2, 6, 3, 7)
VMEM_LIMIT = 56 * 1024 * 1024


def _low_half(shape):
    return lax.broadcasted_iota(jnp.int32, shape, len(shape) - 1) < HEAD_DIM


def _head_rms(xs, g):
    lo = _low_half(xs.shape)
    sq = xs * xs
    sa = jnp.sum(jnp.where(lo, sq, 0.0), axis=-1, keepdims=True)
    sb = jnp.sum(jnp.where(lo, 0.0, sq), axis=-1, keepdims=True)
    ms = jnp.where(lo, sa, sb) * (1.0 / HEAD_DIM)
    return xs * lax.rsqrt(ms + EPS) * g


def _rope(xs, cos, sin_up, sin_dn):
    return xs * cos + pltpu.roll(xs, ROT_DIM // 2, 1) * sin_up + pltpu.roll(xs, LANES - ROT_DIM // 2, 1) * sin_dn


def _rms_rows(x, g):
    return x * lax.rsqrt(jnp.mean(x * x, axis=-1, keepdims=True) + EPS) * g


def _in_proj_body(x_ref, g1_ref, w_ref, lng_ref, lnb_ref, gq_ref, gks_ref, gkw_ref,
                  cos_ref, sup_ref, sdn_ref, wmix_ref, bmix_ref,
                  oa_ref, vlast_ref, qn_ref, qr_ref, kc_ref, vc_ref, ks_ref, vs_ref, kw_ref, vw_ref,
                  ks16_ref, vs16_ref, kw16_ref, vw16_ref, gate_ref, *, tm, tiles_per_seg):
    i = pl.program_id(0)
    h = _rms_rows(x_ref[...], g1_ref[...]).astype(BF16)

    def proj(lo, hi):
        return jnp.dot(h, w_ref[:, lo:hi], preferred_element_type=F32)

    u = jax.nn.gelu(proj(0, A_WIDTH))
    v = jax.nn.gelu(proj(A_WIDTH, 2 * A_WIDTH))
    vc = v - jnp.mean(v, axis=-1, keepdims=True)
    vn = vc * lax.rsqrt(jnp.mean(vc * vc, axis=-1, keepdims=True) + EPS) * lng_ref[...] + lnb_ref[...]

    @pl.when(i % tiles_per_seg == tiles_per_seg - 1)
    def _():
        vlast_ref[0] = vn[tm - CHUNK:tm]

    lo = _low_half((CHUNK, LANES))
    for c in range(tm // CHUNK):
        r0, r1 = c * CHUNK, (c + 1) * CHUNK
        for j in range(A_WIDTH // LANES):
            slab = vn[r0:r1, j * LANES:(j + 1) * LANES]
            a = jnp.where(lo, slab, 0.0).astype(BF16)
            b = jnp.where(lo, 0.0, slab).astype(BF16)
            s = (jnp.dot(wmix_ref[2 * j], a, preferred_element_type=F32)
                 + jnp.dot(wmix_ref[2 * j + 1], b, preferred_element_type=F32)
                 + bmix_ref[:, j * LANES:(j + 1) * LANES])
            oa_ref[r0:r1, j * LANES:(j + 1) * LANES] = (u[r0:r1, j * LANES:(j + 1) * LANES] * s).astype(BF16)

    cos, sup, sdn = cos_ref[...], sup_ref[...], sdn_ref[...]
    q0 = 2 * A_WIDTH
    for j in range(B_WIDTH // LANES):
        qn = _head_rms(proj(q0 + j * LANES, q0 + (j + 1) * LANES), gq_ref[...])
        qn_ref[:, j * LANES:(j + 1) * LANES] = (qn * Q_SCALE).astype(BF16)
        qr_ref[:, j * LANES:(j + 1) * LANES] = (_rope(qn, cos, sup, sdn) * Q_SCALE).astype(BF16)
    k0 = q0 + B_WIDTH
    kc_ref[...] = proj(k0, k0 + LANES)
    vc_ref[...] = proj(k0 + LANES, k0 + 2 * LANES)
    ks = _rope(_head_rms(proj(k0 + 2 * LANES, k0 + 3 * LANES), gks_ref[...]), cos, sup, sdn)
    ks_ref[...] = ks
    ks16_ref[...] = ks.astype(BF16)
    vs = proj(k0 + 3 * LANES, k0 + 4 * LANES)
    vs_ref[...] = vs
    vs16_ref[...] = vs.astype(BF16)
    kw = _rope(_head_rms(proj(k0 + 4 * LANES, k0 + 5 * LANES), gkw_ref[...]), cos, sup, sdn)
    kw_ref[...] = kw
    kw16_ref[...] = kw.astype(BF16)
    vw = proj(k0 + 5 * LANES, k0 + 6 * LANES)
    vw_ref[...] = vw
    vw16_ref[...] = vw.astype(BF16)
    gate_ref[...] = proj(k0 + 6 * LANES, k0 + 7 * LANES)


def _in_proj(x, wts, tabs, wmix, bmix, *, tm, seg_rows):
    n = x.shape[0]
    tiles_per_seg = seg_rows // tm
    n_seg = n // seg_rows
    row = lambda w: pl.BlockSpec((tm, w), lambda i: (i, 0))
    full = lambda a: pl.BlockSpec(a.shape, lambda i: (0,) * a.ndim)
    cos, sup, sdn = tabs
    ins = [x, wts['g1'], wts['w_a'], wts['ln_g'], wts['ln_b'], wts['g_q'], wts['g_ks'], wts['g_kw'],
           cos, sup, sdn, wmix, bmix]
    tab = pl.BlockSpec((tm, LANES), lambda i: (i % tiles_per_seg, 0))
    in_specs = [row(D_MODEL)] + [full(a) for a in ins[1:8]] + [tab] * 3 + [full(wmix), full(bmix)]
    f32o = lambda w: jax.ShapeDtypeStruct((n, w), F32)
    b16o = lambda w: jax.ShapeDtypeStruct((n, w), BF16)
    out_shape = ([b16o(A_WIDTH), jax.ShapeDtypeStruct((n_seg, CHUNK, A_WIDTH), F32), b16o(B_WIDTH), b16o(B_WIDTH)]
                 + [f32o(LANES)] * 6 + [b16o(LANES)] * 4 + [f32o(LANES)])
    out_specs = ([row(A_WIDTH), pl.BlockSpec((1, CHUNK, A_WIDTH), lambda i: (i // tiles_per_seg, 0, 0)),
                  row(B_WIDTH), row(B_WIDTH)] + [row(LANES)] * 11)
    return pl.pallas_call(
        functools.partial(_in_proj_body, tm=tm, tiles_per_seg=tiles_per_seg),
        grid=(n // tm,), in_specs=in_specs, out_specs=out_specs, out_shape=out_shape,
        compiler_params=pltpu.CompilerParams(dimension_semantics=("arbitrary",), vmem_limit_bytes=VMEM_LIMIT),
        name="in_proj")(*ins)


COMPRESS_ROWS = 256


def _compress_math(load_rows, m, w1ab_ref, pe_ref, w1d_ref, w2d_ref, g_ref, norm):
    step = min(COMPRESS_ROWS, m)
    f = jnp.concatenate(
        [jnp.dot(load_rows(r, r + step).astype(BF16), w1ab_ref[...], preferred_element_type=F32)
         for r in range(0, m, step)], axis=0)
    bias = jnp.dot(pe_ref[...], w1d_ref[...], preferred_element_type=F32)[0:1]
    second_next = pltpu.roll(f[:, LANES:], m - 1, 0)
    act = jax.nn.gelu(f[:, :LANES] + second_next + bias)
    out = jnp.dot(act.astype(BF16), w2d_ref[...], preferred_element_type=F32)
    if norm:
        out = _head_rms(out, g_ref[...])
    return out.astype(BF16)


def _compress_body(ch_ref, w1ab_ref, pe_ref, w1d_ref, w2d_ref, g_ref, out_ref, *, norm):
    out_ref[0] = _compress_math(lambda r0, r1: ch_ref[0, r0:r1], ch_ref.shape[1],
                                w1ab_ref, pe_ref, w1d_ref, w2d_ref, g_ref, norm)


def _compress(ch, cw, *, norm):
    nb, m, _ = ch.shape
    full = lambda a: pl.BlockSpec(a.shape, lambda b: (0,) * a.ndim)
    ws = [cw['w1ab'], cw['pe'], cw['w1d'], cw['w2d'], cw['g']]
    return pl.pallas_call(
        functools.partial(_compress_body, norm=norm),
        grid=(nb,),
        in_specs=[pl.BlockSpec((1, m, CHUNK_LANES), lambda b: (b, 0, 0))] + [full(a) for a in ws],
        out_specs=pl.BlockSpec((1, m, LANES), lambda b: (b, 0, 0)),
        out_shape=jax.ShapeDtypeStruct((nb, m, LANES), BF16),
        compiler_params=pltpu.CompilerParams(dimension_semantics=("arbitrary",), vmem_limit_bytes=VMEM_LIMIT),
        name="compress_k" if norm else "compress_v")(ch, *ws)


def _page_copy(pool_ref, page, buf_ref, slot, i, sem_ref):
    return pltpu.make_async_copy(pool_ref.at[page], buf_ref.at[slot, i], sem_ref.at[slot])


def _paged_compress_body(pt_ref, pool_ref, perm_ref, w1ab_ref, pe_ref, w1d_ref, w2d_ref, g_ref, out_ref,
                         buf_ref, sem_ref, ch_ref, *, n_pages, norm):
    b = pl.program_id(0)

    def fetch(bb, slot):
        def one(i, carry):
            _page_copy(pool_ref, pt_ref[bb * n_pages + i], buf_ref, slot, i, sem_ref).start()
            return carry
        lax.fori_loop(0, n_pages, one, 0, unroll=8)

    @pl.when(b == 0)
    def _():
        fetch(0, 0)

    @pl.when(b + 1 < pl.num_programs(0))
    def _():
        fetch(b + 1, (b + 1) % 2)

    slot = b % 2

    pltpu.make_async_copy(pool_ref.at[pl.ds(0, n_pages)], buf_ref.at[slot], sem_ref.at[slot]).wait()

    chunks = PAGE_SIZE // CMP_STRIDE

    def regroup(q, carry):
        xt = jnp.concatenate([buf_ref[slot, 2 * q], buf_ref[slot, 2 * q + 1]], axis=1).astype(BF16)
        y = jnp.dot(xt, perm_ref[...], preferred_element_type=F32)
        t0, t1 = y[:, :PAGE_SIZE].T, y[:, PAGE_SIZE:].T
        rows = pl.ds(pl.multiple_of(q * 2 * chunks, 2 * chunks), 2 * chunks)
        for r in range(CMP_STRIDE):
            piece = jnp.concatenate([t0[r * chunks:(r + 1) * chunks], t1[r * chunks:(r + 1) * chunks]], axis=0)
            ch_ref[rows, r * KV_WIDTH:(r + 1) * KV_WIDTH] = piece.astype(BF16)
        return carry
    lax.fori_loop(0, n_pages // 2, regroup, 0, unroll=16)
    m = n_pages * chunks
    out_ref[0] = _compress_math(lambda r0, r1: ch_ref[r0:r1], m, w1ab_ref, pe_ref, w1d_ref, w2d_ref, g_ref, norm)


def _paged_compress(pt_flat, pool, cw, *, nb, n_pages, norm):
    m = n_pages * (PAGE_SIZE // CMP_STRIDE)
    full = lambda a: pl.BlockSpec(a.shape, lambda b, pt: (0,) * a.ndim)
    lane = jnp.arange(2 * PAGE_SIZE)
    page, chunk, offset = lane // PAGE_SIZE, (lane % PAGE_SIZE) // CMP_STRIDE, lane % CMP_STRIDE
    dest = page * PAGE_SIZE + offset * (PAGE_SIZE // CMP_STRIDE) + chunk
    perm = (dest[:, None] == lane[None, :]).astype(BF16)
    ws = [perm, cw['w1ab'], cw['pe'], cw['w1d'], cw['w2d'], cw['g']]
    grid_spec = pltpu.PrefetchScalarGridSpec(
        num_scalar_prefetch=1, grid=(nb,),
        in_specs=[pl.BlockSpec(memory_space=pl.ANY)] + [full(a) for a in ws],
        out_specs=pl.BlockSpec((1, m, LANES), lambda b, pt: (b, 0, 0)),
        scratch_shapes=[pltpu.VMEM((2, n_pages, KV_WIDTH, PAGE_SIZE), F32), pltpu.SemaphoreType.DMA((2,)),
                        pltpu.VMEM((m, CHUNK_LANES), BF16)])
    return pl.pallas_call(
        functools.partial(_paged_compress_body, n_pages=n_pages, norm=norm),
        grid_spec=grid_spec, out_shape=jax.ShapeDtypeStruct((nb, m, LANES), BF16),
        compiler_params=pltpu.CompilerParams(dimension_semantics=("arbitrary",), vmem_limit_bytes=VMEM_LIMIT),
        name="paged_compress_k" if norm else "paged_compress_v")(pt_flat, pool, *ws)


def _split3(x):
    hi = x.astype(BF16)
    r1 = x - hi.astype(F32)
    mid = r1.astype(BF16)
    lo = (r1 - mid.astype(F32)).astype(BF16)
    return hi, mid, lo


def _dot_nt(a, b):
    return lax.dot_general(a, b, (((1,), (1,)), ((), ())), preferred_element_type=F32)


def _topk_mask_t(rank, n_pick):
    n_blk = rank.shape[0]
    jidx = lax.broadcasted_iota(jnp.int32, rank.shape, 0).astype(F32)

    def body(_, carry):
        r, sel = carry
        m = jnp.max(r, axis=0, keepdims=True)
        first = jnp.min(jnp.where(r == m, jidx, float(n_blk)), axis=0, keepdims=True)
        pick = jidx == first
        return jnp.where(pick, -jnp.inf, r), jnp.where(pick, 1.0, sel)

    _, sel = lax.fori_loop(0, n_pick, body, (rank, jnp.zeros_like(rank)), unroll=True)
    return sel


def _masked_softmax(s, mask, row_valid):
    s = jnp.where(mask, s, NEG)
    e = jnp.exp2(s - jnp.max(s, axis=-1, keepdims=True))
    return e * jnp.where(row_valid, 1.0 / jnp.sum(e, axis=-1, keepdims=True), 0.0)


def _gate_col(gt, head, branch):
    c = head * 3 + branch
    return gt[:, c:c + 1]


def _prompt_attn_body(qn_ref, qr_ref, gate_ref, kc_ref, vc_ref, kst_ref, vs_ref, kw_ref, vw_ref, ebt_ref, at_ref,
                      o_ref, m_ref, l_ref, acc_ref, *, seq, n_cmp):
    i = pl.program_id(1)
    s0 = i * Q_TILE
    nc_pad = kc_ref.shape[1]
    n_sel = seq // SEL_BLOCK
    lo = _low_half((Q_TILE, LANES))
    pos_col = s0 + lax.broadcasted_iota(jnp.int32, (Q_TILE, 1), 0)

    def head_rows(q_ref):
        parts = []
        for hd in range(N_HEADS):
            keep = lo if hd < GROUP else jnp.logical_not(lo)
            j = hd % GROUP
            parts.append(jnp.where(keep, q_ref[0, :, j * LANES:(j + 1) * LANES], 0))
        return jnp.concatenate(parts, axis=0)

    cidx = lax.broadcasted_iota(jnp.int32, (Q_TILE, nc_pad), 1)
    m_c = (cidx * CMP_STRIDE + (CMP_BLOCK - 1) <= pos_col) & (cidx < n_cmp)
    jrow = lax.broadcasted_iota(jnp.int32, (n_sel, Q_TILE), 0)
    pos_row = s0 + lax.broadcasted_iota(jnp.int32, (n_sel, Q_TILE), 1)
    cur = pos_row >> 6
    forced = (jrow == 0) | (jrow == cur) | (jrow == cur - 1)
    valid_b = jrow * SEL_BLOCK <= pos_row
    q_sel = head_rows(qr_ref)
    n_win = WINDOW + Q_TILE
    w0 = pl.multiple_of(jnp.maximum(s0 - WINDOW, 0), Q_TILE)
    dpos = pos_col - (w0 + lax.broadcasted_iota(jnp.int32, (Q_TILE, n_win), 1))
    m_w = (dpos >= 0) & (dpos < WINDOW)
    s_c = _dot_nt(head_rows(qn_ref), kc_ref[0]).reshape(N_HEADS, Q_TILE, nc_pad)
    s_w = _dot_nt(q_sel, kw_ref[0, pl.ds(w0, n_win), :]).reshape(N_HEADS, Q_TILE, n_win)
    p = _masked_softmax(s_c, m_c[None], ((pos_col >= CMP_BLOCK - 1) & (n_cmp > 0))[None])
    o_cmp = jnp.dot(p.reshape(N_HEADS * Q_TILE, nc_pad).astype(BF16), vc_ref[0], preferred_element_type=F32)
    imps = [jnp.sum(p[kv * GROUP:(kv + 1) * GROUP], axis=0) for kv in range(N_KV)]
    p_w = _masked_softmax(s_w, m_w[None], True)
    o_win = jnp.dot(p_w.reshape(N_HEADS * Q_TILE, n_win).astype(BF16), vw_ref[0, pl.ds(w0, n_win), :],
                    preferred_element_type=F32)

    notsel = []
    for kv in range(N_KV):
        imp = imps[kv]
        blk_t = sum(_dot_nt(at_ref[...], part) for part in _split3(imp))
        rank = jnp.where(forced, FORCED_BONUS, jnp.where(valid_b, blk_t, -FORCED_BONUS))
        sel_t = _topk_mask_t(rank, min(N_SELECT, n_sel))
        notsel += [(1.0 - sel_t.T).astype(BF16)] * GROUP

    m_ref[...] = jnp.full(m_ref.shape, NEG, F32)
    l_ref[...] = jnp.zeros(l_ref.shape, F32)
    acc_ref[...] = jnp.zeros(acc_ref.shape, F32)
    q_aug = jnp.concatenate([q_sel, jnp.concatenate(notsel, axis=0)], axis=1)

    def tile_step(tix, diagonal):
        k_aug = jnp.concatenate([kst_ref[0, tix], ebt_ref[tix]], axis=0)
        vt = vs_ref[0, pl.ds(pl.multiple_of(tix * KEY_TILE, KEY_TILE), KEY_TILE), :]
        if diagonal:
            key = tix * KEY_TILE + lax.broadcasted_iota(jnp.int32, (Q_TILE, KEY_TILE), 1)
            future = jnp.where(key <= pos_col, 0.0, NEG)
        half = GROUP * Q_TILE
        scores = [jnp.dot(q_aug[h * half:(h + 1) * half], k_aug, preferred_element_type=F32) for h in range(N_KV)]
        for h in range(N_KV):
            rows = slice(h * half, (h + 1) * half)
            s = scores[h]
            if diagonal:
                s = (s.reshape(GROUP, Q_TILE, KEY_TILE) + future[None]).reshape(half, KEY_TILE)
            m_old = m_ref[rows]
            m_new = jnp.maximum(m_old, jnp.max(s, axis=-1, keepdims=True))
            alpha = jnp.exp2(m_old - m_new)
            p_t = jnp.exp2(s - m_new)
            l_ref[rows] = alpha * l_ref[rows] + jnp.sum(p_t, axis=-1, keepdims=True)
            acc_ref[rows] = alpha * acc_ref[rows] + jnp.dot(p_t.astype(BF16), vt, preferred_element_type=F32)
            m_ref[rows] = m_new

    def tile_body(tix, carry):
        tile_step(tix, False)
        return carry

    q_per_key = KEY_TILE // Q_TILE
    last_tile = lax.shift_right_logical(i, q_per_key.bit_length() - 1)
    lax.fori_loop(0, last_tile, tile_body, 0)
    tile_step(last_tile, True)

    o_sel = acc_ref[...] / l_ref[...]
    gt = jax.nn.sigmoid(gate_ref[0])
    o_heads = []
    for hd in range(N_HEADS):
        rows = slice(hd * Q_TILE, (hd + 1) * Q_TILE)
        o_heads.append(_gate_col(gt, hd, 0) * o_cmp[rows] + _gate_col(gt, hd, 1) * o_sel[rows]
                       + _gate_col(gt, hd, 2) * o_win[rows])
    for j in range(GROUP):
        o_ref[0, :, j * LANES:(j + 1) * LANES] = jnp.where(lo, o_heads[j], o_heads[GROUP + j]).astype(BF16)


def _prompt_attn(qn, qr, gate, kc, vc, kst, vs16, kw16, vw16, ebt, a_t, *, n_cmp):
    nb, seq, _ = qn.shape
    qspec = lambda w: pl.BlockSpec((1, Q_TILE, w), lambda b, i: (b, i, 0))
    seqspec = lambda a: pl.BlockSpec((1,) + a.shape[1:], lambda b, i: (b,) + (0,) * (a.ndim - 1))
    full = lambda a: pl.BlockSpec(a.shape, lambda b, i: (0,) * a.ndim)
    rows = N_HEADS * Q_TILE
    return pl.pallas_call(
        functools.partial(_prompt_attn_body, seq=seq, n_cmp=n_cmp),
        grid=(nb, seq // Q_TILE),
        in_specs=[qspec(B_WIDTH), qspec(B_WIDTH), qspec(LANES), seqspec(kc), seqspec(vc),
                  seqspec(kst), seqspec(vs16), seqspec(kw16), seqspec(vw16), full(ebt), full(a_t)],
        out_specs=qspec(B_WIDTH),
        out_shape=jax.ShapeDtypeStruct((nb, seq, B_WIDTH), BF16),
        scratch_shapes=[pltpu.VMEM((rows, 1), F32), pltpu.VMEM((rows, 1), F32), pltpu.VMEM((rows, LANES), F32)],
        compiler_params=pltpu.CompilerParams(dimension_semantics=("arbitrary", "arbitrary"),
                                             vmem_limit_bytes=VMEM_LIMIT),
        name="prompt_attn")(qn, qr, gate, kc, vc, kst, vs16, kw16, vw16, ebt, a_t)


TOK_PAD = 8
ROWS = N_HEADS * TOK_PAD
ROWS_PAD = 128


def _page_copy_lanes(pool_ref, page, buf_ref, slot, i, sem_ref):
    return pltpu.make_async_copy(pool_ref.at[page], buf_ref.at[slot, :, pl.ds(i * PAGE_SIZE, PAGE_SIZE)],
                                 sem_ref.at[slot])


def _sample_attn_body(pt_ref, qn_ref, qr_ref, gate_ref, kc_ref, vc_ref, kwc_ref, vwc_ref,
                      ksn_ref, vsn_ref, kwn_ref, vwn_ref, at_ref, gsum_ref, ebt_ref, kpool_ref, vpool_ref,
                      o_ref, kbuf_ref, vbuf_ref, ksem_ref, vsem_ref, notsel_ref, m_ref, l_ref, acc_ref, ocw_ref,
                      q_ref, *, n_chunks, pages_per_chunk, past, n_new, n_cmp):
    b = pl.program_id(0)
    c = pl.program_id(1)
    step = b * n_chunks + c
    n_steps = pl.num_programs(0) * n_chunks
    n_past_blk = past // SEL_BLOCK
    blk_per_chunk = n_past_blk // n_chunks

    def fetch(st, slot):
        def one(i, carry):
            page = pt_ref[st * pages_per_chunk + i]
            _page_copy_lanes(kpool_ref, page, kbuf_ref, slot, i, ksem_ref).start()
            _page_copy_lanes(vpool_ref, page, vbuf_ref, slot, i, vsem_ref).start()
            return carry
        lax.fori_loop(0, pages_per_chunk, one, 0, unroll=8)

    @pl.when(step == 0)
    def _():
        fetch(0, 0)

    @pl.when(step + 1 < n_steps)
    def _():
        fetch(step + 1, (step + 1) % 2)

    lo = _low_half((TOK_PAD, LANES))
    tok_col = lax.broadcasted_iota(jnp.int32, (ROWS, 1), 0) % TOK_PAD
    pos_col = past + tok_col

    def rows_q(ref):
        parts = []
        for hd in range(N_HEADS):
            keep = lo if hd < GROUP else jnp.logical_not(lo)
            j = hd % GROUP
            parts.append(jnp.where(keep, ref[0, :, j * LANES:(j + 1) * LANES], 0))
        return jnp.concatenate(parts, axis=0)

    def gate_rows(gt, branch):
        return jnp.concatenate([_gate_col(gt, hd, branch) for hd in range(N_HEADS)], axis=0)

    @pl.when(c == 0)
    def _():
        gt = jax.nn.sigmoid(gate_ref[0])
        nc_pad = kc_ref.shape[1]
        cidx = lax.broadcasted_iota(jnp.int32, (ROWS, nc_pad), 1)
        m_c = (cidx * CMP_STRIDE + (CMP_BLOCK - 1) <= pos_col) & (cidx < n_cmp)
        p = _masked_softmax(_dot_nt(rows_q(qn_ref), kc_ref[0]), m_c, (pos_col >= CMP_BLOCK - 1) & (n_cmp > 0))
        o_c = jnp.dot(p.astype(BF16), vc_ref[0], preferred_element_type=F32)
        imp = sum(jnp.dot(gsum_ref[...], part, preferred_element_type=F32) for part in _split3(p))
        blk_t = sum(_dot_nt(at_ref[...], part) for part in _split3(imp))
        jrow = lax.broadcasted_iota(jnp.int32, blk_t.shape, 0)
        rank = jnp.where((jrow == 0) | (jrow == n_past_blk - 1), FORCED_BONUS, blk_t)
        n_top = min(N_SELECT, n_past_blk + 1)
        sel_t = _topk_mask_t(rank, n_top - 1)
        for cc in range(n_chunks):
            blk = sel_t[cc * blk_per_chunk:(cc + 1) * blk_per_chunk]
            notsel_ref[cc] = (1.0 - blk.T[:ROWS]).astype(BF16)
        q_r = rows_q(qr_ref)
        q_ref[...] = q_r
        n_buf = kwc_ref.shape[2]
        kidx = lax.broadcasted_iota(jnp.int32, (ROWS, n_buf), 1)
        dpos = pos_col - (past - n_buf + kidx)
        m_old = (dpos >= 0) & (dpos < WINDOW)
        knew = lax.broadcasted_iota(jnp.int32, (ROWS, TOK_PAD), 1)
        m_new = (knew <= tok_col) & (knew < n_new)
        s_old = jnp.where(m_old, jnp.dot(q_r, kwc_ref[0].astype(BF16), preferred_element_type=F32), NEG)
        s_new = jnp.where(m_new, _dot_nt(q_r, kwn_ref[0]), NEG)
        mx = jnp.maximum(jnp.max(s_old, axis=-1, keepdims=True), jnp.max(s_new, axis=-1, keepdims=True))
        e_old = jnp.exp2(s_old - mx)
        e_new = jnp.exp2(s_new - mx)
        den = jnp.sum(e_old, axis=-1, keepdims=True) + jnp.sum(e_new, axis=-1, keepdims=True)
        p_old = e_old / den * m_old.astype(F32)
        p_new = e_new / den * m_new.astype(F32)
        o_w = (_dot_nt(p_old.astype(BF16), vwc_ref[0].astype(BF16))
               + jnp.dot(p_new.astype(BF16), vwn_ref[0], preferred_element_type=F32))
        ocw_ref[...] = gate_rows(gt, 0) * o_c + gate_rows(gt, 2) * o_w
        s = jnp.where(m_new, _dot_nt(q_r, ksn_ref[0]), NEG)
        m0 = jnp.max(s, axis=-1, keepdims=True)
        e = jnp.exp2(s - m0) * m_new.astype(F32)
        m_ref[...] = m0
        l_ref[...] = jnp.sum(e, axis=-1, keepdims=True)
        acc_ref[...] = jnp.dot(e.astype(BF16), vsn_ref[0], preferred_element_type=F32)

    slot = step % 2

    pltpu.make_async_copy(kbuf_ref.at[slot], kbuf_ref.at[slot], ksem_ref.at[slot]).wait()
    pltpu.make_async_copy(vbuf_ref.at[slot], vbuf_ref.at[slot], vsem_ref.at[slot]).wait()

    q_aug = jnp.concatenate([q_ref[...], notsel_ref[c]], axis=1)
    k_aug = jnp.concatenate([kbuf_ref[slot].astype(BF16), ebt_ref[...]], axis=0)
    s = jnp.dot(q_aug, k_aug, preferred_element_type=F32)
    m_old = m_ref[...]
    m_new = jnp.maximum(m_old, jnp.max(s, axis=-1, keepdims=True))
    alpha = jnp.exp2(m_old - m_new)
    p = jnp.exp2(s - m_new)
    l_ref[...] = alpha * l_ref[...] + jnp.sum(p, axis=-1, keepdims=True)
    acc_ref[...] = alpha * acc_ref[...] + _dot_nt(p.astype(BF16), vbuf_ref[slot].astype(BF16))
    m_ref[...] = m_new

    @pl.when(c == n_chunks - 1)
    def _():
        gt = jax.nn.sigmoid(gate_ref[0])
        o = ocw_ref[...] + gate_rows(gt, 1) * (acc_ref[...] / l_ref[...])
        for j in range(GROUP):
            top = o[j * TOK_PAD:(j + 1) * TOK_PAD]
            bot = o[(GROUP + j) * TOK_PAD:(GROUP + j + 1) * TOK_PAD]
            o_ref[0, :, j * LANES:(j + 1) * LANES] = jnp.where(lo, top, bot).astype(BF16)


def _sample_attn(pt_flat, qn, qr, gate, kc, vc, kwc, vwc, ksn, vsn, kwn, vwn, a_t, gsum, ebt, kpool, vpool,
                 *, n_chunks, pages_per_chunk, past, n_new, n_cmp):
    nb = qn.shape[0]
    chunk_keys = pages_per_chunk * PAGE_SIZE
    blk_per_chunk = chunk_keys // SEL_BLOCK
    bspec = lambda a: pl.BlockSpec((1,) + a.shape[1:], lambda b, c, pt: (b, 0, 0))
    full = lambda a: pl.BlockSpec(a.shape, lambda b, c, pt: (0,) * a.ndim)
    anyspec = pl.BlockSpec(memory_space=pl.ANY)
    blocked = [qn, qr, gate, kc, vc, kwc, vwc, ksn, vsn, kwn, vwn]
    grid_spec = pltpu.PrefetchScalarGridSpec(
        num_scalar_prefetch=1, grid=(nb, n_chunks),
        in_specs=[bspec(a) for a in blocked] + [full(a_t), full(gsum), full(ebt), anyspec, anyspec],
        out_specs=pl.BlockSpec((1, TOK_PAD, B_WIDTH), lambda b, c, pt: (b, 0, 0)),
        scratch_shapes=[pltpu.VMEM((2, KV_WIDTH, chunk_keys), F32), pltpu.VMEM((2, KV_WIDTH, chunk_keys), F32),
                        pltpu.SemaphoreType.DMA((2,)), pltpu.SemaphoreType.DMA((2,)),
                        pltpu.VMEM((n_chunks, ROWS, blk_per_chunk), BF16),
                        pltpu.VMEM((ROWS, 1), F32), pltpu.VMEM((ROWS, 1), F32), pltpu.VMEM((ROWS, LANES), F32),
                        pltpu.VMEM((ROWS, LANES), F32), pltpu.VMEM((ROWS, LANES), BF16)])
    return pl.pallas_call(
        functools.partial(_sample_attn_body, n_chunks=n_chunks, pages_per_chunk=pages_per_chunk, past=past,
                          n_new=n_new, n_cmp=n_cmp),
        grid_spec=grid_spec, out_shape=jax.ShapeDtypeStruct((nb, TOK_PAD, B_WIDTH), BF16),
        compiler_params=pltpu.CompilerParams(dimension_semantics=("arbitrary", "arbitrary"),
                                             vmem_limit_bytes=VMEM_LIMIT),
        name="sample_attn")(pt_flat, *blocked, a_t, gsum, ebt, kpool, vpool)


FF_TILE = 1024


def _merge_ffn_body(x_ref, oa_ref, ob_ref, g1_ref, wmg_ref, wba_ref, wbb_ref, wout_ref, g2_ref, wup_ref, wdn_ref,
                    y_ref):
    x = x_ref[...]
    h = _rms_rows(x, g1_ref[...]).astype(BF16)
    y_a = jnp.dot(oa_ref[...], wba_ref[...], preferred_element_type=F32)
    y_b = jnp.dot(ob_ref[...], wbb_ref[...], preferred_element_type=F32)
    g_a = jax.nn.sigmoid(jnp.dot(h, wmg_ref[:, :D_MODEL], preferred_element_type=F32))
    g_b = jax.nn.sigmoid(jnp.dot(h, wmg_ref[:, D_MODEL:], preferred_element_type=F32))
    x1 = x + jnp.dot((g_a * y_a + g_b * y_b).astype(BF16), wout_ref[...], preferred_element_type=F32)
    h2 = _rms_rows(x1, g2_ref[...]).astype(BF16)
    y = x1
    for f in range(D_FF // FF_TILE):
        up = jnp.dot(h2, wup_ref[:, f * FF_TILE:(f + 1) * FF_TILE], preferred_element_type=F32)
        act = jnp.square(jnp.maximum(up, 0.0)).astype(BF16)
        y = y + jnp.dot(act, wdn_ref[f * FF_TILE:(f + 1) * FF_TILE, :], preferred_element_type=F32)
    y_ref[...] = y


def _merge_ffn(x, o_a, o_b, wts, *, tm):
    n = x.shape[0]
    row = lambda w: pl.BlockSpec((tm, w), lambda i: (i, 0))
    const = lambda a: pl.BlockSpec(a.shape, lambda i: (0,) * a.ndim, pipeline_mode=pl.Buffered(1))
    ws = [wts['g1'], wts['w_mg'], wts['w_br_a'], wts['w_br_b'], wts['w_out'], wts['g2'], wts['w_up'], wts['w_down']]
    return pl.pallas_call(
        _merge_ffn_body, grid=(n // tm,),
        in_specs=[row(D_MODEL), row(A_WIDTH), row(B_WIDTH)] + [const(a) for a in ws],
        out_specs=row(D_MODEL), out_shape=jax.ShapeDtypeStruct((n, D_MODEL), F32),
        compiler_params=pltpu.CompilerParams(dimension_semantics=("arbitrary",), vmem_limit_bytes=VMEM_LIMIT),
        name="merge_ffn")(x, o_a, o_b, *ws)


def _rope_tables(pos):
    half = ROT_DIM // 2
    lane = jnp.arange(LANES) % HEAD_DIM
    inv = ROPE_THETA ** (-(2 * (lane % half)).astype(F32) / ROT_DIM)
    ang = pos.astype(F32)[:, None] * inv[None, :]
    cos, sin = jnp.cos(ang), jnp.sin(ang)
    cos_t = jnp.where(lane < ROT_DIM, cos, 1.0)
    sin_up = jnp.where((lane >= half) & (lane < ROT_DIM), sin, 0.0)
    sin_dn = jnp.where(lane < half, -sin, 0.0)
    return cos_t, sin_up, sin_dn


def _two(g):
    return jnp.concatenate([g, g])[None, :].astype(F32)


def _layer_weights(l, g_norm1, w_in, ln_v_g, ln_v_b, g_q, g_ks, g_kw, w_branch, w_out, g_norm2, w_up, w_down):
    w = w_in[l]
    q0, k0 = 2 * A_WIDTH, 2 * A_WIDTH + B_WIDTH
    g0 = k0 + 6 * KV_WIDTH
    perm = jnp.array(HEAD_PERM)
    q_cols = w[:, q0:k0].reshape(D_MODEL, N_HEADS, HEAD_DIM)[:, perm].reshape(D_MODEL, B_WIDTH)
    gate_cols = jnp.pad(w[:, g0:g0 + 3 * N_HEADS], ((0, 0), (0, LANES - 3 * N_HEADS)))
    w_a = jnp.concatenate([w[:, :q0], q_cols, w[:, k0:g0], gate_cols], axis=1).astype(BF16)
    wb = w_branch[l]
    w_br_b = wb[A_WIDTH:].reshape(N_HEADS, HEAD_DIM, D_MODEL)[perm].reshape(B_WIDTH, D_MODEL)
    return dict(
        g1=g_norm1[l][None, :], w_a=w_a, ln_g=ln_v_g[l][None, :], ln_b=ln_v_b[l][None, :],
        g_q=_two(g_q[l]), g_ks=_two(g_ks[l]), g_kw=_two(g_kw[l]),
        w_mg=w[:, g0 + 3 * N_HEADS:].astype(BF16), w_br_a=wb[:A_WIDTH].astype(BF16), w_br_b=w_br_b.astype(BF16),
        w_out=w_out[l].astype(BF16), g2=g_norm2[l][None, :], w_up=w_up[l].astype(BF16),
        w_down=w_down[l].astype(BF16))


def _compress_weights(w1, w2, pe, g):
    half = CMP_STRIDE * HEAD_DIM
    eye = jnp.eye(N_KV, dtype=F32)

    def spread(wh):
        return jnp.einsum('rde,kl->rkdle', wh.reshape(CMP_STRIDE, HEAD_DIM, HEAD_DIM), eye).reshape(
            CHUNK_LANES, KV_WIDTH)

    w1ab = jnp.concatenate([spread(w1[:half]), spread(w1[half:])], axis=1).astype(BF16)
    pe_rows = jnp.zeros((8, CMP_BLOCK * HEAD_DIM), F32).at[0].set(pe.reshape(-1)).astype(BF16)
    w1d = jnp.concatenate([w1, w1], axis=1).astype(BF16)
    w2d = jnp.einsum('de,kl->kdle', w2, eye).reshape(KV_WIDTH, KV_WIDTH).astype(BF16)
    return dict(w1ab=w1ab, pe=pe_rows, w1d=w1d, w2d=w2d, g=_two(g))


def _block_score_matrix(n_blk, n_chunk):
    j = jnp.arange(n_blk)[:, None]
    c = jnp.arange(n_chunk)[None, :]
    per = SEL_BLOCK // CMP_STRIDE
    a = ((c >= per * j) & (c <= per * j + per - 1)).astype(F32) + ((c >= per * j - 1) & (c <= per * j + per - 2))
    return a.astype(BF16)


def kernel(x_prompt, x_sample, cache_k_cmp, cache_v_cmp, cache_k_sel, cache_v_sel, cache_k_win, cache_v_win,
           page_table, g_norm1, w_in, ln_v_g, ln_v_b, w_s, b_s, g_q, g_kc, g_ks, g_kw, w_ck1, w_ck2, pe_k,
           w_cv1, w_cv2, pe_v, w_branch, w_out, g_norm2, w_up, w_down):
    nb, seq, _ = x_prompt.shape
    db, dseq, _ = x_sample.shape
    depth = w_in.shape[0]
    n_pages = page_table.shape[1]
    past = n_pages * PAGE_SIZE
    assert depth == 1 and seq % KEY_TILE == 0 and seq >= WINDOW + Q_TILE and dseq <= TOK_PAD
    assert past % (SEL_BLOCK * LANES) == 0 and (past + dseq) // CMP_STRIDE == past // CMP_STRIDE
    l = 0
    wts = _layer_weights(l, g_norm1, w_in, ln_v_g, ln_v_b, g_q, g_ks, g_kw, w_branch, w_out, g_norm2, w_up, w_down)
    cw_k = _compress_weights(w_ck1[l], w_ck2[l], pe_k[l], g_kc[l])
    cw_v = _compress_weights(w_cv1[l], w_cv2[l], pe_v[l], g_kc[l])
    causal = jnp.tril(jnp.ones((CHUNK, CHUNK), dtype=bool))
    ws_tril = jnp.where(causal[None], w_s[l], 0.0)
    heads = lambda a, lead: a.reshape(lead + (N_KV, HEAD_DIM))

    n_p = nb * seq
    xp = x_prompt.reshape(n_p, D_MODEL)
    tabs_p = _rope_tables(jnp.arange(seq, dtype=jnp.int32))
    bmix_p = jnp.repeat(b_s[l].T, A_WIDTH // A_GROUPS, axis=1)
    (oa_p, vlast_p, qn_p, qr_p, kc_p, vc_p, ks_p, vs_p, kw_p, vw_p, ks16, vs16, kw16, vw16, gate_p) = _in_proj(
        xp, wts, tabs_p, ws_tril.astype(BF16), bmix_p, tm=512, seg_rows=seq)
    n_chunk_p = seq // CMP_STRIDE
    kcmp_p = _compress(kc_p.reshape(nb, n_chunk_p, CHUNK_LANES), cw_k, norm=True)
    vcmp_p = _compress(vc_p.reshape(nb, n_chunk_p, CHUNK_LANES), cw_v, norm=False)
    n_sel_p = seq // SEL_BLOCK
    n_tiles = seq // KEY_TILE
    kblk = (jnp.arange(seq) // SEL_BLOCK).reshape(n_tiles, 1, KEY_TILE)
    ebt = jnp.where(kblk == jnp.arange(n_sel_p)[None, :, None], MASK_BIAS, 0.0).astype(BF16)
    a_t_p = _block_score_matrix(n_sel_p, n_chunk_p)
    b3 = lambda a: a.reshape(nb, seq, a.shape[-1])
    kst = ks16.reshape(nb, n_tiles, KEY_TILE, KV_WIDTH).transpose(0, 1, 3, 2)
    ob_p = _prompt_attn(b3(qn_p), b3(qr_p), b3(gate_p), kcmp_p, vcmp_p, kst, b3(vs16), b3(kw16), b3(vw16),
                        ebt, a_t_p, n_cmp=n_chunk_p - 1)
    y_p = _merge_ffn(xp, oa_p, ob_p.reshape(n_p, B_WIDTH), wts, tm=512).reshape(nb, seq, D_MODEL)

    n_s = db * dseq
    xs = x_sample.reshape(n_s, D_MODEL)
    tabs_s = _rope_tables(jnp.tile(past + jnp.arange(dseq, dtype=jnp.int32), db))
    wmix_s = jnp.einsum('gpr,bc->gbpcr', ws_tril[:, :dseq, :dseq], jnp.eye(db, dtype=F32)).reshape(
        A_GROUPS, n_s, n_s)
    bmix_s = jnp.tile(bmix_p[:dseq], (db, 1))
    (oa_s, v_s, qn_s, qr_s, kc_s, vc_s, ks_s, vs_s, kw_s, vw_s, ks16s, vs16s, kw16s, vw16s, gate_s) = _in_proj(
        xs, wts, tabs_s, wmix_s.astype(BF16), bmix_s, tm=n_s, seg_rows=n_s)
    pt_flat = page_table.reshape(-1)
    keys_last = lambda a: a.transpose(0, 2, 3, 1).reshape(a.shape[0], KV_WIDTH, a.shape[1])
    kcmp_s = _paged_compress(pt_flat, keys_last(cache_k_cmp[l]), cw_k, nb=db, n_pages=n_pages, norm=True)
    vcmp_s = _paged_compress(pt_flat, keys_last(cache_v_cmp[l]), cw_v, nb=db, n_pages=n_pages, norm=False)
    n_chunk_s = past // CMP_STRIDE
    tokpad = lambda a: jnp.pad(a.reshape(db, dseq, a.shape[-1]), ((0, 0), (0, TOK_PAD - dseq), (0, 0)))
    hh = jnp.arange(ROWS_PAD) // TOK_PAD
    tt = jnp.arange(ROWS_PAD) % TOK_PAD
    gsum = ((hh[:, None] // GROUP == hh[None, :ROWS] // GROUP) & (tt[:, None] == tt[None, :ROWS])
            & (hh[:, None] < N_HEADS)).astype(BF16)
    n_chunks = 2
    chunk_keys = past // n_chunks
    ebt = jnp.where(jnp.arange(chunk_keys)[None, :] // SEL_BLOCK == jnp.arange(chunk_keys // SEL_BLOCK)[:, None],
                    MASK_BIAS, 0.0).astype(BF16)
    ob_s = _sample_attn(
        pt_flat, tokpad(qn_s), tokpad(qr_s), tokpad(gate_s), kcmp_s, vcmp_s,
        keys_last(cache_k_win[l]), keys_last(cache_v_win[l]),
        tokpad(ks16s), tokpad(vs16s), tokpad(kw16s), tokpad(vw16s),
        _block_score_matrix(past // SEL_BLOCK, n_chunk_s), gsum, ebt,
        keys_last(cache_k_sel[l]), keys_last(cache_v_sel[l]),
        n_chunks=n_chunks, pages_per_chunk=n_pages // n_chunks, past=past, n_new=dseq, n_cmp=n_chunk_s - 1)
    y_s = _merge_ffn(xs, oa_s, ob_s[:, :dseq].reshape(n_s, B_WIDTH), wts, tm=n_s).reshape(db, dseq, D_MODEL)

    w_p = min(WINDOW, seq)
    pk = lambda a: heads(a, (nb, seq))[None]
    sk = lambda a: heads(a, (db, dseq))[None]
    return (y_p, y_s, pk(kc_p), pk(vc_p), pk(ks_p), pk(vs_p), pk(kw_p)[:, :, seq - w_p:], pk(vw_p)[:, :, seq - w_p:],
            vlast_p[None], sk(kc_s), sk(vc_s), sk(ks_s), sk(vs_s), sk(kw_s), sk(vw_s),
            v_s.reshape(db, dseq, A_WIDTH)[None])
```

```python
import functools

import jax
import jax.numpy as jnp
from jax import lax
from jax.experimental import pallas as pl
from jax.experimental.pallas import tpu as pltpu

F32 = jnp.float32
BF16 = jnp.bfloat16

D_MODEL = 1024
A_WIDTH = 512
A_GROUPS = 8
CHUNK = 128
N_HEADS = 8
HEAD_DIM = 64
N_KV = 2
GROUP = N_HEADS // N_KV
B_WIDTH = N_HEADS * HEAD_DIM
KV_WIDTH = N_KV * HEAD_DIM
CMP_BLOCK = 32
CMP_STRIDE = 16
SEL_BLOCK = 64
N_SELECT = 16
WINDOW = 512
ROT_DIM = 16
ROPE_THETA = 500000.0
D_FF = 4096
PAGE_SIZE = 128
EPS = 1e-6
NEG = -1e30
FORCED_BONUS = 1e4
MASK_BIAS = -(2.0 ** 100)
SCALE = HEAD_DIM ** -0.5
Q_SCALE = SCALE * 1.4426950408889634

LANES = 128
Q_TILE = 128
KEY_TILE = 2048
FLASH_SPLITS = 4
CHUNK_LANES = CMP_STRIDE * KV_WIDTH
HEAD_PERM = (0, 4, 1, 5, 2, 6, 3, 7)
VMEM_LIMIT = 56 * 1024 * 1024


def _low_half(shape):
    return lax.broadcasted_iota(jnp.int32, shape, len(shape) - 1) < HEAD_DIM


def _head_rms(xs, g):
    lo = _low_half(xs.shape)
    sq = xs * xs
    sa = jnp.sum(jnp.where(lo, sq, 0.0), axis=-1, keepdims=True)
    sb = jnp.sum(jnp.where(lo, 0.0, sq), axis=-1, keepdims=True)
    ms = jnp.where(lo, sa, sb) * (1.0 / HEAD_DIM)
    return xs * lax.rsqrt(ms + EPS) * g


def _rope(xs, cos, sin_up, sin_dn):
    return xs * cos + pltpu.roll(xs, ROT_DIM // 2, 1) * sin_up + pltpu.roll(xs, LANES - ROT_DIM // 2, 1) * sin_dn


def _rms_rows(x, g):
    return x * lax.rsqrt(jnp.mean(x * x, axis=-1, keepdims=True) + EPS) * g


def _in_proj_body(x_ref, g1_ref, w_ref, lng_ref, lnb_ref, gq_ref, gks_ref, gkw_ref,
                  cos_ref, sup_ref, sdn_ref, wmix_ref, bmix_ref,
                  oa_ref, vlast_ref, qn_ref, qr_ref, kc_ref, vc_ref, ks_ref, vs_ref, kw_ref, vw_ref,
                  ks16_ref, vs16_ref, kw16_ref, vw16_ref, gate_ref, *, tm, tiles_per_seg):
    i = pl.program_id(0)
    h = _rms_rows(x_ref[...], g1_ref[...]).astype(BF16)

    def proj(lo, hi):
        return jnp.dot(h, w_ref[:, lo:hi], preferred_element_type=F32)

    u = jax.nn.gelu(proj(0, A_WIDTH))
    v = jax.nn.gelu(proj(A_WIDTH, 2 * A_WIDTH))
    vc = v - jnp.mean(v, axis=-1, keepdims=True)
    vn = vc * lax.rsqrt(jnp.mean(vc * vc, axis=-1, keepdims=True) + EPS) * lng_ref[...] + lnb_ref[...]

    @pl.when(i % tiles_per_seg == tiles_per_seg - 1)
    def _():
        vlast_ref[0] = vn[tm - CHUNK:tm]

    lo = _low_half((CHUNK, LANES))
    for c in range(tm // CHUNK):
        r0, r1 = c * CHUNK, (c + 1) * CHUNK
        for j in range(A_WIDTH // LANES):
            slab = vn[r0:r1, j * LANES:(j + 1) * LANES]
            a = jnp.where(lo, slab, 0.0).astype(BF16)
            b = jnp.where(lo, 0.0, slab).astype(BF16)
            s = (jnp.dot(wmix_ref[2 * j], a, preferred_element_type=F32)
                 + jnp.dot(wmix_ref[2 * j + 1], b, preferred_element_type=F32)
                 + bmix_ref[:, j * LANES:(j + 1) * LANES])
            oa_ref[r0:r1, j * LANES:(j + 1) * LANES] = (u[r0:r1, j * LANES:(j + 1) * LANES] * s).astype(BF16)

    cos, sup, sdn = cos_ref[...], sup_ref[...], sdn_ref[...]
    q0 = 2 * A_WIDTH
    for j in range(B_WIDTH // LANES):
        qn = _head_rms(proj(q0 + j * LANES, q0 + (j + 1) * LANES), gq_ref[...])
        qn_ref[:, j * LANES:(j + 1) * LANES] = (qn * Q_SCALE).astype(BF16)
        qr_ref[:, j * LANES:(j + 1) * LANES] = (_rope(qn, cos, sup, sdn) * Q_SCALE).astype(BF16)
    k0 = q0 + B_WIDTH
    kc_ref[...] = proj(k0, k0 + LANES)
    vc_ref[...] = proj(k0 + LANES, k0 + 2 * LANES)
    ks = _rope(_head_rms(proj(k0 + 2 * LANES, k0 + 3 * LANES), gks_ref[...]), cos, sup, sdn)
    ks_ref[...] = ks
    ks16_ref[...] = ks.astype(BF16)
    vs = proj(k0 + 3 * LANES, k0 + 4 * LANES)
    vs_ref[...] = vs
    vs16_ref[...] = vs.astype(BF16)
    kw = _rope(_head_rms(proj(k0 + 4 * LANES, k0 + 5 * LANES), gkw_ref[...]), cos, sup, sdn)
    kw_ref[...] = kw
    kw16_ref[...] = kw.astype(BF16)
    vw = proj(k0 + 5 * LANES, k0 + 6 * LANES)
    vw_ref[...] = vw
    vw16_ref[...] = vw.astype(BF16)
    gate_ref[...] = proj(k0 + 6 * LANES, k0 + 7 * LANES)


def _in_proj(x, wts, tabs, wmix, bmix, *, tm, seg_rows):
    n = x.shape[0]
    tiles_per_seg = seg_rows // tm
    n_seg = n // seg_rows
    row = lambda w: pl.BlockSpec((tm, w), lambda i: (i, 0))
    full = lambda a: pl.BlockSpec(a.shape, lambda i: (0,) * a.ndim)
    cos, sup, sdn = tabs
    ins = [x, wts['g1'], wts['w_a'], wts['ln_g'], wts['ln_b'], wts['g_q'], wts['g_ks'], wts['g_kw'],
           cos, sup, sdn, wmix, bmix]
    tab = pl.BlockSpec((tm, LANES), lambda i: (i % tiles_per_seg, 0))
    in_specs = [row(D_MODEL)] + [full(a) for a in ins[1:8]] + [tab] * 3 + [full(wmix), full(bmix)]
    f32o = lambda w: jax.ShapeDtypeStruct((n, w), F32)
    b16o = lambda w: jax.ShapeDtypeStruct((n, w), BF16)
    out_shape = ([b16o(A_WIDTH), jax.ShapeDtypeStruct((n_seg, CHUNK, A_WIDTH), F32), b16o(B_WIDTH), b16o(B_WIDTH)]
                 + [f32o(LANES)] * 6 + [b16o(LANES)] * 4 + [f32o(LANES)])
    out_specs = ([row(A_WIDTH), pl.BlockSpec((1, CHUNK, A_WIDTH), lambda i: (i // tiles_per_seg, 0, 0)),
                  row(B_WIDTH), row(B_WIDTH)] + [row(LANES)] * 11)
    return pl.pallas_call(
        functools.partial(_in_proj_body, tm=tm, tiles_per_seg=tiles_per_seg),
        grid=(n // tm,), in_specs=in_specs, out_specs=out_specs, out_shape=out_shape,
        compiler_params=pltpu.CompilerParams(dimension_semantics=("arbitrary",), vmem_limit_bytes=VMEM_LIMIT),
        name="in_proj")(*ins)


COMPRESS_ROWS = 256


def _compress_math(load_rows, m, w1ab_ref, pe_ref, w1d_ref, w2d_ref, g_ref, norm):
    step = min(COMPRESS_ROWS, m)
    f = jnp.concatenate(
        [jnp.dot(load_rows(r, r + step).astype(BF16), w1ab_ref[...], preferred_element_type=F32)
         for r in range(0, m, step)], axis=0)
    bias = jnp.dot(pe_ref[...], w1d_ref[...], preferred_element_type=F32)[0:1]
    second_next = pltpu.roll(f[:, LANES:], m - 1, 0)
    act = jax.nn.gelu(f[:, :LANES] + second_next + bias)
    out = jnp.dot(act.astype(BF16), w2d_ref[...], preferred_element_type=F32)
    if norm:
        out = _head_rms(out, g_ref[...])
    return out.astype(BF16)


def _compress_body(ch_ref, w1ab_ref, pe_ref, w1d_ref, w2d_ref, g_ref, out_ref, *, norm):
    out_ref[0] = _compress_math(lambda r0, r1: ch_ref[0, r0:r1], ch_ref.shape[1],
                                w1ab_ref, pe_ref, w1d_ref, w2d_ref, g_ref, norm)


def _compress(ch, cw, *, norm):
    nb, m, _ = ch.shape
    full = lambda a: pl.BlockSpec(a.shape, lambda b: (0,) * a.ndim)
    ws = [cw['w1ab'], cw['pe'], cw['w1d'], cw['w2d'], cw['g']]
    return pl.pallas_call(
        functools.partial(_compress_body, norm=norm),
        grid=(nb,),
        in_specs=[pl.BlockSpec((1, m, CHUNK_LANES), lambda b: (b, 0, 0))] + [full(a) for a in ws],
        out_specs=pl.BlockSpec((1, m, LANES), lambda b: (b, 0, 0)),
        out_shape=jax.ShapeDtypeStruct((nb, m, LANES), BF16),
        compiler_params=pltpu.CompilerParams(dimension_semantics=("arbitrary",), vmem_limit_bytes=VMEM_LIMIT),
        name="compress_k" if norm else "compress_v")(ch, *ws)


def _page_copy(pool_ref, page, buf_ref, slot, i, sem_ref):
    return pltpu.make_async_copy(pool_ref.at[page], buf_ref.at[slot, i], sem_ref.at[slot])


def _paged_compress_body(pt_ref, pool_ref, perm_ref, w1ab_ref, pe_ref, w1d_ref, w2d_ref, g_ref, out_ref,
                         buf_ref, sem_ref, ch_ref, *, n_pages, norm):
    b = pl.program_id(0)

    def fetch(bb, slot):
        def one(i, carry):
            _page_copy(pool_ref, pt_ref[bb * n_pages + i], buf_ref, slot, i, sem_ref).start()
            return carry
        lax.fori_loop(0, n_pages, one, 0, unroll=8)

    @pl.when(b == 0)
    def _():
        fetch(0, 0)

    @pl.when(b + 1 < pl.num_programs(0))
    def _():
        fetch(b + 1, (b + 1) % 2)

    slot = b % 2

    pltpu.make_async_copy(pool_ref.at[pl.ds(0, n_pages)], buf_ref.at[slot], sem_ref.at[slot]).wait()

    chunks = PAGE_SIZE // CMP_STRIDE

    def regroup(q, carry):
        xt = jnp.concatenate([buf_ref[slot, 2 * q], buf_ref[slot, 2 * q + 1]], axis=1).astype(BF16)
        y = jnp.dot(xt, perm_ref[...], preferred_element_type=F32)
        t0, t1 = y[:, :PAGE_SIZE].T, y[:, PAGE_SIZE:].T
        rows = pl.ds(pl.multiple_of(q * 2 * chunks, 2 * chunks), 2 * chunks)
        for r in range(CMP_STRIDE):
            piece = jnp.concatenate([t0[r * chunks:(r + 1) * chunks], t1[r * chunks:(r + 1) * chunks]], axis=0)
            ch_ref[rows, r * KV_WIDTH:(r + 1) * KV_WIDTH] = piece.astype(BF16)
        return carry
    lax.fori_loop(0, n_pages // 2, regroup, 0, unroll=16)
    m = n_pages * chunks
    out_ref[0] = _compress_math(lambda r0, r1: ch_ref[r0:r1], m, w1ab_ref, pe_ref, w1d_ref, w2d_ref, g_ref, norm)


def _paged_compress(pt_flat, pool, cw, *, nb, n_pages, norm):
    m = n_pages * (PAGE_SIZE // CMP_STRIDE)
    full = lambda a: pl.BlockSpec(a.shape, lambda b, pt: (0,) * a.ndim)
    lane = jnp.arange(2 * PAGE_SIZE)
    page, chunk, offset = lane // PAGE_SIZE, (lane % PAGE_SIZE) // CMP_STRIDE, lane % CMP_STRIDE
    dest = page * PAGE_SIZE + offset * (PAGE_SIZE // CMP_STRIDE) + chunk
    perm = (dest[:, None] == lane[None, :]).astype(BF16)
    ws = [perm, cw['w1ab'], cw['pe'], cw['w1d'], cw['w2d'], cw['g']]
    grid_spec = pltpu.PrefetchScalarGridSpec(
        num_scalar_prefetch=1, grid=(nb,),
        in_specs=[pl.BlockSpec(memory_space=pl.ANY)] + [full(a) for a in ws],
        out_specs=pl.BlockSpec((1, m, LANES), lambda b, pt: (b, 0, 0)),
        scratch_shapes=[pltpu.VMEM((2, n_pages, KV_WIDTH, PAGE_SIZE), F32), pltpu.SemaphoreType.DMA((2,)),
                        pltpu.VMEM((m, CHUNK_LANES), BF16)])
    return pl.pallas_call(
        functools.partial(_paged_compress_body, n_pages=n_pages, norm=norm),
        grid_spec=grid_spec, out_shape=jax.ShapeDtypeStruct((nb, m, LANES), BF16),
        compiler_params=pltpu.CompilerParams(dimension_semantics=("arbitrary",), vmem_limit_bytes=VMEM_LIMIT),
        name="paged_compress_k" if norm else "paged_compress_v")(pt_flat, pool, *ws)


def _split3(x):
    hi = x.astype(BF16)
    r1 = x - hi.astype(F32)
    mid = r1.astype(BF16)
    lo = (r1 - mid.astype(F32)).astype(BF16)
    return hi, mid, lo


def _dot_nt(a, b):
    return lax.dot_general(a, b, (((1,), (1,)), ((), ())), preferred_element_type=F32)


def _topk_mask_t(rank, n_pick):
    n_blk = rank.shape[0]
    jidx = lax.broadcasted_iota(jnp.int32, rank.shape, 0).astype(F32)

    def body(_, carry):
        r, sel = carry
        m = jnp.max(r, axis=0, keepdims=True)
        first = jnp.min(jnp.where(r == m, jidx, float(n_blk)), axis=0, keepdims=True)
        pick = jidx == first
        return jnp.where(pick, -jnp.inf, r), jnp.where(pick, 1.0, sel)

    _, sel = lax.fori_loop(0, n_pick, body, (rank, jnp.zeros_like(rank)), unroll=True)
    return sel


def _masked_softmax(s, mask, row_valid):
    s = jnp.where(mask, s, NEG)
    e = jnp.exp2(s - jnp.max(s, axis=-1, keepdims=True))
    return e * jnp.where(row_valid, 1.0 / jnp.sum(e, axis=-1, keepdims=True), 0.0)


def _gate_col(gt, head, branch):
    c = head * 3 + branch
    return gt[:, c:c + 1]


def _prompt_attn_body(qn_ref, qr_ref, gate_ref, kc_ref, vc_ref, kst_ref, vs_ref, kw_ref, vw_ref, ebt_ref, at_ref,
                      o_ref, m_ref, l_ref, acc_ref, *, seq, n_cmp):
    i = pl.program_id(1)
    s0 = i * Q_TILE
    nc_pad = kc_ref.shape[1]
    n_sel = seq // SEL_BLOCK
    lo = _low_half((Q_TILE, LANES))
    pos_col = s0 + lax.broadcasted_iota(jnp.int32, (Q_TILE, 1), 0)

    def head_rows(q_ref):
        parts = []
        for hd in range(N_HEADS):
            keep = lo if hd < GROUP else jnp.logical_not(lo)
            j = hd % GROUP
            parts.append(jnp.where(keep, q_ref[0, :, j * LANES:(j + 1) * LANES], 0))
        return jnp.concatenate(parts, axis=0)

    cidx = lax.broadcasted_iota(jnp.int32, (Q_TILE, nc_pad), 1)
    m_c = (cidx * CMP_STRIDE + (CMP_BLOCK - 1) <= pos_col) & (cidx < n_cmp)
    jrow = lax.broadcasted_iota(jnp.int32, (n_sel, Q_TILE), 0)
    pos_row = s0 + lax.broadcasted_iota(jnp.int32, (n_sel, Q_TILE), 1)
    cur = pos_row >> 6
    forced = (jrow == 0) | (jrow == cur) | (jrow == cur - 1)
    valid_b = jrow * SEL_BLOCK <= pos_row
    q_sel = head_rows(qr_ref)
    n_win = WINDOW + Q_TILE
    w0 = pl.multiple_of(jnp.maximum(s0 - WINDOW, 0), Q_TILE)
    dpos = pos_col - (w0 + lax.broadcasted_iota(jnp.int32, (Q_TILE, n_win), 1))
    m_w = (dpos >= 0) & (dpos < WINDOW)
    s_c = _dot_nt(head_rows(qn_ref), kc_ref[0]).reshape(N_HEADS, Q_TILE, nc_pad)
    s_w = _dot_nt(q_sel, kw_ref[0, pl.ds(w0, n_win), :]).reshape(N_HEADS, Q_TILE, n_win)
    p = _masked_softmax(s_c, m_c[None], ((pos_col >= CMP_BLOCK - 1) & (n_cmp > 0))[None])
    o_cmp = jnp.dot(p.reshape(N_HEADS * Q_TILE, nc_pad).astype(BF16), vc_ref[0], preferred_element_type=F32)
    imps = [jnp.sum(p[kv * GROUP:(kv + 1) * GROUP], axis=0) for kv in range(N_KV)]
    p_w = _masked_softmax(s_w, m_w[None], True)
    o_win = jnp.dot(p_w.reshape(N_HEADS * Q_TILE, n_win).astype(BF16), vw_ref[0, pl.ds(w0, n_win), :],
                    preferred_element_type=F32)

    notsel = []
    for kv in range(N_KV):
        imp = imps[kv]
        blk_t = sum(_dot_nt(at_ref[...], part) for part in _split3(imp))
        rank = jnp.where(forced, FORCED_BONUS, jnp.where(valid_b, blk_t, -FORCED_BONUS))
        sel_t = _topk_mask_t(rank, min(N_SELECT, n_sel))
        notsel += [(1.0 - sel_t.T).astype(BF16)] * GROUP

    m_ref[...] = jnp.full(m_ref.shape, NEG, F32)
    l_ref[...] = jnp.zeros(l_ref.shape, F32)
    acc_ref[...] = jnp.zeros(acc_ref.shape, F32)
    q_aug = jnp.concatenate([q_sel, jnp.concatenate(notsel, axis=0)], axis=1)

    def tile_step(tix, lane0, width, diagonal):
        lanes = slice(lane0, lane0 + width)
        key0 = pl.multiple_of(tix * KEY_TILE + lane0, width)
        k_aug = jnp.concatenate([kst_ref[0, tix, :, lanes], ebt_ref[tix, :, lanes]], axis=0)
        vt = vs_ref[0, pl.ds(key0, width), :]
        if diagonal:
            key = key0 + lax.broadcasted_iota(jnp.int32, (Q_TILE, width), 1)
            future = jnp.where(key <= pos_col, 0.0, NEG)
        part = N_HEADS * Q_TILE // FLASH_SPLITS
        scores = [jnp.dot(q_aug[h * part:(h + 1) * part], k_aug, preferred_element_type=F32)
                  for h in range(FLASH_SPLITS)]
        for h in range(FLASH_SPLITS):
            rows = slice(h * part, (h + 1) * part)
            s = scores[h]
            if diagonal:
                s = (s.reshape(part // Q_TILE, Q_TILE, width) + future[None]).reshape(part, width)
            m_old = m_ref[rows]
            m_new = jnp.maximum(m_old, jnp.max(s, axis=-1, keepdims=True))
            alpha = jnp.exp2(m_old - m_new)
            p_t = jnp.exp2(s - m_new)
            l_ref[rows] = alpha * l_ref[rows] + jnp.sum(p_t, axis=-1, keepdims=True)
            acc_ref[rows] = alpha * acc_ref[rows] + jnp.dot(p_t.astype(BF16), vt, preferred_element_type=F32)
            m_ref[rows] = m_new

    def tile_body(tix, carry):
        tile_step(tix, 0, KEY_TILE, False)
        return carry

    q_per_key = KEY_TILE // Q_TILE
    last_tile = lax.shift_right_logical(i, q_per_key.bit_length() - 1)
    lax.fori_loop(0, last_tile, tile_body, 0)
    half = KEY_TILE // 2
    in_second_half = (i & (q_per_key - 1)) >= q_per_key // 2

    @pl.when(in_second_half)
    def _():
        tile_step(last_tile, 0, half, False)
        tile_step(last_tile, half, half, True)

    @pl.when(jnp.logical_not(in_second_half))
    def _():
        tile_step(last_tile, 0, half, True)

    o_sel = acc_ref[...] / l_ref[...]
    gt = jax.nn.sigmoid(gate_ref[0])
    o_heads = []
    for hd in range(N_HEADS):
        rows = slice(hd * Q_TILE, (hd + 1) * Q_TILE)
        o_heads.append(_gate_col(gt, hd, 0) * o_cmp[rows] + _gate_col(gt, hd, 1) * o_sel[rows]
                       + _gate_col(gt, hd, 2) * o_win[rows])
    for j in range(GROUP):
        o_ref[0, :, j * LANES:(j + 1) * LANES] = jnp.where(lo, o_heads[j], o_heads[GROUP + j]).astype(BF16)


def _prompt_attn(qn, qr, gate, kc, vc, kst, vs16, kw16, vw16, ebt, a_t, *, n_cmp):
    nb, seq, _ = qn.shape
    qspec = lambda w: pl.BlockSpec((1, Q_TILE, w), lambda b, i: (b, i, 0))
    seqspec = lambda a: pl.BlockSpec((1,) + a.shape[1:], lambda b, i: (b,) + (0,) * (a.ndim - 1))
    full = lambda a: pl.BlockSpec(a.shape, lambda b, i: (0,) * a.ndim)
    rows = N_HEADS * Q_TILE
    return pl.pallas_call(
        functools.partial(_prompt_attn_body, seq=seq, n_cmp=n_cmp),
        grid=(nb, seq // Q_TILE),
        in_specs=[qspec(B_WIDTH), qspec(B_WIDTH), qspec(LANES), seqspec(kc), seqspec(vc),
                  seqspec(kst), seqspec(vs16), seqspec(kw16), seqspec(vw16), full(ebt), full(a_t)],
        out_specs=qspec(B_WIDTH),
        out_shape=jax.ShapeDtypeStruct((nb, seq, B_WIDTH), BF16),
        scratch_shapes=[pltpu.VMEM((rows, 1), F32), pltpu.VMEM((rows, 1), F32), pltpu.VMEM((rows, LANES), F32)],
        compiler_params=pltpu.CompilerParams(dimension_semantics=("arbitrary", "arbitrary"),
                                             vmem_limit_bytes=VMEM_LIMIT),
        name="prompt_attn")(qn, qr, gate, kc, vc, kst, vs16, kw16, vw16, ebt, a_t)


TOK_PAD = 8
ROWS = N_HEADS * TOK_PAD
ROWS_PAD = 128


def _page_copy_lanes(pool_ref, page, buf_ref, slot, i, sem_ref):
    return pltpu.make_async_copy(pool_ref.at[page], buf_ref.at[slot, :, pl.ds(i * PAGE_SIZE, PAGE_SIZE)],
                                 sem_ref.at[slot])


def _sample_attn_body(pt_ref, qn_ref, qr_ref, gate_ref, kc_ref, vc_ref, kwc_ref, vwc_ref,
                      ksn_ref, vsn_ref, kwn_ref, vwn_ref, at_ref, gsum_ref, ebt_ref, kpool_ref, vpool_ref,
                      o_ref, kbuf_ref, vbuf_ref, ksem_ref, vsem_ref, notsel_ref, m_ref, l_ref, acc_ref, ocw_ref,
                      q_ref, *, n_chunks, pages_per_chunk, past, n_new, n_cmp):
    b = pl.program_id(0)
    c = pl.program_id(1)
    step = b * n_chunks + c
    n_steps = pl.num_programs(0) * n_chunks
    n_past_blk = past // SEL_BLOCK
    blk_per_chunk = n_past_blk // n_chunks

    def fetch(st, slot):
        def one(i, carry):
            page = pt_ref[st * pages_per_chunk + i]
            _page_copy_lanes(kpool_ref, page, kbuf_ref, slot, i, ksem_ref).start()
            _page_copy_lanes(vpool_ref, page, vbuf_ref, slot, i, vsem_ref).start()
            return carry
        lax.fori_loop(0, pages_per_chunk, one, 0, unroll=8)

    @pl.when(step == 0)
    def _():
        fetch(0, 0)

    @pl.when(step + 1 < n_steps)
    def _():
        fetch(step + 1, (step + 1) % 2)

    lo = _low_half((TOK_PAD, LANES))
    tok_col = lax.broadcasted_iota(jnp.int32, (ROWS, 1), 0) % TOK_PAD
    pos_col = past + tok_col

    def rows_q(ref):
        parts = []
        for hd in range(N_HEADS):
            keep = lo if hd < GROUP else jnp.logical_not(lo)
            j = hd % GROUP
            parts.append(jnp.where(keep, ref[0, :, j * LANES:(j + 1) * LANES], 0))
        return jnp.concatenate(parts, axis=0)

    def gate_rows(gt, branch):
        return jnp.concatenate([_gate_col(gt, hd, branch) for hd in range(N_HEADS)], axis=0)

    @pl.when(c == 0)
    def _():
        gt = jax.nn.sigmoid(gate_ref[0])
        nc_pad = kc_ref.shape[1]
        cidx = lax.broadcasted_iota(jnp.int32, (ROWS, nc_pad), 1)
        m_c = (cidx * CMP_STRIDE + (CMP_BLOCK - 1) <= pos_col) & (cidx < n_cmp)
        p = _masked_softmax(_dot_nt(rows_q(qn_ref), kc_ref[0]), m_c, (pos_col >= CMP_BLOCK - 1) & (n_cmp > 0))
        o_c = jnp.dot(p.astype(BF16), vc_ref[0], preferred_element_type=F32)
        imp = sum(jnp.dot(gsum_ref[...], part, preferred_element_type=F32) for part in _split3(p))
        blk_t = sum(_dot_nt(at_ref[...], part) for part in _split3(imp))
        jrow = lax.broadcasted_iota(jnp.int32, blk_t.shape, 0)
        rank = jnp.where((jrow == 0) | (jrow == n_past_blk - 1), FORCED_BONUS, blk_t)
        n_top = min(N_SELECT, n_past_blk + 1)
        sel_t = _topk_mask_t(rank, n_top - 1)
        for cc in range(n_chunks):
            blk = sel_t[cc * blk_per_chunk:(cc + 1) * blk_per_chunk]
            notsel_ref[cc] = (1.0 - blk.T[:ROWS]).astype(BF16)
        q_r = rows_q(qr_ref)
        q_ref[...] = q_r
        n_buf = kwc_ref.shape[2]
        kidx = lax.broadcasted_iota(jnp.int32, (ROWS, n_buf), 1)
        dpos = pos_col - (past - n_buf + kidx)
        m_old = (dpos >= 0) & (dpos < WINDOW)
        knew = lax.broadcasted_iota(jnp.int32, (ROWS, TOK_PAD), 1)
        m_new = (knew <= tok_col) & (knew < n_new)
        s_old = jnp.where(m_old, jnp.dot(q_r, kwc_ref[0].astype(BF16), preferred_element_type=F32), NEG)
        s_new = jnp.where(m_new, _dot_nt(q_r, kwn_ref[0]), NEG)
        mx = jnp.maximum(jnp.max(s_old, axis=-1, keepdims=True), jnp.max(s_new, axis=-1, keepdims=True))
        e_old = jnp.exp2(s_old - mx)
        e_new = jnp.exp2(s_new - mx)
        den = jnp.sum(e_old, axis=-1, keepdims=True) + jnp.sum(e_new, axis=-1, keepdims=True)
        p_old = e_old / den * m_old.astype(F32)
        p_new = e_new / den * m_new.astype(F32)
        o_w = (_dot_nt(p_old.astype(BF16), vwc_ref[0].astype(BF16))
               + jnp.dot(p_new.astype(BF16), vwn_ref[0], preferred_element_type=F32))
        ocw_ref[...] = gate_rows(gt, 0) * o_c + gate_rows(gt, 2) * o_w
        s = jnp.where(m_new, _dot_nt(q_r, ksn_ref[0]), NEG)
        m0 = jnp.max(s, axis=-1, keepdims=True)
        e = jnp.exp2(s - m0) * m_new.astype(F32)
        m_ref[...] = m0
        l_ref[...] = jnp.sum(e, axis=-1, keepdims=True)
        acc_ref[...] = jnp.dot(e.astype(BF16), vsn_ref[0], preferred_element_type=F32)

    slot = step % 2

    pltpu.make_async_copy(kbuf_ref.at[slot], kbuf_ref.at[slot], ksem_ref.at[slot]).wait()
    pltpu.make_async_copy(vbuf_ref.at[slot], vbuf_ref.at[slot], vsem_ref.at[slot]).wait()

    q_aug = jnp.concatenate([q_ref[...], notsel_ref[c]], axis=1)
    k_aug = jnp.concatenate([kbuf_ref[slot].astype(BF16), ebt_ref[...]], axis=0)
    s = jnp.dot(q_aug, k_aug, preferred_element_type=F32)
    m_old = m_ref[...]
    m_new = jnp.maximum(m_old, jnp.max(s, axis=-1, keepdims=True))
    alpha = jnp.exp2(m_old - m_new)
    p = jnp.exp2(s - m_new)
    l_ref[...] = alpha * l_ref[...] + jnp.sum(p, axis=-1, keepdims=True)
    acc_ref[...] = alpha * acc_ref[...] + _dot_nt(p.astype(BF16), vbuf_ref[slot].astype(BF16))
    m_ref[...] = m_new

    @pl.when(c == n_chunks - 1)
    def _():
        gt = jax.nn.sigmoid(gate_ref[0])
        o = ocw_ref[...] + gate_rows(gt, 1) * (acc_ref[...] / l_ref[...])
        for j in range(GROUP):
            top = o[j * TOK_PAD:(j + 1) * TOK_PAD]
            bot = o[(GROUP + j) * TOK_PAD:(GROUP + j + 1) * TOK_PAD]
            o_ref[0, :, j * LANES:(j + 1) * LANES] = jnp.where(lo, top, bot).astype(BF16)


def _sample_attn(pt_flat, qn, qr, gate, kc, vc, kwc, vwc, ksn, vsn, kwn, vwn, a_t, gsum, ebt, kpool, vpool,
                 *, n_chunks, pages_per_chunk, past, n_new, n_cmp):
    nb = qn.shape[0]
    chunk_keys = pages_per_chunk * PAGE_SIZE
    blk_per_chunk = chunk_keys // SEL_BLOCK
    bspec = lambda a: pl.BlockSpec((1,) + a.shape[1:], lambda b, c, pt: (b, 0, 0))
    full = lambda a: pl.BlockSpec(a.shape, lambda b, c, pt: (0,) * a.ndim)
    anyspec = pl.BlockSpec(memory_space=pl.ANY)
    blocked = [qn, qr, gate, kc, vc, kwc, vwc, ksn, vsn, kwn, vwn]
    grid_spec = pltpu.PrefetchScalarGridSpec(
        num_scalar_prefetch=1, grid=(nb, n_chunks),
        in_specs=[bspec(a) for a in blocked] + [full(a_t), full(gsum), full(ebt), anyspec, anyspec],
        out_specs=pl.BlockSpec((1, TOK_PAD, B_WIDTH), lambda b, c, pt: (b, 0, 0)),
        scratch_shapes=[pltpu.VMEM((2, KV_WIDTH, chunk_keys), F32), pltpu.VMEM((2, KV_WIDTH, chunk_keys), F32),
                        pltpu.SemaphoreType.DMA((2,)), pltpu.SemaphoreType.DMA((2,)),
                        pltpu.VMEM((n_chunks, ROWS, blk_per_chunk), BF16),
                        pltpu.VMEM((ROWS, 1), F32), pltpu.VMEM((ROWS, 1), F32), pltpu.VMEM((ROWS, LANES), F32),
                        pltpu.VMEM((ROWS, LANES), F32), pltpu.VMEM((ROWS, LANES), BF16)])
    return pl.pallas_call(
        functools.partial(_sample_attn_body, n_chunks=n_chunks, pages_per_chunk=pages_per_chunk, past=past,
                          n_new=n_new, n_cmp=n_cmp),
        grid_spec=grid_spec, out_shape=jax.ShapeDtypeStruct((nb, TOK_PAD, B_WIDTH), BF16),
        compiler_params=pltpu.CompilerParams(dimension_semantics=("arbitrary", "arbitrary"),
                                             vmem_limit_bytes=VMEM_LIMIT),
        name="sample_attn")(pt_flat, *blocked, a_t, gsum, ebt, kpool, vpool)


FF_TILE = 1024


def _merge_ffn_body(x_ref, oa_ref, ob_ref, g1_ref, wmg_ref, wba_ref, wbb_ref, wout_ref, g2_ref, wup_ref, wdn_ref,
                    y_ref):
    x = x_ref[...]
    h = _rms_rows(x, g1_ref[...]).astype(BF16)
    y_a = jnp.dot(oa_ref[...], wba_ref[...], preferred_element_type=F32)
    y_b = jnp.dot(ob_ref[...], wbb_ref[...], preferred_element_type=F32)
    g_a = jax.nn.sigmoid(jnp.dot(h, wmg_ref[:, :D_MODEL], preferred_element_type=F32))
    g_b = jax.nn.sigmoid(jnp.dot(h, wmg_ref[:, D_MODEL:], preferred_element_type=F32))
    x1 = x + jnp.dot((g_a * y_a + g_b * y_b).astype(BF16), wout_ref[...], preferred_element_type=F32)
    h2 = _rms_rows(x1, g2_ref[...]).astype(BF16)
    y = x1
    for f in range(D_FF // FF_TILE):
        up = jnp.dot(h2, wup_ref[:, f * FF_TILE:(f + 1) * FF_TILE], preferred_element_type=F32)
        act = jnp.square(jnp.maximum(up, 0.0)).astype(BF16)
        y = y + jnp.dot(act, wdn_ref[f * FF_TILE:(f + 1) * FF_TILE, :], preferred_element_type=F32)
    y_ref[...] = y


def _merge_ffn(x, o_a, o_b, wts, *, tm):
    n = x.shape[0]
    row = lambda w: pl.BlockSpec((tm, w), lambda i: (i, 0))
    const = lambda a: pl.BlockSpec(a.shape, lambda i: (0,) * a.ndim, pipeline_mode=pl.Buffered(1))
    ws = [wts['g1'], wts['w_mg'], wts['w_br_a'], wts['w_br_b'], wts['w_out'], wts['g2'], wts['w_up'], wts['w_down']]
    return pl.pallas_call(
        _merge_ffn_body, grid=(n // tm,),
        in_specs=[row(D_MODEL), row(A_WIDTH), row(B_WIDTH)] + [const(a) for a in ws],
        out_specs=row(D_MODEL), out_shape=jax.ShapeDtypeStruct((n, D_MODEL), F32),
        compiler_params=pltpu.CompilerParams(dimension_semantics=("arbitrary",), vmem_limit_bytes=VMEM_LIMIT),
        name="merge_ffn")(x, o_a, o_b, *ws)


def _rope_tables(pos):
    half = ROT_DIM // 2
    lane = jnp.arange(LANES) % HEAD_DIM
    inv = ROPE_THETA ** (-(2 * (lane % half)).astype(F32) / ROT_DIM)
    ang = pos.astype(F32)[:, None] * inv[None, :]
    cos, sin = jnp.cos(ang), jnp.sin(ang)
    cos_t = jnp.where(lane < ROT_DIM, cos, 1.0)
    sin_up = jnp.where((lane >= half) & (lane < ROT_DIM), sin, 0.0)
    sin_dn = jnp.where(lane < half, -sin, 0.0)
    return cos_t, sin_up, sin_dn


def _two(g):
    return jnp.concatenate([g, g])[None, :].astype(F32)


def _layer_weights(l, g_norm1, w_in, ln_v_g, ln_v_b, g_q, g_ks, g_kw, w_branch, w_out, g_norm2, w_up, w_down):
    w = w_in[l]
    q0, k0 = 2 * A_WIDTH, 2 * A_WIDTH + B_WIDTH
    g0 = k0 + 6 * KV_WIDTH
    perm = jnp.array(HEAD_PERM)
    q_cols = w[:, q0:k0].reshape(D_MODEL, N_HEADS, HEAD_DIM)[:, perm].reshape(D_MODEL, B_WIDTH)
    gate_cols = jnp.pad(w[:, g0:g0 + 3 * N_HEADS], ((0, 0), (0, LANES - 3 * N_HEADS)))
    w_a = jnp.concatenate([w[:, :q0], q_cols, w[:, k0:g0], gate_cols], axis=1).astype(BF16)
    wb = w_branch[l]
    w_br_b = wb[A_WIDTH:].reshape(N_HEADS, HEAD_DIM, D_MODEL)[perm].reshape(B_WIDTH, D_MODEL)
    return dict(
        g1=g_norm1[l][None, :], w_a=w_a, ln_g=ln_v_g[l][None, :], ln_b=ln_v_b[l][None, :],
        g_q=_two(g_q[l]), g_ks=_two(g_ks[l]), g_kw=_two(g_kw[l]),
        w_mg=w[:, g0 + 3 * N_HEADS:].astype(BF16), w_br_a=wb[:A_WIDTH].astype(BF16), w_br_b=w_br_b.astype(BF16),
        w_out=w_out[l].astype(BF16), g2=g_norm2[l][None, :], w_up=w_up[l].astype(BF16),
        w_down=w_down[l].astype(BF16))


def _compress_weights(w1, w2, pe, g):
    half = CMP_STRIDE * HEAD_DIM
    eye = jnp.eye(N_KV, dtype=F32)

    def spread(wh):
        return jnp.einsum('rde,kl->rkdle', wh.reshape(CMP_STRIDE, HEAD_DIM, HEAD_DIM), eye).reshape(
            CHUNK_LANES, KV_WIDTH)

    w1ab = jnp.concatenate([spread(w1[:half]), spread(w1[half:])], axis=1).astype(BF16)
    pe_rows = jnp.zeros((8, CMP_BLOCK * HEAD_DIM), F32).at[0].set(pe.reshape(-1)).astype(BF16)
    w1d = jnp.concatenate([w1, w1], axis=1).astype(BF16)
    w2d = jnp.einsum('de,kl->kdle', w2, eye).reshape(KV_WIDTH, KV_WIDTH).astype(BF16)
    return dict(w1ab=w1ab, pe=pe_rows, w1d=w1d, w2d=w2d, g=_two(g))


def _block_score_matrix(n_blk, n_chunk):
    j = jnp.arange(n_blk)[:, None]
    c = jnp.arange(n_chunk)[None, :]
    per = SEL_BLOCK // CMP_STRIDE
    a = ((c >= per * j) & (c <= per * j + per - 1)).astype(F32) + ((c >= per * j - 1) & (c <= per * j + per - 2))
    return a.astype(BF16)


def kernel(x_prompt, x_sample, cache_k_cmp, cache_v_cmp, cache_k_sel, cache_v_sel, cache_k_win, cache_v_win,
           page_table, g_norm1, w_in, ln_v_g, ln_v_b, w_s, b_s, g_q, g_kc, g_ks, g_kw, w_ck1, w_ck2, pe_k,
           w_cv1, w_cv2, pe_v, w_branch, w_out, g_norm2, w_up, w_down):
    nb, seq, _ = x_prompt.shape
    db, dseq, _ = x_sample.shape
    depth = w_in.shape[0]
    n_pages = page_table.shape[1]
    past = n_pages * PAGE_SIZE
    assert depth == 1 and seq % KEY_TILE == 0 and seq >= WINDOW + Q_TILE and dseq <= TOK_PAD
    assert past % (SEL_BLOCK * LANES) == 0 and (past + dseq) // CMP_STRIDE == past // CMP_STRIDE
    l = 0
    wts = _layer_weights(l, g_norm1, w_in, ln_v_g, ln_v_b, g_q, g_ks, g_kw, w_branch, w_out, g_norm2, w_up, w_down)
    cw_k = _compress_weights(w_ck1[l], w_ck2[l], pe_k[l], g_kc[l])
    cw_v = _compress_weights(w_cv1[l], w_cv2[l], pe_v[l], g_kc[l])
    causal = jnp.tril(jnp.ones((CHUNK, CHUNK), dtype=bool))
    ws_tril = jnp.where(causal[None], w_s[l], 0.0)
    heads = lambda a, lead: a.reshape(lead + (N_KV, HEAD_DIM))

    n_p = nb * seq
    xp = x_prompt.reshape(n_p, D_MODEL)
    tabs_p = _rope_tables(jnp.arange(seq, dtype=jnp.int32))
    bmix_p = jnp.repeat(b_s[l].T, A_WIDTH // A_GROUPS, axis=1)
    (oa_p, vlast_p, qn_p, qr_p, kc_p, vc_p, ks_p, vs_p, kw_p, vw_p, ks16, vs16, kw16, vw16, gate_p) = _in_proj(
        xp, wts, tabs_p, ws_tril.astype(BF16), bmix_p, tm=512, seg_rows=seq)
    n_chunk_p = seq // CMP_STRIDE
    kcmp_p = _compress(kc_p.reshape(nb, n_chunk_p, CHUNK_LANES), cw_k, norm=True)
    vcmp_p = _compress(vc_p.reshape(nb, n_chunk_p, CHUNK_LANES), cw_v, norm=False)
    n_sel_p = seq // SEL_BLOCK
    n_tiles = seq // KEY_TILE
    kblk = (jnp.arange(seq) // SEL_BLOCK).reshape(n_tiles, 1, KEY_TILE)
    ebt = jnp.where(kblk == jnp.arange(n_sel_p)[None, :, None], MASK_BIAS, 0.0).astype(BF16)
    a_t_p = _block_score_matrix(n_sel_p, n_chunk_p)
    b3 = lambda a: a.reshape(nb, seq, a.shape[-1])
    kst = ks16.reshape(nb, n_tiles, KEY_TILE, KV_WIDTH).transpose(0, 1, 3, 2)
    ob_p = _prompt_attn(b3(qn_p), b3(qr_p), b3(gate_p), kcmp_p, vcmp_p, kst, b3(vs16), b3(kw16), b3(vw16),
                        ebt, a_t_p, n_cmp=n_chunk_p - 1)
    y_p = _merge_ffn(xp, oa_p, ob_p.reshape(n_p, B_WIDTH), wts, tm=512).reshape(nb, seq, D_MODEL)

    n_s = db * dseq
    xs = x_sample.reshape(n_s, D_MODEL)
    tabs_s = _rope_tables(jnp.tile(past + jnp.arange(dseq, dtype=jnp.int32), db))
    wmix_s = jnp.einsum('gpr,bc->gbpcr', ws_tril[:, :dseq, :dseq], jnp.eye(db, dtype=F32)).reshape(
        A_GROUPS, n_s, n_s)
    bmix_s = jnp.tile(bmix_p[:dseq], (db, 1))
    (oa_s, v_s, qn_s, qr_s, kc_s, vc_s, ks_s, vs_s, kw_s, vw_s, ks16s, vs16s, kw16s, vw16s, gate_s) = _in_proj(
        xs, wts, tabs_s, wmix_s.astype(BF16), bmix_s, tm=n_s, seg_rows=n_s)
    pt_flat = page_table.reshape(-1)
    keys_last = lambda a: a.transpose(0, 2, 3, 1).reshape(a.shape[0], KV_WIDTH, a.shape[1])
    kcmp_s = _paged_compress(pt_flat, keys_last(cache_k_cmp[l]), cw_k, nb=db, n_pages=n_pages, norm=True)
    vcmp_s = _paged_compress(pt_flat, keys_last(cache_v_cmp[l]), cw_v, nb=db, n_pages=n_pages, norm=False)
    n_chunk_s = past // CMP_STRIDE
    tokpad = lambda a: jnp.pad(a.reshape(db, dseq, a.shape[-1]), ((0, 0), (0, TOK_PAD - dseq), (0, 0)))
    hh = jnp.arange(ROWS_PAD) // TOK_PAD
    tt = jnp.arange(ROWS_PAD) % TOK_PAD
    gsum = ((hh[:, None] // GROUP == hh[None, :ROWS] // GROUP) & (tt[:, None] == tt[None, :ROWS])
            & (hh[:, None] < N_HEADS)).astype(BF16)
    n_chunks = 2
    chunk_keys = past // n_chunks
    ebt = jnp.where(jnp.arange(chunk_keys)[None, :] // SEL_BLOCK == jnp.arange(chunk_keys // SEL_BLOCK)[:, None],
                    MASK_BIAS, 0.0).astype(BF16)
    ob_s = _sample_attn(
        pt_flat, tokpad(qn_s), tokpad(qr_s), tokpad(gate_s), kcmp_s, vcmp_s,
        keys_last(cache_k_win[l]), keys_last(cache_v_win[l]),
        tokpad(ks16s), tokpad(vs16s), tokpad(kw16s), tokpad(vw16s),
        _block_score_matrix(past // SEL_BLOCK, n_chunk_s), gsum, ebt,
        keys_last(cache_k_sel[l]), keys_last(cache_v_sel[l]),
        n_chunks=n_chunks, pages_per_chunk=n_pages // n_chunks, past=past, n_new=dseq, n_cmp=n_chunk_s - 1)
    y_s = _merge_ffn(xs, oa_s, ob_s[:, :dseq].reshape(n_s, B_WIDTH), wts, tm=n_s).reshape(db, dseq, D_MODEL)

    w_p = min(WINDOW, seq)
    pk = lambda a: heads(a, (nb, seq))[None]
    sk = lambda a: heads(a, (db, dseq))[None]
    return (y_p, y_s, pk(kc_p), pk(vc_p), pk(ks_p), pk(vs_p), pk(kw_p)[:, :, seq - w_p:], pk(vw_p)[:, :, seq - w_p:],
            vlast_p[None], sk(kc_s), sk(vc_s), sk(ks_s), sk(vs_s), sk(kw_s), sk(vw_s),
            v_s.reshape(db, dseq, A_WIDTH)[None])
```

```python
import functools

import jax
import jax.numpy as jnp
from jax import lax
from jax.experimental import pallas as pl
from jax.experimental.pallas import tpu as pltpu

F32 = jnp.float32
BF16 = jnp.bfloat16

D_MODEL = 1024
A_WIDTH = 512
A_GROUPS = 8
CHUNK = 128
N_HEADS = 8
HEAD_DIM = 64
N_KV = 2
GROUP = N_HEADS // N_KV
B_WIDTH = N_HEADS * HEAD_DIM
KV_WIDTH = N_KV * HEAD_DIM
CMP_BLOCK = 32
CMP_STRIDE = 16
SEL_BLOCK = 64
N_SELECT = 16
WINDOW = 512
ROT_DIM = 16
ROPE_THETA = 500000.0
D_FF = 4096
PAGE_SIZE = 128
EPS = 1e-6
NEG = -1e30
FORCED_BONUS = 1e4
MASK_BIAS = -(2.0 ** 100)
SCALE = HEAD_DIM ** -0.5
Q_SCALE = SCALE * 1.4426950408889634

LANES = 128
Q_TILE = 128
KEY_TILE = 2048
FLASH_SPLITS = 4
CHUNK_LANES = CMP_STRIDE * KV_WIDTH
HEAD_PERM = (0, 4, 1, 5, 2, 6, 3, 7)
VMEM_LIMIT = 56 * 1024 * 1024


def _low_half(shape):
    return lax.broadcasted_iota(jnp.int32, shape, len(shape) - 1) < HEAD_DIM


def _head_rms(xs, g):
    lo = _low_half(xs.shape)
    sq = xs * xs
    sa = jnp.sum(jnp.where(lo, sq, 0.0), axis=-1, keepdims=True)
    sb = jnp.sum(jnp.where(lo, 0.0, sq), axis=-1, keepdims=True)
    ms = jnp.where(lo, sa, sb) * (1.0 / HEAD_DIM)
    return xs * lax.rsqrt(ms + EPS) * g


def _rope(xs, cos, sin_up, sin_dn):
    return xs * cos + pltpu.roll(xs, ROT_DIM // 2, 1) * sin_up + pltpu.roll(xs, LANES - ROT_DIM // 2, 1) * sin_dn


def _rms_rows(x, g):
    return x * lax.rsqrt(jnp.mean(x * x, axis=-1, keepdims=True) + EPS) * g


def _in_proj_body(x_ref, g1_ref, w_ref, lng_ref, lnb_ref, gq_ref, gks_ref, gkw_ref,
                  cos_ref, sup_ref, sdn_ref, wmix_ref, bmix_ref,
                  oa_ref, vlast_ref, qn_ref, qr_ref, kc_ref, vc_ref, ks_ref, vs_ref, kw_ref, vw_ref,
                  kc16_ref, vc16_ref, ks16_ref, vs16_ref, kw16_ref, vw16_ref, gate_ref, *, tm, tiles_per_seg):
    i = pl.program_id(0)
    h = _rms_rows(x_ref[...], g1_ref[...]).astype(BF16)

    def proj(lo, hi):
        return jnp.dot(h, w_ref[:, lo:hi], preferred_element_type=F32)

    u = jax.nn.gelu(proj(0, A_WIDTH))
    v = jax.nn.gelu(proj(A_WIDTH, 2 * A_WIDTH))
    vc = v - jnp.mean(v, axis=-1, keepdims=True)
    vn = vc * lax.rsqrt(jnp.mean(vc * vc, axis=-1, keepdims=True) + EPS) * lng_ref[...] + lnb_ref[...]

    @pl.when(i % tiles_per_seg == tiles_per_seg - 1)
    def _():
        vlast_ref[0] = vn[tm - CHUNK:tm]

    lo = _low_half((CHUNK, LANES))
    for c in range(tm // CHUNK):
        r0, r1 = c * CHUNK, (c + 1) * CHUNK
        for j in range(A_WIDTH // LANES):
            slab = vn[r0:r1, j * LANES:(j + 1) * LANES]
            a = jnp.where(lo, slab, 0.0).astype(BF16)
            b = jnp.where(lo, 0.0, slab).astype(BF16)
            s = (jnp.dot(wmix_ref[2 * j], a, preferred_element_type=F32)
                 + jnp.dot(wmix_ref[2 * j + 1], b, preferred_element_type=F32)
                 + bmix_ref[:, j * LANES:(j + 1) * LANES])
            oa_ref[r0:r1, j * LANES:(j + 1) * LANES] = (u[r0:r1, j * LANES:(j + 1) * LANES] * s).astype(BF16)

    cos, sup, sdn = cos_ref[...], sup_ref[...], sdn_ref[...]
    q0 = 2 * A_WIDTH
    for j in range(B_WIDTH // LANES):
        qn = _head_rms(proj(q0 + j * LANES, q0 + (j + 1) * LANES), gq_ref[...])
        qn_ref[:, j * LANES:(j + 1) * LANES] = (qn * Q_SCALE).astype(BF16)
        qr_ref[:, j * LANES:(j + 1) * LANES] = (_rope(qn, cos, sup, sdn) * Q_SCALE).astype(BF16)
    k0 = q0 + B_WIDTH

    def emit(val, state_ref, bf16_ref):
        state_ref[0] = val.T
        bf16_ref[...] = val.astype(BF16)

    emit(proj(k0, k0 + LANES), kc_ref, kc16_ref)
    emit(proj(k0 + LANES, k0 + 2 * LANES), vc_ref, vc16_ref)
    emit(_rope(_head_rms(proj(k0 + 2 * LANES, k0 + 3 * LANES), gks_ref[...]), cos, sup, sdn), ks_ref, ks16_ref)
    emit(proj(k0 + 3 * LANES, k0 + 4 * LANES), vs_ref, vs16_ref)
    emit(_rope(_head_rms(proj(k0 + 4 * LANES, k0 + 5 * LANES), gkw_ref[...]), cos, sup, sdn), kw_ref, kw16_ref)
    emit(proj(k0 + 5 * LANES, k0 + 6 * LANES), vw_ref, vw16_ref)
    gate_ref[...] = proj(k0 + 6 * LANES, k0 + 7 * LANES)


def _in_proj(x, wts, tabs, wmix, bmix, *, tm, seg_rows):
    n = x.shape[0]
    tiles_per_seg = seg_rows // tm
    n_seg = n // seg_rows
    row = lambda w: pl.BlockSpec((tm, w), lambda i: (i, 0))
    full = lambda a: pl.BlockSpec(a.shape, lambda i: (0,) * a.ndim)
    cos, sup, sdn = tabs
    ins = [x, wts['g1'], wts['w_a'], wts['ln_g'], wts['ln_b'], wts['g_q'], wts['g_ks'], wts['g_kw'],
           cos, sup, sdn, wmix, bmix]
    tab = pl.BlockSpec((tm, LANES), lambda i: (i % tiles_per_seg, 0))
    in_specs = [row(D_MODEL)] + [full(a) for a in ins[1:8]] + [tab] * 3 + [full(wmix), full(bmix)]
    f32o = lambda w: jax.ShapeDtypeStruct((n, w), F32)
    b16o = lambda w: jax.ShapeDtypeStruct((n, w), BF16)
    out_shape = ([b16o(A_WIDTH), jax.ShapeDtypeStruct((n_seg, CHUNK, A_WIDTH), F32), b16o(B_WIDTH), b16o(B_WIDTH)]
                 + [jax.ShapeDtypeStruct((n_seg, LANES, seg_rows), F32)] * 6 + [b16o(LANES)] * 6 + [f32o(LANES)])
    state = pl.BlockSpec((1, LANES, tm), lambda i: (i // tiles_per_seg, 0, i % tiles_per_seg))
    out_specs = ([row(A_WIDTH), pl.BlockSpec((1, CHUNK, A_WIDTH), lambda i: (i // tiles_per_seg, 0, 0)),
                  row(B_WIDTH), row(B_WIDTH)] + [state] * 6 + [row(LANES)] * 7)
    return pl.pallas_call(
        functools.partial(_in_proj_body, tm=tm, tiles_per_seg=tiles_per_seg),
        grid=(n // tm,), in_specs=in_specs, out_specs=out_specs, out_shape=out_shape,
        compiler_params=pltpu.CompilerParams(dimension_semantics=("arbitrary",), vmem_limit_bytes=VMEM_LIMIT),
        name="in_proj")(*ins)


COMPRESS_ROWS = 256


def _compress_math(load_rows, m, w1ab_ref, pe_ref, w1d_ref, w2d_ref, g_ref, norm):
    step = min(COMPRESS_ROWS, m)
    f = jnp.concatenate(
        [jnp.dot(load_rows(r, r + step).astype(BF16), w1ab_ref[...], preferred_element_type=F32)
         for r in range(0, m, step)], axis=0)
    bias = jnp.dot(pe_ref[...], w1d_ref[...], preferred_element_type=F32)[0:1]
    second_next = pltpu.roll(f[:, LANES:], m - 1, 0)
    act = jax.nn.gelu(f[:, :LANES] + second_next + bias)
    out = jnp.dot(act.astype(BF16), w2d_ref[...], preferred_element_type=F32)
    if norm:
        out = _head_rms(out, g_ref[...])
    return out.astype(BF16)


def _compress_body(ch_ref, w1ab_ref, pe_ref, w1d_ref, w2d_ref, g_ref, out_ref, *, norm):
    out_ref[0] = _compress_math(lambda r0, r1: ch_ref[0, r0:r1], ch_ref.shape[1],
                                w1ab_ref, pe_ref, w1d_ref, w2d_ref, g_ref, norm)


def _compress(ch, cw, *, norm):
    nb, m, _ = ch.shape
    full = lambda a: pl.BlockSpec(a.shape, lambda b: (0,) * a.ndim)
    ws = [cw['w1ab'], cw['pe'], cw['w1d'], cw['w2d'], cw['g']]
    return pl.pallas_call(
        functools.partial(_compress_body, norm=norm),
        grid=(nb,),
        in_specs=[pl.BlockSpec((1, m, CHUNK_LANES), lambda b: (b, 0, 0))] + [full(a) for a in ws],
        out_specs=pl.BlockSpec((1, m, LANES), lambda b: (b, 0, 0)),
        out_shape=jax.ShapeDtypeStruct((nb, m, LANES), BF16),
        compiler_params=pltpu.CompilerParams(dimension_semantics=("arbitrary",), vmem_limit_bytes=VMEM_LIMIT),
        name="compress_k" if norm else "compress_v")(ch, *ws)


def _page_copy(pool_ref, page, buf_ref, slot, i, sem_ref):
    return pltpu.make_async_copy(pool_ref.at[page], buf_ref.at[slot, i], sem_ref.at[slot])


def _paged_compress_body(pt_ref, pool_ref, perm_ref, w1ab_ref, pe_ref, w1d_ref, w2d_ref, g_ref, out_ref,
                         buf_ref, sem_ref, ch_ref, *, n_pages, norm):
    b = pl.program_id(0)

    def fetch(bb, slot):
        def one(i, carry):
            _page_copy(pool_ref, pt_ref[bb * n_pages + i], buf_ref, slot, i, sem_ref).start()
            return carry
        lax.fori_loop(0, n_pages, one, 0, unroll=8)

    @pl.when(b == 0)
    def _():
        fetch(0, 0)

    @pl.when(b + 1 < pl.num_programs(0))
    def _():
        fetch(b + 1, (b + 1) % 2)

    slot = b % 2

    pltpu.make_async_copy(pool_ref.at[pl.ds(0, n_pages)], buf_ref.at[slot], sem_ref.at[slot]).wait()

    chunks = PAGE_SIZE // CMP_STRIDE

    def regroup(q, carry):
        xt = jnp.concatenate([buf_ref[slot, 2 * q], buf_ref[slot, 2 * q + 1]], axis=1).astype(BF16)
        y = jnp.dot(xt, perm_ref[...], preferred_element_type=F32)
        t0, t1 = y[:, :PAGE_SIZE].T, y[:, PAGE_SIZE:].T
        rows = pl.ds(pl.multiple_of(q * 2 * chunks, 2 * chunks), 2 * chunks)
        for r in range(CMP_STRIDE):
            piece = jnp.concatenate([t0[r * chunks:(r + 1) * chunks], t1[r * chunks:(r + 1) * chunks]], axis=0)
            ch_ref[rows, r * KV_WIDTH:(r + 1) * KV_WIDTH] = piece.astype(BF16)
        return carry
    lax.fori_loop(0, n_pages // 2, regroup, 0, unroll=16)
    m = n_pages * chunks
    out_ref[0] = _compress_math(lambda r0, r1: ch_ref[r0:r1], m, w1ab_ref, pe_ref, w1d_ref, w2d_ref, g_ref, norm)


def _paged_compress(pt_flat, pool, cw, *, nb, n_pages, norm):
    m = n_pages * (PAGE_SIZE // CMP_STRIDE)
    full = lambda a: pl.BlockSpec(a.shape, lambda b, pt: (0,) * a.ndim)
    lane = jnp.arange(2 * PAGE_SIZE)
    page, chunk, offset = lane // PAGE_SIZE, (lane % PAGE_SIZE) // CMP_STRIDE, lane % CMP_STRIDE
    dest = page * PAGE_SIZE + offset * (PAGE_SIZE // CMP_STRIDE) + chunk
    perm = (dest[:, None] == lane[None, :]).astype(BF16)
    ws = [perm, cw['w1ab'], cw['pe'], cw['w1d'], cw['w2d'], cw['g']]
    grid_spec = pltpu.PrefetchScalarGridSpec(
        num_scalar_prefetch=1, grid=(nb,),
        in_specs=[pl.BlockSpec(memory_space=pl.ANY)] + [full(a) for a in ws],
        out_specs=pl.BlockSpec((1, m, LANES), lambda b, pt: (b, 0, 0)),
        scratch_shapes=[pltpu.VMEM((2, n_pages, KV_WIDTH, PAGE_SIZE), F32), pltpu.SemaphoreType.DMA((2,)),
                        pltpu.VMEM((m, CHUNK_LANES), BF16)])
    return pl.pallas_call(
        functools.partial(_paged_compress_body, n_pages=n_pages, norm=norm),
        grid_spec=grid_spec, out_shape=jax.ShapeDtypeStruct((nb, m, LANES), BF16),
        compiler_params=pltpu.CompilerParams(dimension_semantics=("arbitrary",), vmem_limit_bytes=VMEM_LIMIT),
        name="paged_compress_k" if norm else "paged_compress_v")(pt_flat, pool, *ws)


def _split3(x):
    hi = x.astype(BF16)
    r1 = x - hi.astype(F32)
    mid = r1.astype(BF16)
    lo = (r1 - mid.astype(F32)).astype(BF16)
    return hi, mid, lo


def _dot_nt(a, b):
    return lax.dot_general(a, b, (((1,), (1,)), ((), ())), preferred_element_type=F32)


def _topk_mask_t(rank, n_pick):
    n_blk = rank.shape[0]
    jidx = lax.broadcasted_iota(jnp.int32, rank.shape, 0).astype(F32)

    def body(_, carry):
        r, sel = carry
        m = jnp.max(r, axis=0, keepdims=True)
        first = jnp.min(jnp.where(r == m, jidx, float(n_blk)), axis=0, keepdims=True)
        pick = jidx == first
        return jnp.where(pick, -jnp.inf, r), jnp.where(pick, 1.0, sel)

    _, sel = lax.fori_loop(0, n_pick, body, (rank, jnp.zeros_like(rank)), unroll=True)
    return sel


def _masked_softmax(s, mask, row_valid):
    s = jnp.where(mask, s, NEG)
    e = jnp.exp2(s - jnp.max(s, axis=-1, keepdims=True))
    return e * jnp.where(row_valid, 1.0 / jnp.sum(e, axis=-1, keepdims=True), 0.0)


def _gate_col(gt, head, branch):
    c = head * 3 + branch
    return gt[:, c:c + 1]


def _prompt_attn_body(qn_ref, qr_ref, gate_ref, kc_ref, vc_ref, kst_ref, vs_ref, kw_ref, vw_ref, ebt_ref, at_ref,
                      o_ref, m_ref, l_ref, acc_ref, *, seq, n_cmp):
    i = pl.program_id(1)
    s0 = i * Q_TILE
    nc_pad = kc_ref.shape[1]
    n_sel = seq // SEL_BLOCK
    lo = _low_half((Q_TILE, LANES))
    pos_col = s0 + lax.broadcasted_iota(jnp.int32, (Q_TILE, 1), 0)

    def head_rows(q_ref):
        parts = []
        for hd in range(N_HEADS):
            keep = lo if hd < GROUP else jnp.logical_not(lo)
            j = hd % GROUP
            parts.append(jnp.where(keep, q_ref[0, :, j * LANES:(j + 1) * LANES], 0))
        return jnp.concatenate(parts, axis=0)

    cidx = lax.broadcasted_iota(jnp.int32, (Q_TILE, nc_pad), 1)
    m_c = (cidx * CMP_STRIDE + (CMP_BLOCK - 1) <= pos_col) & (cidx < n_cmp)
    jrow = lax.broadcasted_iota(jnp.int32, (n_sel, Q_TILE), 0)
    pos_row = s0 + lax.broadcasted_iota(jnp.int32, (n_sel, Q_TILE), 1)
    cur = pos_row >> 6
    forced = (jrow == 0) | (jrow == cur) | (jrow == cur - 1)
    valid_b = jrow * SEL_BLOCK <= pos_row
    q_sel = head_rows(qr_ref)
    n_win = WINDOW + Q_TILE
    w0 = pl.multiple_of(jnp.maximum(s0 - WINDOW, 0), Q_TILE)
    dpos = pos_col - (w0 + lax.broadcasted_iota(jnp.int32, (Q_TILE, n_win), 1))
    m_w = (dpos >= 0) & (dpos < WINDOW)
    s_c = _dot_nt(head_rows(qn_ref), kc_ref[0]).reshape(N_HEADS, Q_TILE, nc_pad)
    s_w = _dot_nt(q_sel, kw_ref[0, pl.ds(w0, n_win), :]).reshape(N_HEADS, Q_TILE, n_win)
    p = _masked_softmax(s_c, m_c[None], ((pos_col >= CMP_BLOCK - 1) & (n_cmp > 0))[None])
    o_cmp = jnp.dot(p.reshape(N_HEADS * Q_TILE, nc_pad).astype(BF16), vc_ref[0], preferred_element_type=F32)
    imps = [jnp.sum(p[kv * GROUP:(kv + 1) * GROUP], axis=0) for kv in range(N_KV)]
    p_w = _masked_softmax(s_w, m_w[None], True)
    o_win = jnp.dot(p_w.reshape(N_HEADS * Q_TILE, n_win).astype(BF16), vw_ref[0, pl.ds(w0, n_win), :],
                    preferred_element_type=F32)

    notsel = []
    for kv in range(N_KV):
        imp = imps[kv]
        blk_t = sum(_dot_nt(at_ref[...], part) for part in _split3(imp))
        rank = jnp.where(forced, FORCED_BONUS, jnp.where(valid_b, blk_t, -FORCED_BONUS))
        sel_t = _topk_mask_t(rank, min(N_SELECT, n_sel))
        notsel += [(1.0 - sel_t.T).astype(BF16)] * GROUP

    m_ref[...] = jnp.full(m_ref.shape, NEG, F32)
    l_ref[...] = jnp.zeros(l_ref.shape, F32)
    acc_ref[...] = jnp.zeros(acc_ref.shape, F32)
    q_aug = jnp.concatenate([q_sel, jnp.concatenate(notsel, axis=0)], axis=1)

    def tile_step(tix, lane0, width, diagonal):
        lanes = slice(lane0, lane0 + width)
        key0 = pl.multiple_of(tix * KEY_TILE + lane0, width)
        k_aug = jnp.concatenate([kst_ref[0, tix, :, lanes], ebt_ref[tix, :, lanes]], axis=0)
        vt = vs_ref[0, pl.ds(key0, width), :]
        if diagonal:
            key = key0 + lax.broadcasted_iota(jnp.int32, (Q_TILE, width), 1)
            future = jnp.where(key <= pos_col, 0.0, NEG)
        part = N_HEADS * Q_TILE // FLASH_SPLITS
        scores = [jnp.dot(q_aug[h * part:(h + 1) * part], k_aug, preferred_element_type=F32)
                  for h in range(FLASH_SPLITS)]
        for h in range(FLASH_SPLITS):
            rows = slice(h * part, (h + 1) * part)
            s = scores[h]
            if diagonal:
                s = (s.reshape(part // Q_TILE, Q_TILE, width) + future[None]).reshape(part, width)
            m_old = m_ref[rows]
            m_new = jnp.maximum(m_old, jnp.max(s, axis=-1, keepdims=True))
            alpha = jnp.exp2(m_old - m_new)
            p_t = jnp.exp2(s - m_new)
            l_ref[rows] = alpha * l_ref[rows] + jnp.sum(p_t, axis=-1, keepdims=True)
            acc_ref[rows] = alpha * acc_ref[rows] + jnp.dot(p_t.astype(BF16), vt, preferred_element_type=F32)
            m_ref[rows] = m_new

    def tile_body(tix, carry):
        tile_step(tix, 0, KEY_TILE, False)
        return carry

    q_per_key = KEY_TILE // Q_TILE
    last_tile = lax.shift_right_logical(i, q_per_key.bit_length() - 1)
    lax.fori_loop(0, last_tile, tile_body, 0)
    half = KEY_TILE // 2
    in_second_half = (i & (q_per_key - 1)) >= q_per_key // 2

    @pl.when(in_second_half)
    def _():
        tile_step(last_tile, 0, half, False)
        tile_step(last_tile, half, half, True)

    @pl.when(jnp.logical_not(in_second_half))
    def _():
        tile_step(last_tile, 0, half, True)

    o_sel = acc_ref[...] / l_ref[...]
    gt = jax.nn.sigmoid(gate_ref[0])
    o_heads = []
    for hd in range(N_HEADS):
        rows = slice(hd * Q_TILE, (hd + 1) * Q_TILE)
        o_heads.append(_gate_col(gt, hd, 0) * o_cmp[rows] + _gate_col(gt, hd, 1) * o_sel[rows]
                       + _gate_col(gt, hd, 2) * o_win[rows])
    for j in range(GROUP):
        o_ref[0, :, j * LANES:(j + 1) * LANES] = jnp.where(lo, o_heads[j], o_heads[GROUP + j]).astype(BF16)


def _prompt_attn(qn, qr, gate, kc, vc, kst, vs16, kw16, vw16, ebt, a_t, *, n_cmp):
    nb, seq, _ = qn.shape
    qspec = lambda w: pl.BlockSpec((1, Q_TILE, w), lambda b, i: (b, i, 0))
    seqspec = lambda a: pl.BlockSpec((1,) + a.shape[1:], lambda b, i: (b,) + (0,) * (a.ndim - 1))
    full = lambda a: pl.BlockSpec(a.shape, lambda b, i: (0,) * a.ndim)
    rows = N_HEADS * Q_TILE
    return pl.pallas_call(
        functools.partial(_prompt_attn_body, seq=seq, n_cmp=n_cmp),
        grid=(nb, seq // Q_TILE),
        in_specs=[qspec(B_WIDTH), qspec(B_WIDTH), qspec(LANES), seqspec(kc), seqspec(vc),
                  seqspec(kst), seqspec(vs16), seqspec(kw16), seqspec(vw16), full(ebt), full(a_t)],
        out_specs=qspec(B_WIDTH),
        out_shape=jax.ShapeDtypeStruct((nb, seq, B_WIDTH), BF16),
        scratch_shapes=[pltpu.VMEM((rows, 1), F32), pltpu.VMEM((rows, 1), F32), pltpu.VMEM((rows, LANES), F32)],
        compiler_params=pltpu.CompilerParams(dimension_semantics=("arbitrary", "arbitrary"),
                                             vmem_limit_bytes=VMEM_LIMIT),
        name="prompt_attn")(qn, qr, gate, kc, vc, kst, vs16, kw16, vw16, ebt, a_t)


TOK_PAD = 8
ROWS = N_HEADS * TOK_PAD
ROWS_PAD = 128


def _page_copy_lanes(pool_ref, page, buf_ref, slot, i, sem_ref):
    return pltpu.make_async_copy(pool_ref.at[page], buf_ref.at[slot, :, pl.ds(i * PAGE_SIZE, PAGE_SIZE)],
                                 sem_ref.at[slot])


def _sample_attn_body(pt_ref, qn_ref, qr_ref, gate_ref, kc_ref, vc_ref, kwc_ref, vwc_ref,
                      ksn_ref, vsn_ref, kwn_ref, vwn_ref, at_ref, gsum_ref, ebt_ref, kpool_ref, vpool_ref,
                      o_ref, kbuf_ref, vbuf_ref, ksem_ref, vsem_ref, notsel_ref, m_ref, l_ref, acc_ref, ocw_ref,
                      q_ref, *, n_chunks, pages_per_chunk, past, n_new, n_cmp):
    b = pl.program_id(0)
    c = pl.program_id(1)
    step = b * n_chunks + c
    n_steps = pl.num_programs(0) * n_chunks
    n_past_blk = past // SEL_BLOCK
    blk_per_chunk = n_past_blk // n_chunks

    def fetch(st, slot):
        def one(i, carry):
            page = pt_ref[st * pages_per_chunk + i]
            _page_copy_lanes(kpool_ref, page, kbuf_ref, slot, i, ksem_ref).start()
            _page_copy_lanes(vpool_ref, page, vbuf_ref, slot, i, vsem_ref).start()
            return carry
        lax.fori_loop(0, pages_per_chunk, one, 0, unroll=8)

    @pl.when(step == 0)
    def _():
        fetch(0, 0)

    @pl.when(step + 1 < n_steps)
    def _():
        fetch(step + 1, (step + 1) % 2)

    lo = _low_half((TOK_PAD, LANES))
    tok_col = lax.broadcasted_iota(jnp.int32, (ROWS, 1), 0) % TOK_PAD
    pos_col = past + tok_col

    def rows_q(ref):
        parts = []
        for hd in range(N_HEADS):
            keep = lo if hd < GROUP else jnp.logical_not(lo)
            j = hd % GROUP
            parts.append(jnp.where(keep, ref[0, :, j * LANES:(j + 1) * LANES], 0))
        return jnp.concatenate(parts, axis=0)

    def gate_rows(gt, branch):
        return jnp.concatenate([_gate_col(gt, hd, branch) for hd in range(N_HEADS)], axis=0)

    @pl.when(c == 0)
    def _():
        gt = jax.nn.sigmoid(gate_ref[0])
        nc_pad = kc_ref.shape[1]
        cidx = lax.broadcasted_iota(jnp.int32, (ROWS, nc_pad), 1)
        m_c = (cidx * CMP_STRIDE + (CMP_BLOCK - 1) <= pos_col) & (cidx < n_cmp)
        p = _masked_softmax(_dot_nt(rows_q(qn_ref), kc_ref[0]), m_c, (pos_col >= CMP_BLOCK - 1) & (n_cmp > 0))
        o_c = jnp.dot(p.astype(BF16), vc_ref[0], preferred_element_type=F32)
        imp = sum(jnp.dot(gsum_ref[...], part, preferred_element_type=F32) for part in _split3(p))
        blk_t = sum(_dot_nt(at_ref[...], part) for part in _split3(imp))
        jrow = lax.broadcasted_iota(jnp.int32, blk_t.shape, 0)
        rank = jnp.where((jrow == 0) | (jrow == n_past_blk - 1), FORCED_BONUS, blk_t)
        n_top = min(N_SELECT, n_past_blk + 1)
        sel_t = _topk_mask_t(rank, n_top - 1)
        for cc in range(n_chunks):
            blk = sel_t[cc * blk_per_chunk:(cc + 1) * blk_per_chunk]
            notsel_ref[cc] = (1.0 - blk.T[:ROWS]).astype(BF16)
        q_r = rows_q(qr_ref)
        q_ref[...] = q_r
        n_buf = kwc_ref.shape[2]
        kidx = lax.broadcasted_iota(jnp.int32, (ROWS, n_buf), 1)
        dpos = pos_col - (past - n_buf + kidx)
        m_old = (dpos >= 0) & (dpos < WINDOW)
        knew = lax.broadcasted_iota(jnp.int32, (ROWS, TOK_PAD), 1)
        m_new = (knew <= tok_col) & (knew < n_new)
        s_old = jnp.where(m_old, jnp.dot(q_r, kwc_ref[0].astype(BF16), preferred_element_type=F32), NEG)
        s_new = jnp.where(m_new, _dot_nt(q_r, kwn_ref[0]), NEG)
        mx = jnp.maximum(jnp.max(s_old, axis=-1, keepdims=True), jnp.max(s_new, axis=-1, keepdims=True))
        e_old = jnp.exp2(s_old - mx)
        e_new = jnp.exp2(s_new - mx)
        den = jnp.sum(e_old, axis=-1, keepdims=True) + jnp.sum(e_new, axis=-1, keepdims=True)
        p_old = e_old / den * m_old.astype(F32)
        p_new = e_new / den * m_new.astype(F32)
        o_w = (_dot_nt(p_old.astype(BF16), vwc_ref[0].astype(BF16))
               + jnp.dot(p_new.astype(BF16), vwn_ref[0], preferred_element_type=F32))
        ocw_ref[...] = gate_rows(gt, 0) * o_c + gate_rows(gt, 2) * o_w
        s = jnp.where(m_new, _dot_nt(q_r, ksn_ref[0]), NEG)
        m0 = jnp.max(s, axis=-1, keepdims=True)
        e = jnp.exp2(s - m0) * m_new.astype(F32)
        m_ref[...] = m0
        l_ref[...] = jnp.sum(e, axis=-1, keepdims=True)
        acc_ref[...] = jnp.dot(e.astype(BF16), vsn_ref[0], preferred_element_type=F32)

    slot = step % 2

    pltpu.make_async_copy(kbuf_ref.at[slot], kbuf_ref.at[slot], ksem_ref.at[slot]).wait()
    pltpu.make_async_copy(vbuf_ref.at[slot], vbuf_ref.at[slot], vsem_ref.at[slot]).wait()

    q_aug = jnp.concatenate([q_ref[...], notsel_ref[c]], axis=1)
    k_aug = jnp.concatenate([kbuf_ref[slot].astype(BF16), ebt_ref[...]], axis=0)
    s = jnp.dot(q_aug, k_aug, preferred_element_type=F32)
    m_old = m_ref[...]
    m_new = jnp.maximum(m_old, jnp.max(s, axis=-1, keepdims=True))
    alpha = jnp.exp2(m_old - m_new)
    p = jnp.exp2(s - m_new)
    l_ref[...] = alpha * l_ref[...] + jnp.sum(p, axis=-1, keepdims=True)
    acc_ref[...] = alpha * acc_ref[...] + _dot_nt(p.astype(BF16), vbuf_ref[slot].astype(BF16))
    m_ref[...] = m_new

    @pl.when(c == n_chunks - 1)
    def _():
        gt = jax.nn.sigmoid(gate_ref[0])
        o = ocw_ref[...] + gate_rows(gt, 1) * (acc_ref[...] / l_ref[...])
        for j in range(GROUP):
            top = o[j * TOK_PAD:(j + 1) * TOK_PAD]
            bot = o[(GROUP + j) * TOK_PAD:(GROUP + j + 1) * TOK_PAD]
            o_ref[0, :, j * LANES:(j + 1) * LANES] = jnp.where(lo, top, bot).astype(BF16)


def _sample_attn(pt_flat, qn, qr, gate, kc, vc, kwc, vwc, ksn, vsn, kwn, vwn, a_t, gsum, ebt, kpool, vpool,
                 *, n_chunks, pages_per_chunk, past, n_new, n_cmp):
    nb = qn.shape[0]
    chunk_keys = pages_per_chunk * PAGE_SIZE
    blk_per_chunk = chunk_keys // SEL_BLOCK
    bspec = lambda a: pl.BlockSpec((1,) + a.shape[1:], lambda b, c, pt: (b, 0, 0))
    full = lambda a: pl.BlockSpec(a.shape, lambda b, c, pt: (0,) * a.ndim)
    anyspec = pl.BlockSpec(memory_space=pl.ANY)
    blocked = [qn, qr, gate, kc, vc, kwc, vwc, ksn, vsn, kwn, vwn]
    grid_spec = pltpu.PrefetchScalarGridSpec(
        num_scalar_prefetch=1, grid=(nb, n_chunks),
        in_specs=[bspec(a) for a in blocked] + [full(a_t), full(gsum), full(ebt), anyspec, anyspec],
        out_specs=pl.BlockSpec((1, TOK_PAD, B_WIDTH), lambda b, c, pt: (b, 0, 0)),
        scratch_shapes=[pltpu.VMEM((2, KV_WIDTH, chunk_keys), F32), pltpu.VMEM((2, KV_WIDTH, chunk_keys), F32),
                        pltpu.SemaphoreType.DMA((2,)), pltpu.SemaphoreType.DMA((2,)),
                        pltpu.VMEM((n_chunks, ROWS, blk_per_chunk), BF16),
                        pltpu.VMEM((ROWS, 1), F32), pltpu.VMEM((ROWS, 1), F32), pltpu.VMEM((ROWS, LANES), F32),
                        pltpu.VMEM((ROWS, LANES), F32), pltpu.VMEM((ROWS, LANES), BF16)])
    return pl.pallas_call(
        functools.partial(_sample_attn_body, n_chunks=n_chunks, pages_per_chunk=pages_per_chunk, past=past,
                          n_new=n_new, n_cmp=n_cmp),
        grid_spec=grid_spec, out_shape=jax.ShapeDtypeStruct((nb, TOK_PAD, B_WIDTH), BF16),
        compiler_params=pltpu.CompilerParams(dimension_semantics=("arbitrary", "arbitrary"),
                                             vmem_limit_bytes=VMEM_LIMIT),
        name="sample_attn")(pt_flat, *blocked, a_t, gsum, ebt, kpool, vpool)


FF_TILE = 1024


def _merge_ffn_body(x_ref, oa_ref, ob_ref, g1_ref, wmg_ref, wba_ref, wbb_ref, wout_ref, g2_ref, wup_ref, wdn_ref,
                    y_ref):
    x = x_ref[...]
    h = _rms_rows(x, g1_ref[...]).astype(BF16)
    y_a = jnp.dot(oa_ref[...], wba_ref[...], preferred_element_type=F32)
    y_b = jnp.dot(ob_ref[...], wbb_ref[...], preferred_element_type=F32)
    g_a = jax.nn.sigmoid(jnp.dot(h, wmg_ref[:, :D_MODEL], preferred_element_type=F32))
    g_b = jax.nn.sigmoid(jnp.dot(h, wmg_ref[:, D_MODEL:], preferred_element_type=F32))
    x1 = x + jnp.dot((g_a * y_a + g_b * y_b).astype(BF16), wout_ref[...], preferred_element_type=F32)
    h2 = _rms_rows(x1, g2_ref[...]).astype(BF16)
    y = x1
    for f in range(D_FF // FF_TILE):
        up = jnp.dot(h2, wup_ref[:, f * FF_TILE:(f + 1) * FF_TILE], preferred_element_type=F32)
        act = jnp.square(jnp.maximum(up, 0.0)).astype(BF16)
        y = y + jnp.dot(act, wdn_ref[f * FF_TILE:(f + 1) * FF_TILE, :], preferred_element_type=F32)
    y_ref[...] = y


def _merge_ffn(x, o_a, o_b, wts, *, tm):
    n = x.shape[0]
    row = lambda w: pl.BlockSpec((tm, w), lambda i: (i, 0))
    const = lambda a: pl.BlockSpec(a.shape, lambda i: (0,) * a.ndim, pipeline_mode=pl.Buffered(1))
    ws = [wts['g1'], wts['w_mg'], wts['w_br_a'], wts['w_br_b'], wts['w_out'], wts['g2'], wts['w_up'], wts['w_down']]
    return pl.pallas_call(
        _merge_ffn_body, grid=(n // tm,),
        in_specs=[row(D_MODEL), row(A_WIDTH), row(B_WIDTH)] + [const(a) for a in ws],
        out_specs=row(D_MODEL), out_shape=jax.ShapeDtypeStruct((n, D_MODEL), F32),
        compiler_params=pltpu.CompilerParams(dimension_semantics=("arbitrary",), vmem_limit_bytes=VMEM_LIMIT),
        name="merge_ffn")(x, o_a, o_b, *ws)


def _rope_tables(pos):
    half = ROT_DIM // 2
    lane = jnp.arange(LANES) % HEAD_DIM
    inv = ROPE_THETA ** (-(2 * (lane % half)).astype(F32) / ROT_DIM)
    ang = pos.astype(F32)[:, None] * inv[None, :]
    cos, sin = jnp.cos(ang), jnp.sin(ang)
    cos_t = jnp.where(lane < ROT_DIM, cos, 1.0)
    sin_up = jnp.where((lane >= half) & (lane < ROT_DIM), sin, 0.0)
    sin_dn = jnp.where(lane < half, -sin, 0.0)
    return cos_t, sin_up, sin_dn


def _two(g):
    return jnp.concatenate([g, g])[None, :].astype(F32)


def _layer_weights(l, g_norm1, w_in, ln_v_g, ln_v_b, g_q, g_ks, g_kw, w_branch, w_out, g_norm2, w_up, w_down):
    w = w_in[l]
    q0, k0 = 2 * A_WIDTH, 2 * A_WIDTH + B_WIDTH
    g0 = k0 + 6 * KV_WIDTH
    perm = jnp.array(HEAD_PERM)
    q_cols = w[:, q0:k0].reshape(D_MODEL, N_HEADS, HEAD_DIM)[:, perm].reshape(D_MODEL, B_WIDTH)
    gate_cols = jnp.pad(w[:, g0:g0 + 3 * N_HEADS], ((0, 0), (0, LANES - 3 * N_HEADS)))
    w_a = jnp.concatenate([w[:, :q0], q_cols, w[:, k0:g0], gate_cols], axis=1).astype(BF16)
    wb = w_branch[l]
    w_br_b = wb[A_WIDTH:].reshape(N_HEADS, HEAD_DIM, D_MODEL)[perm].reshape(B_WIDTH, D_MODEL)
    return dict(
        g1=g_norm1[l][None, :], w_a=w_a, ln_g=ln_v_g[l][None, :], ln_b=ln_v_b[l][None, :],
        g_q=_two(g_q[l]), g_ks=_two(g_ks[l]), g_kw=_two(g_kw[l]),
        w_mg=w[:, g0 + 3 * N_HEADS:].astype(BF16), w_br_a=wb[:A_WIDTH].astype(BF16), w_br_b=w_br_b.astype(BF16),
        w_out=w_out[l].astype(BF16), g2=g_norm2[l][None, :], w_up=w_up[l].astype(BF16),
        w_down=w_down[l].astype(BF16))


def _compress_weights(w1, w2, pe, g):
    half = CMP_STRIDE * HEAD_DIM
    eye = jnp.eye(N_KV, dtype=F32)

    def spread(wh):
        return jnp.einsum('rde,kl->rkdle', wh.reshape(CMP_STRIDE, HEAD_DIM, HEAD_DIM), eye).reshape(
            CHUNK_LANES, KV_WIDTH)

    w1ab = jnp.concatenate([spread(w1[:half]), spread(w1[half:])], axis=1).astype(BF16)
    pe_rows = jnp.zeros((8, CMP_BLOCK * HEAD_DIM), F32).at[0].set(pe.reshape(-1)).astype(BF16)
    w1d = jnp.concatenate([w1, w1], axis=1).astype(BF16)
    w2d = jnp.einsum('de,kl->kdle', w2, eye).reshape(KV_WIDTH, KV_WIDTH).astype(BF16)
    return dict(w1ab=w1ab, pe=pe_rows, w1d=w1d, w2d=w2d, g=_two(g))


def _block_score_matrix(n_blk, n_chunk):
    j = jnp.arange(n_blk)[:, None]
    c = jnp.arange(n_chunk)[None, :]
    per = SEL_BLOCK // CMP_STRIDE
    a = ((c >= per * j) & (c <= per * j + per - 1)).astype(F32) + ((c >= per * j - 1) & (c <= per * j + per - 2))
    return a.astype(BF16)


def kernel(x_prompt, x_sample, cache_k_cmp, cache_v_cmp, cache_k_sel, cache_v_sel, cache_k_win, cache_v_win,
           page_table, g_norm1, w_in, ln_v_g, ln_v_b, w_s, b_s, g_q, g_kc, g_ks, g_kw, w_ck1, w_ck2, pe_k,
           w_cv1, w_cv2, pe_v, w_branch, w_out, g_norm2, w_up, w_down):
    nb, seq, _ = x_prompt.shape
    db, dseq, _ = x_sample.shape
    depth = w_in.shape[0]
    n_pages = page_table.shape[1]
    past = n_pages * PAGE_SIZE
    assert depth == 1 and seq % KEY_TILE == 0 and seq >= WINDOW + Q_TILE and dseq <= TOK_PAD
    assert past % (SEL_BLOCK * LANES) == 0 and (past + dseq) // CMP_STRIDE == past // CMP_STRIDE
    l = 0
    wts = _layer_weights(l, g_norm1, w_in, ln_v_g, ln_v_b, g_q, g_ks, g_kw, w_branch, w_out, g_norm2, w_up, w_down)
    cw_k = _compress_weights(w_ck1[l], w_ck2[l], pe_k[l], g_kc[l])
    cw_v = _compress_weights(w_cv1[l], w_cv2[l], pe_v[l], g_kc[l])
    causal = jnp.tril(jnp.ones((CHUNK, CHUNK), dtype=bool))
    ws_tril = jnp.where(causal[None], w_s[l], 0.0)

    n_p = nb * seq
    xp = x_prompt.reshape(n_p, D_MODEL)
    tabs_p = _rope_tables(jnp.arange(seq, dtype=jnp.int32))
    bmix_p = jnp.repeat(b_s[l].T, A_WIDTH // A_GROUPS, axis=1)
    (oa_p, vlast_p, qn_p, qr_p, kc_p, vc_p, ks_p, vs_p, kw_p, vw_p, kc16, vc16, ks16, vs16, kw16, vw16,
     gate_p) = _in_proj(
        xp, wts, tabs_p, ws_tril.astype(BF16), bmix_p, tm=512, seg_rows=seq)
    n_chunk_p = seq // CMP_STRIDE
    kcmp_p = _compress(kc16.reshape(nb, n_chunk_p, CHUNK_LANES), cw_k, norm=True)
    vcmp_p = _compress(vc16.reshape(nb, n_chunk_p, CHUNK_LANES), cw_v, norm=False)
    n_sel_p = seq // SEL_BLOCK
    n_tiles = seq // KEY_TILE
    kblk = (jnp.arange(seq) // SEL_BLOCK).reshape(n_tiles, 1, KEY_TILE)
    ebt = jnp.where(kblk == jnp.arange(n_sel_p)[None, :, None], MASK_BIAS, 0.0).astype(BF16)
    a_t_p = _block_score_matrix(n_sel_p, n_chunk_p)
    b3 = lambda a: a.reshape(nb, seq, a.shape[-1])
    kst = ks16.reshape(nb, n_tiles, KEY_TILE, KV_WIDTH).transpose(0, 1, 3, 2)
    ob_p = _prompt_attn(b3(qn_p), b3(qr_p), b3(gate_p), kcmp_p, vcmp_p, kst, b3(vs16), b3(kw16), b3(vw16),
                        ebt, a_t_p, n_cmp=n_chunk_p - 1)
    y_p = _merge_ffn(xp, oa_p, ob_p.reshape(n_p, B_WIDTH), wts, tm=512).reshape(nb, seq, D_MODEL)

    n_s = db * dseq
    xs = x_sample.reshape(n_s, D_MODEL)
    tabs_s = _rope_tables(jnp.tile(past + jnp.arange(dseq, dtype=jnp.int32), db))
    wmix_s = jnp.einsum('gpr,bc->gbpcr', ws_tril[:, :dseq, :dseq], jnp.eye(db, dtype=F32)).reshape(
        A_GROUPS, n_s, n_s)
    bmix_s = jnp.tile(bmix_p[:dseq], (db, 1))
    (oa_s, v_s, qn_s, qr_s, kc_s, vc_s, ks_s, vs_s, kw_s, vw_s, _, _, ks16s, vs16s, kw16s, vw16s,
     gate_s) = _in_proj(
        xs, wts, tabs_s, wmix_s.astype(BF16), bmix_s, tm=n_s, seg_rows=n_s)
    pt_flat = page_table.reshape(-1)
    keys_last = lambda a: a.transpose(0, 2, 3, 1).reshape(a.shape[0], KV_WIDTH, a.shape[1])
    kcmp_s = _paged_compress(pt_flat, keys_last(cache_k_cmp[l]), cw_k, nb=db, n_pages=n_pages, norm=True)
    vcmp_s = _paged_compress(pt_flat, keys_last(cache_v_cmp[l]), cw_v, nb=db, n_pages=n_pages, norm=False)
    n_chunk_s = past // CMP_STRIDE
    tokpad = lambda a: jnp.pad(a.reshape(db, dseq, a.shape[-1]), ((0, 0), (0, TOK_PAD - dseq), (0, 0)))
    hh = jnp.arange(ROWS_PAD) // TOK_PAD
    tt = jnp.arange(ROWS_PAD) % TOK_PAD
    gsum = ((hh[:, None] // GROUP == hh[None, :ROWS] // GROUP) & (tt[:, None] == tt[None, :ROWS])
            & (hh[:, None] < N_HEADS)).astype(BF16)
    n_chunks = 2
    chunk_keys = past // n_chunks
    ebt = jnp.where(jnp.arange(chunk_keys)[None, :] // SEL_BLOCK == jnp.arange(chunk_keys // SEL_BLOCK)[:, None],
                    MASK_BIAS, 0.0).astype(BF16)
    ob_s = _sample_attn(
        pt_flat, tokpad(qn_s), tokpad(qr_s), tokpad(gate_s), kcmp_s, vcmp_s,
        keys_last(cache_k_win[l]), keys_last(cache_v_win[l]),
        tokpad(ks16s), tokpad(vs16s), tokpad(kw16s), tokpad(vw16s),
        _block_score_matrix(past // SEL_BLOCK, n_chunk_s), gsum, ebt,
        keys_last(cache_k_sel[l]), keys_last(cache_v_sel[l]),
        n_chunks=n_chunks, pages_per_chunk=n_pages // n_chunks, past=past, n_new=dseq, n_cmp=n_chunk_s - 1)
    y_s = _merge_ffn(xs, oa_s, ob_s[:, :dseq].reshape(n_s, B_WIDTH), wts, tm=n_s).reshape(db, dseq, D_MODEL)

    w_p = min(WINDOW, seq)
    pk = lambda a: a.reshape(1, nb, N_KV, HEAD_DIM, seq).transpose(0, 1, 4, 2, 3)
    sk = lambda a: a[0].T.reshape(1, db, dseq, N_KV, HEAD_DIM)
    return (y_p, y_s, pk(kc_p), pk(vc_p), pk(ks_p), pk(vs_p), pk(kw_p)[:, :, seq - w_p:], pk(vw_p)[:, :, seq - w_p:],
            vlast_p[None], sk(kc_s), sk(vc_s), sk(ks_s), sk(vs_s), sk(kw_s), sk(vw_s),
            v_s.reshape(db, dseq, A_WIDTH)[None])
```

```python
import functools

import jax
import jax.numpy as jnp
from jax import lax
from jax.experimental import pallas as pl
from jax.experimental.pallas import tpu as pltpu

F32 = jnp.float32
BF16 = jnp.bfloat16

D_MODEL = 1024
A_WIDTH = 512
A_GROUPS = 8
CHUNK = 128
N_HEADS = 8
HEAD_DIM = 64
N_KV = 2
GROUP = N_HEADS // N_KV
B_WIDTH = N_HEADS * HEAD_DIM
KV_WIDTH = N_KV * HEAD_DIM
CMP_BLOCK = 32
CMP_STRIDE = 16
SEL_BLOCK = 64
N_SELECT = 16
WINDOW = 512
ROT_DIM = 16
ROPE_THETA = 500000.0
D_FF = 4096
PAGE_SIZE = 128
EPS = 1e-6
NEG = -1e30
FORCED_BONUS = 1e4
MASK_BIAS = -(2.0 ** 100)
SCALE = HEAD_DIM ** -0.5
Q_SCALE = SCALE * 1.4426950408889634

LANES = 128
Q_TILE = 128
KEY_TILE = 2048
FLASH_SPLITS = 4
CHUNK_LANES = CMP_STRIDE * KV_WIDTH
HEAD_PERM = (0, 4, 1, 5, 2, 6, 3, 7)
VMEM_LIMIT = 56 * 1024 * 1024


def _low_half(shape):
    return lax.broadcasted_iota(jnp.int32, shape, len(shape) - 1) < HEAD_DIM


def _head_rms(xs, g):
    lo = _low_half(xs.shape)
    sq = xs * xs
    sa = jnp.sum(jnp.where(lo, sq, 0.0), axis=-1, keepdims=True)
    sb = jnp.sum(jnp.where(lo, 0.0, sq), axis=-1, keepdims=True)
    ms = jnp.where(lo, sa, sb) * (1.0 / HEAD_DIM)
    return xs * lax.rsqrt(ms + EPS) * g


def _rope(xs, cos, sin_up, sin_dn):
    return xs * cos + pltpu.roll(xs, ROT_DIM // 2, 1) * sin_up + pltpu.roll(xs, LANES - ROT_DIM // 2, 1) * sin_dn


def _rms_rows(x, g):
    return x * lax.rsqrt(jnp.mean(x * x, axis=-1, keepdims=True) + EPS) * g


def _in_proj_body(x_ref, g1_ref, w_ref, lng_ref, lnb_ref, gq_ref, gks_ref, gkw_ref,
                  cos_ref, sup_ref, sdn_ref, wmix_ref, bmix_ref,
                  oa_ref, vlast_ref, qn_ref, qr_ref, kc_ref, vc_ref, ks_ref, vs_ref, kw_ref, vw_ref,
                  kc16_ref, vc16_ref, ks16_ref, vs16_ref, kw16_ref, vw16_ref, gate_ref, *, tm, tiles_per_seg):
    i = pl.program_id(0)
    h = _rms_rows(x_ref[...], g1_ref[...]).astype(BF16)

    def proj(lo, hi):
        return jnp.dot(h, w_ref[:, lo:hi], preferred_element_type=F32)

    u = jax.nn.gelu(proj(0, A_WIDTH))
    v = jax.nn.gelu(proj(A_WIDTH, 2 * A_WIDTH))
    vc = v - jnp.mean(v, axis=-1, keepdims=True)
    vn = vc * lax.rsqrt(jnp.mean(vc * vc, axis=-1, keepdims=True) + EPS) * lng_ref[...] + lnb_ref[...]

    @pl.when(i % tiles_per_seg == tiles_per_seg - 1)
    def _():
        vlast_ref[0] = vn[tm - CHUNK:tm]

    lo = _low_half((CHUNK, LANES))
    for c in range(tm // CHUNK):
        r0, r1 = c * CHUNK, (c + 1) * CHUNK
        for j in range(A_WIDTH // LANES):
            slab = vn[r0:r1, j * LANES:(j + 1) * LANES]
            a = jnp.where(lo, slab, 0.0).astype(BF16)
            b = jnp.where(lo, 0.0, slab).astype(BF16)
            s = (jnp.dot(wmix_ref[2 * j], a, preferred_element_type=F32)
                 + jnp.dot(wmix_ref[2 * j + 1], b, preferred_element_type=F32)
                 + bmix_ref[:, j * LANES:(j + 1) * LANES])
            oa_ref[r0:r1, j * LANES:(j + 1) * LANES] = (u[r0:r1, j * LANES:(j + 1) * LANES] * s).astype(BF16)

    cos, sup, sdn = cos_ref[...], sup_ref[...], sdn_ref[...]
    q0 = 2 * A_WIDTH
    for j in range(B_WIDTH // LANES):
        qn = _head_rms(proj(q0 + j * LANES, q0 + (j + 1) * LANES), gq_ref[...])
        qn_ref[:, j * LANES:(j + 1) * LANES] = (qn * Q_SCALE).astype(BF16)
        qr_ref[:, j * LANES:(j + 1) * LANES] = (_rope(qn, cos, sup, sdn) * Q_SCALE).astype(BF16)
    k0 = q0 + B_WIDTH

    def emit(val, state_ref, bf16_ref):
        state_ref[0] = val.T
        bf16_ref[...] = val.astype(BF16)

    emit(proj(k0, k0 + LANES), kc_ref, kc16_ref)
    emit(proj(k0 + LANES, k0 + 2 * LANES), vc_ref, vc16_ref)
    emit(_rope(_head_rms(proj(k0 + 2 * LANES, k0 + 3 * LANES), gks_ref[...]), cos, sup, sdn), ks_ref, ks16_ref)
    emit(proj(k0 + 3 * LANES, k0 + 4 * LANES), vs_ref, vs16_ref)
    emit(_rope(_head_rms(proj(k0 + 4 * LANES, k0 + 5 * LANES), gkw_ref[...]), cos, sup, sdn), kw_ref, kw16_ref)
    emit(proj(k0 + 5 * LANES, k0 + 6 * LANES), vw_ref, vw16_ref)
    gate_ref[...] = proj(k0 + 6 * LANES, k0 + 7 * LANES)


def _in_proj(x, wts, tabs, wmix, bmix, *, tm, seg_rows):
    n = x.shape[0]
    tiles_per_seg = seg_rows // tm
    n_seg = n // seg_rows
    row = lambda w: pl.BlockSpec((tm, w), lambda i: (i, 0))
    full = lambda a: pl.BlockSpec(a.shape, lambda i: (0,) * a.ndim)
    cos, sup, sdn = tabs
    ins = [x, wts['g1'], wts['w_a'], wts['ln_g'], wts['ln_b'], wts['g_q'], wts['g_ks'], wts['g_kw'],
           cos, sup, sdn, wmix, bmix]
    tab = pl.BlockSpec((tm, LANES), lambda i: (i % tiles_per_seg, 0))
    in_specs = [row(D_MODEL)] + [full(a) for a in ins[1:8]] + [tab] * 3 + [full(wmix), full(bmix)]
    f32o = lambda w: jax.ShapeDtypeStruct((n, w), F32)
    b16o = lambda w: jax.ShapeDtypeStruct((n, w), BF16)
    out_shape = ([b16o(A_WIDTH), jax.ShapeDtypeStruct((n_seg, CHUNK, A_WIDTH), F32), b16o(B_WIDTH), b16o(B_WIDTH)]
                 + [jax.ShapeDtypeStruct((n_seg, LANES, seg_rows), F32)] * 6 + [b16o(LANES)] * 6 + [f32o(LANES)])
    state = pl.BlockSpec((1, LANES, tm), lambda i: (i // tiles_per_seg, 0, i % tiles_per_seg))
    out_specs = ([row(A_WIDTH), pl.BlockSpec((1, CHUNK, A_WIDTH), lambda i: (i // tiles_per_seg, 0, 0)),
                  row(B_WIDTH), row(B_WIDTH)] + [state] * 6 + [row(LANES)] * 7)
    return pl.pallas_call(
        functools.partial(_in_proj_body, tm=tm, tiles_per_seg=tiles_per_seg),
        grid=(n // tm,), in_specs=in_specs, out_specs=out_specs, out_shape=out_shape,
        compiler_params=pltpu.CompilerParams(dimension_semantics=("arbitrary",), vmem_limit_bytes=VMEM_LIMIT),
        name="in_proj")(*ins)


COMPRESS_ROWS = 1024


def _compress_math(load_rows, m, w1ab_ref, pe_ref, w1d_ref, w2d_ref, g_ref, norm):
    step = min(COMPRESS_ROWS, m)
    f = jnp.concatenate(
        [jnp.dot(load_rows(r, r + step).astype(BF16), w1ab_ref[...], preferred_element_type=F32)
         for r in range(0, m, step)], axis=0)
    bias = jnp.dot(pe_ref[...], w1d_ref[...], preferred_element_type=F32)[0:1]
    second_next = pltpu.roll(f[:, LANES:], m - 1, 0)
    act = jax.nn.gelu(f[:, :LANES] + second_next + bias)
    out = jnp.dot(act.astype(BF16), w2d_ref[...], preferred_element_type=F32)
    if norm:
        out = _head_rms(out, g_ref[...])
    return out.astype(BF16)


def _compress_body(ch_ref, w1ab_ref, pe_ref, w1d_ref, w2d_ref, g_ref, out_ref, *, norm):
    out_ref[0] = _compress_math(lambda r0, r1: ch_ref[0, r0:r1], ch_ref.shape[1],
                                w1ab_ref, pe_ref, w1d_ref, w2d_ref, g_ref, norm)


def _compress(ch, cw, *, norm):
    nb, m, _ = ch.shape
    full = lambda a: pl.BlockSpec(a.shape, lambda b: (0,) * a.ndim)
    ws = [cw['w1ab'], cw['pe'], cw['w1d'], cw['w2d'], cw['g']]
    return pl.pallas_call(
        functools.partial(_compress_body, norm=norm),
        grid=(nb,),
        in_specs=[pl.BlockSpec((1, m, CHUNK_LANES), lambda b: (b, 0, 0))] + [full(a) for a in ws],
        out_specs=pl.BlockSpec((1, m, LANES), lambda b: (b, 0, 0)),
        out_shape=jax.ShapeDtypeStruct((nb, m, LANES), BF16),
        compiler_params=pltpu.CompilerParams(dimension_semantics=("arbitrary",), vmem_limit_bytes=VMEM_LIMIT),
        name="compress_k" if norm else "compress_v")(ch, *ws)


def _page_copy(pool_ref, page, buf_ref, slot, i, sem_ref):
    return pltpu.make_async_copy(pool_ref.at[page], buf_ref.at[slot, i], sem_ref.at[slot])


def _paged_compress_body(pt_ref, pool_ref, perm_ref, w1ab_ref, pe_ref, w1d_ref, w2d_ref, g_ref, out_ref,
                         buf_ref, sem_ref, ch_ref, *, n_pages, norm):
    b = pl.program_id(0)

    def fetch(bb, slot):
        def one(i, carry):
            _page_copy(pool_ref, pt_ref[bb * n_pages + i], buf_ref, slot, i, sem_ref).start()
            return carry
        lax.fori_loop(0, n_pages, one, 0, unroll=8)

    @pl.when(b == 0)
    def _():
        fetch(0, 0)

    @pl.when(b + 1 < pl.num_programs(0))
    def _():
        fetch(b + 1, (b + 1) % 2)

    slot = b % 2

    pltpu.make_async_copy(pool_ref.at[pl.ds(0, n_pages)], buf_ref.at[slot], sem_ref.at[slot]).wait()

    chunks = PAGE_SIZE // CMP_STRIDE

    def regroup(q, carry):
        xt = jnp.concatenate([buf_ref[slot, 2 * q], buf_ref[slot, 2 * q + 1]], axis=1).astype(BF16)
        y = jnp.dot(xt, perm_ref[...], preferred_element_type=F32)
        t0, t1 = y[:, :PAGE_SIZE].T, y[:, PAGE_SIZE:].T
        rows = pl.ds(pl.multiple_of(q * 2 * chunks, 2 * chunks), 2 * chunks)
        for r in range(CMP_STRIDE):
            piece = jnp.concatenate([t0[r * chunks:(r + 1) * chunks], t1[r * chunks:(r + 1) * chunks]], axis=0)
            ch_ref[rows, r * KV_WIDTH:(r + 1) * KV_WIDTH] = piece.astype(BF16)
        return carry
    lax.fori_loop(0, n_pages // 2, regroup, 0, unroll=16)
    m = n_pages * chunks
    out_ref[0] = _compress_math(lambda r0, r1: ch_ref[r0:r1], m, w1ab_ref, pe_ref, w1d_ref, w2d_ref, g_ref, norm)


def _paged_compress(pt_flat, pool, cw, *, nb, n_pages, norm):
    m = n_pages * (PAGE_SIZE // CMP_STRIDE)
    full = lambda a: pl.BlockSpec(a.shape, lambda b, pt: (0,) * a.ndim)
    lane = jnp.arange(2 * PAGE_SIZE)
    page, chunk, offset = lane // PAGE_SIZE, (lane % PAGE_SIZE) // CMP_STRIDE, lane % CMP_STRIDE
    dest = page * PAGE_SIZE + offset * (PAGE_SIZE // CMP_STRIDE) + chunk
    perm = (dest[:, None] == lane[None, :]).astype(BF16)
    ws = [perm, cw['w1ab'], cw['pe'], cw['w1d'], cw['w2d'], cw['g']]
    grid_spec = pltpu.PrefetchScalarGridSpec(
        num_scalar_prefetch=1, grid=(nb,),
        in_specs=[pl.BlockSpec(memory_space=pl.ANY)] + [full(a) for a in ws],
        out_specs=pl.BlockSpec((1, m, LANES), lambda b, pt: (b, 0, 0)),
        scratch_shapes=[pltpu.VMEM((2, n_pages, KV_WIDTH, PAGE_SIZE), F32), pltpu.SemaphoreType.DMA((2,)),
                        pltpu.VMEM((m, CHUNK_LANES), BF16)])
    return pl.pallas_call(
        functools.partial(_paged_compress_body, n_pages=n_pages, norm=norm),
        grid_spec=grid_spec, out_shape=jax.ShapeDtypeStruct((nb, m, LANES), BF16),
        compiler_params=pltpu.CompilerParams(dimension_semantics=("arbitrary",), vmem_limit_bytes=VMEM_LIMIT),
        name="paged_compress_k" if norm else "paged_compress_v")(pt_flat, pool, *ws)


def _split3(x):
    hi = x.astype(BF16)
    r1 = x - hi.astype(F32)
    mid = r1.astype(BF16)
    lo = (r1 - mid.astype(F32)).astype(BF16)
    return hi, mid, lo


def _dot_nt(a, b):
    return lax.dot_general(a, b, (((1,), (1,)), ((), ())), preferred_element_type=F32)


def _topk_mask_t(rank, n_pick):
    n_blk = rank.shape[0]
    jidx = lax.broadcasted_iota(jnp.int32, rank.shape, 0).astype(F32)

    def body(_, carry):
        r, sel = carry
        m = jnp.max(r, axis=0, keepdims=True)
        first = jnp.min(jnp.where(r == m, jidx, float(n_blk)), axis=0, keepdims=True)
        pick = jidx == first
        return jnp.where(pick, -jnp.inf, r), jnp.where(pick, 1.0, sel)

    _, sel = lax.fori_loop(0, n_pick, body, (rank, jnp.zeros_like(rank)), unroll=True)
    return sel


def _masked_softmax(s, mask, row_valid):
    s = jnp.where(mask, s, NEG)
    e = jnp.exp2(s - jnp.max(s, axis=-1, keepdims=True))
    return e * jnp.where(row_valid, 1.0 / jnp.sum(e, axis=-1, keepdims=True), 0.0)


def _gate_col(gt, head, branch):
    c = head * 3 + branch
    return gt[:, c:c + 1]


def _prompt_attn_body(qn_ref, qr_ref, gate_ref, kc_ref, vc_ref, kst_ref, vs_ref, kw_ref, vw_ref, ebt_ref, at_ref,
                      o_ref, m_ref, l_ref, acc_ref, *, seq, n_cmp):
    i = pl.program_id(1)
    s0 = i * Q_TILE
    nc_pad = kc_ref.shape[1]
    n_sel = seq // SEL_BLOCK
    lo = _low_half((Q_TILE, LANES))
    pos_col = s0 + lax.broadcasted_iota(jnp.int32, (Q_TILE, 1), 0)

    def head_rows(q_ref):
        parts = []
        for hd in range(N_HEADS):
            keep = lo if hd < GROUP else jnp.logical_not(lo)
            j = hd % GROUP
            parts.append(jnp.where(keep, q_ref[0, :, j * LANES:(j + 1) * LANES], 0))
        return jnp.concatenate(parts, axis=0)

    cidx = lax.broadcasted_iota(jnp.int32, (Q_TILE, nc_pad), 1)
    m_c = (cidx * CMP_STRIDE + (CMP_BLOCK - 1) <= pos_col) & (cidx < n_cmp)
    jrow = lax.broadcasted_iota(jnp.int32, (n_sel, Q_TILE), 0)
    pos_row = s0 + lax.broadcasted_iota(jnp.int32, (n_sel, Q_TILE), 1)
    cur = pos_row >> 6
    forced = (jrow == 0) | (jrow == cur) | (jrow == cur - 1)
    valid_b = jrow * SEL_BLOCK <= pos_row
    q_sel = head_rows(qr_ref)
    n_win = WINDOW + Q_TILE
    w0 = pl.multiple_of(jnp.maximum(s0 - WINDOW, 0), Q_TILE)
    dpos = pos_col - (w0 + lax.broadcasted_iota(jnp.int32, (Q_TILE, n_win), 1))
    m_w = (dpos >= 0) & (dpos < WINDOW)
    s_c = _dot_nt(head_rows(qn_ref), kc_ref[0]).reshape(N_HEADS, Q_TILE, nc_pad)
    s_w = _dot_nt(q_sel, kw_ref[0, pl.ds(w0, n_win), :]).reshape(N_HEADS, Q_TILE, n_win)
    p = _masked_softmax(s_c, m_c[None], ((pos_col >= CMP_BLOCK - 1) & (n_cmp > 0))[None])
    o_cmp = jnp.dot(p.reshape(N_HEADS * Q_TILE, nc_pad).astype(BF16), vc_ref[0], preferred_element_type=F32)
    imps = [jnp.sum(p[kv * GROUP:(kv + 1) * GROUP], axis=0) for kv in range(N_KV)]
    p_w = _masked_softmax(s_w, m_w[None], True)
    o_win = jnp.dot(p_w.reshape(N_HEADS * Q_TILE, n_win).astype(BF16), vw_ref[0, pl.ds(w0, n_win), :],
                    preferred_element_type=F32)

    notsel = []
    for kv in range(N_KV):
        imp = imps[kv]
        blk_t = sum(_dot_nt(at_ref[...], part) for part in _split3(imp))
        rank = jnp.where(forced, FORCED_BONUS, jnp.where(valid_b, blk_t, -FORCED_BONUS))
        sel_t = _topk_mask_t(rank, min(N_SELECT, n_sel))
        notsel += [(1.0 - sel_t.T).astype(BF16)] * GROUP

    m_ref[...] = jnp.full(m_ref.shape, NEG, F32)
    l_ref[...] = jnp.zeros(l_ref.shape, F32)
    acc_ref[...] = jnp.zeros(acc_ref.shape, F32)
    q_aug = jnp.concatenate([q_sel, jnp.concatenate(notsel, axis=0)], axis=1)

    def tile_step(tix, lane0, width, diagonal):
        lanes = slice(lane0, lane0 + width)
        key0 = pl.multiple_of(tix * KEY_TILE + lane0, width)
        k_aug = jnp.concatenate([kst_ref[0, tix, :, lanes], ebt_ref[tix, :, lanes]], axis=0)
        vt = vs_ref[0, pl.ds(key0, width), :]
        if diagonal:
            key = key0 + lax.broadcasted_iota(jnp.int32, (Q_TILE, width), 1)
            future = jnp.where(key <= pos_col, 0.0, NEG)
        part = N_HEADS * Q_TILE // FLASH_SPLITS
        scores = [jnp.dot(q_aug[h * part:(h + 1) * part], k_aug, preferred_element_type=F32)
                  for h in range(FLASH_SPLITS)]
        for h in range(FLASH_SPLITS):
            rows = slice(h * part, (h + 1) * part)
            s = scores[h]
            if diagonal:
                s = (s.reshape(part // Q_TILE, Q_TILE, width) + future[None]).reshape(part, width)
            m_old = m_ref[rows]
            m_new = jnp.maximum(m_old, jnp.max(s, axis=-1, keepdims=True))
            alpha = jnp.exp2(m_old - m_new)
            p_t = jnp.exp2(s - m_new)
            l_ref[rows] = alpha * l_ref[rows] + jnp.sum(p_t, axis=-1, keepdims=True)
            acc_ref[rows] = alpha * acc_ref[rows] + jnp.dot(p_t.astype(BF16), vt, preferred_element_type=F32)
            m_ref[rows] = m_new

    def tile_body(tix, carry):
        tile_step(tix, 0, KEY_TILE, False)
        return carry

    q_per_key = KEY_TILE // Q_TILE
    last_tile = lax.shift_right_logical(i, q_per_key.bit_length() - 1)
    lax.fori_loop(0, last_tile, tile_body, 0)
    half = KEY_TILE // 2
    in_second_half = (i & (q_per_key - 1)) >= q_per_key // 2

    @pl.when(in_second_half)
    def _():
        tile_step(last_tile, 0, half, False)
        tile_step(last_tile, half, half, True)

    @pl.when(jnp.logical_not(in_second_half))
    def _():
        tile_step(last_tile, 0, half, True)

    o_sel = acc_ref[...] / l_ref[...]
    gt = jax.nn.sigmoid(gate_ref[0])
    o_heads = []
    for hd in range(N_HEADS):
        rows = slice(hd * Q_TILE, (hd + 1) * Q_TILE)
        o_heads.append(_gate_col(gt, hd, 0) * o_cmp[rows] + _gate_col(gt, hd, 1) * o_sel[rows]
                       + _gate_col(gt, hd, 2) * o_win[rows])
    for j in range(GROUP):
        o_ref[0, :, j * LANES:(j + 1) * LANES] = jnp.where(lo, o_heads[j], o_heads[GROUP + j]).astype(BF16)


def _prompt_attn(qn, qr, gate, kc, vc, kst, vs16, kw16, vw16, ebt, a_t, *, n_cmp):
    nb, seq, _ = qn.shape
    qspec = lambda w: pl.BlockSpec((1, Q_TILE, w), lambda b, i: (b, i, 0))
    seqspec = lambda a: pl.BlockSpec((1,) + a.shape[1:], lambda b, i: (b,) + (0,) * (a.ndim - 1))
    full = lambda a: pl.BlockSpec(a.shape, lambda b, i: (0,) * a.ndim)
    rows = N_HEADS * Q_TILE
    return pl.pallas_call(
        functools.partial(_prompt_attn_body, seq=seq, n_cmp=n_cmp),
        grid=(nb, seq // Q_TILE),
        in_specs=[qspec(B_WIDTH), qspec(B_WIDTH), qspec(LANES), seqspec(kc), seqspec(vc),
                  seqspec(kst), seqspec(vs16), seqspec(kw16), seqspec(vw16), full(ebt), full(a_t)],
        out_specs=qspec(B_WIDTH),
        out_shape=jax.ShapeDtypeStruct((nb, seq, B_WIDTH), BF16),
        scratch_shapes=[pltpu.VMEM((rows, 1), F32), pltpu.VMEM((rows, 1), F32), pltpu.VMEM((rows, LANES), F32)],
        compiler_params=pltpu.CompilerParams(dimension_semantics=("arbitrary", "arbitrary"),
                                             vmem_limit_bytes=VMEM_LIMIT),
        name="prompt_attn")(qn, qr, gate, kc, vc, kst, vs16, kw16, vw16, ebt, a_t)


TOK_PAD = 8
ROWS = N_HEADS * TOK_PAD
ROWS_PAD = 128


def _page_copy_lanes(pool_ref, page, buf_ref, slot, i, sem_ref):
    return pltpu.make_async_copy(pool_ref.at[page], buf_ref.at[slot, :, pl.ds(i * PAGE_SIZE, PAGE_SIZE)],
                                 sem_ref.at[slot])


def _sample_attn_body(pt_ref, qn_ref, qr_ref, gate_ref, kc_ref, vc_ref, kwc_ref, vwc_ref,
                      ksn_ref, vsn_ref, kwn_ref, vwn_ref, at_ref, gsum_ref, ebt_ref, kpool_ref, vpool_ref,
                      o_ref, kbuf_ref, vbuf_ref, ksem_ref, vsem_ref, notsel_ref, m_ref, l_ref, acc_ref, ocw_ref,
                      q_ref, *, n_chunks, pages_per_chunk, past, n_new, n_cmp):
    b = pl.program_id(0)
    c = pl.program_id(1)
    step = b * n_chunks + c
    n_steps = pl.num_programs(0) * n_chunks
    n_past_blk = past // SEL_BLOCK
    blk_per_chunk = n_past_blk // n_chunks

    def fetch(st, slot):
        def one(i, carry):
            page = pt_ref[st * pages_per_chunk + i]
            _page_copy_lanes(kpool_ref, page, kbuf_ref, slot, i, ksem_ref).start()
            _page_copy_lanes(vpool_ref, page, vbuf_ref, slot, i, vsem_ref).start()
            return carry
        lax.fori_loop(0, pages_per_chunk, one, 0, unroll=8)

    @pl.when(step == 0)
    def _():
        fetch(0, 0)

    @pl.when(step + 1 < n_steps)
    def _():
        fetch(step + 1, (step + 1) % 2)

    lo = _low_half((TOK_PAD, LANES))
    tok_col = lax.broadcasted_iota(jnp.int32, (ROWS, 1), 0) % TOK_PAD
    pos_col = past + tok_col

    def rows_q(ref):
        parts = []
        for hd in range(N_HEADS):
            keep = lo if hd < GROUP else jnp.logical_not(lo)
            j = hd % GROUP
            parts.append(jnp.where(keep, ref[0, :, j * LANES:(j + 1) * LANES], 0))
        return jnp.concatenate(parts, axis=0)

    def gate_rows(gt, branch):
        return jnp.concatenate([_gate_col(gt, hd, branch) for hd in range(N_HEADS)], axis=0)

    @pl.when(c == 0)
    def _():
        gt = jax.nn.sigmoid(gate_ref[0])
        nc_pad = kc_ref.shape[1]
        cidx = lax.broadcasted_iota(jnp.int32, (ROWS, nc_pad), 1)
        m_c = (cidx * CMP_STRIDE + (CMP_BLOCK - 1) <= pos_col) & (cidx < n_cmp)
        p = _masked_softmax(_dot_nt(rows_q(qn_ref), kc_ref[0]), m_c, (pos_col >= CMP_BLOCK - 1) & (n_cmp > 0))
        o_c = jnp.dot(p.astype(BF16), vc_ref[0], preferred_element_type=F32)
        imp = sum(jnp.dot(gsum_ref[...], part, preferred_element_type=F32) for part in _split3(p))
        blk_t = sum(_dot_nt(at_ref[...], part) for part in _split3(imp))
        jrow = lax.broadcasted_iota(jnp.int32, blk_t.shape, 0)
        rank = jnp.where((jrow == 0) | (jrow == n_past_blk - 1), FORCED_BONUS, blk_t)
        n_top = min(N_SELECT, n_past_blk + 1)
        sel_t = _topk_mask_t(rank, n_top - 1)
        for cc in range(n_chunks):
            blk = sel_t[cc * blk_per_chunk:(cc + 1) * blk_per_chunk]
            notsel_ref[cc] = (1.0 - blk.T[:ROWS]).astype(BF16)
        q_r = rows_q(qr_ref)
        q_ref[...] = q_r
        n_buf = kwc_ref.shape[2]
        kidx = lax.broadcasted_iota(jnp.int32, (ROWS, n_buf), 1)
        dpos = pos_col - (past - n_buf + kidx)
        m_old = (dpos >= 0) & (dpos < WINDOW)
        knew = lax.broadcasted_iota(jnp.int32, (ROWS, TOK_PAD), 1)
        m_new = (knew <= tok_col) & (knew < n_new)
        s_old = jnp.where(m_old, jnp.dot(q_r, kwc_ref[0].astype(BF16), preferred_element_type=F32), NEG)
        s_new = jnp.where(m_new, _dot_nt(q_r, kwn_ref[0]), NEG)
        mx = jnp.maximum(jnp.max(s_old, axis=-1, keepdims=True), jnp.max(s_new, axis=-1, keepdims=True))
        e_old = jnp.exp2(s_old - mx)
        e_new = jnp.exp2(s_new - mx)
        den = jnp.sum(e_old, axis=-1, keepdims=True) + jnp.sum(e_new, axis=-1, keepdims=True)
        p_old = e_old / den * m_old.astype(F32)
        p_new = e_new / den * m_new.astype(F32)
        o_w = (_dot_nt(p_old.astype(BF16), vwc_ref[0].astype(BF16))
               + jnp.dot(p_new.astype(BF16), vwn_ref[0], preferred_element_type=F32))
        ocw_ref[...] = gate_rows(gt, 0) * o_c + gate_rows(gt, 2) * o_w
        s = jnp.where(m_new, _dot_nt(q_r, ksn_ref[0]), NEG)
        m0 = jnp.max(s, axis=-1, keepdims=True)
        e = jnp.exp2(s - m0) * m_new.astype(F32)
        m_ref[...] = m0
        l_ref[...] = jnp.sum(e, axis=-1, keepdims=True)
        acc_ref[...] = jnp.dot(e.astype(BF16), vsn_ref[0], preferred_element_type=F32)

    slot = step % 2

    pltpu.make_async_copy(kbuf_ref.at[slot], kbuf_ref.at[slot], ksem_ref.at[slot]).wait()
    pltpu.make_async_copy(vbuf_ref.at[slot], vbuf_ref.at[slot], vsem_ref.at[slot]).wait()

    q_aug = jnp.concatenate([q_ref[...], notsel_ref[c]], axis=1)
    k_aug = jnp.concatenate([kbuf_ref[slot].astype(BF16), ebt_ref[...]], axis=0)
    s = jnp.dot(q_aug, k_aug, preferred_element_type=F32)
    m_old = m_ref[...]
    m_new = jnp.maximum(m_old, jnp.max(s, axis=-1, keepdims=True))
    alpha = jnp.exp2(m_old - m_new)
    p = jnp.exp2(s - m_new)
    l_ref[...] = alpha * l_ref[...] + jnp.sum(p, axis=-1, keepdims=True)
    acc_ref[...] = alpha * acc_ref[...] + _dot_nt(p.astype(BF16), vbuf_ref[slot].astype(BF16))
    m_ref[...] = m_new

    @pl.when(c == n_chunks - 1)
    def _():
        gt = jax.nn.sigmoid(gate_ref[0])
        o = ocw_ref[...] + gate_rows(gt, 1) * (acc_ref[...] / l_ref[...])
        for j in range(GROUP):
            top = o[j * TOK_PAD:(j + 1) * TOK_PAD]
            bot = o[(GROUP + j) * TOK_PAD:(GROUP + j + 1) * TOK_PAD]
            o_ref[0, :, j * LANES:(j + 1) * LANES] = jnp.where(lo, top, bot).astype(BF16)


def _sample_attn(pt_flat, qn, qr, gate, kc, vc, kwc, vwc, ksn, vsn, kwn, vwn, a_t, gsum, ebt, kpool, vpool,
                 *, n_chunks, pages_per_chunk, past, n_new, n_cmp):
    nb = qn.shape[0]
    chunk_keys = pages_per_chunk * PAGE_SIZE
    blk_per_chunk = chunk_keys // SEL_BLOCK
    bspec = lambda a: pl.BlockSpec((1,) + a.shape[1:], lambda b, c, pt: (b, 0, 0))
    full = lambda a: pl.BlockSpec(a.shape, lambda b, c, pt: (0,) * a.ndim)
    anyspec = pl.BlockSpec(memory_space=pl.ANY)
    blocked = [qn, qr, gate, kc, vc, kwc, vwc, ksn, vsn, kwn, vwn]
    grid_spec = pltpu.PrefetchScalarGridSpec(
        num_scalar_prefetch=1, grid=(nb, n_chunks),
        in_specs=[bspec(a) for a in blocked] + [full(a_t), full(gsum), full(ebt), anyspec, anyspec],
        out_specs=pl.BlockSpec((1, TOK_PAD, B_WIDTH), lambda b, c, pt: (b, 0, 0)),
        scratch_shapes=[pltpu.VMEM((2, KV_WIDTH, chunk_keys), F32), pltpu.VMEM((2, KV_WIDTH, chunk_keys), F32),
                        pltpu.SemaphoreType.DMA((2,)), pltpu.SemaphoreType.DMA((2,)),
                        pltpu.VMEM((n_chunks, ROWS, blk_per_chunk), BF16),
                        pltpu.VMEM((ROWS, 1), F32), pltpu.VMEM((ROWS, 1), F32), pltpu.VMEM((ROWS, LANES), F32),
                        pltpu.VMEM((ROWS, LANES), F32), pltpu.VMEM((ROWS, LANES), BF16)])
    return pl.pallas_call(
        functools.partial(_sample_attn_body, n_chunks=n_chunks, pages_per_chunk=pages_per_chunk, past=past,
                          n_new=n_new, n_cmp=n_cmp),
        grid_spec=grid_spec, out_shape=jax.ShapeDtypeStruct((nb, TOK_PAD, B_WIDTH), BF16),
        compiler_params=pltpu.CompilerParams(dimension_semantics=("arbitrary", "arbitrary"),
                                             vmem_limit_bytes=VMEM_LIMIT),
        name="sample_attn")(pt_flat, *blocked, a_t, gsum, ebt, kpool, vpool)


FF_TILE = 1024


def _merge_ffn_body(x_ref, oa_ref, ob_ref, g1_ref, wmg_ref, wba_ref, wbb_ref, wout_ref, g2_ref, wup_ref, wdn_ref,
                    y_ref):
    x = x_ref[...]
    h = _rms_rows(x, g1_ref[...]).astype(BF16)
    y_a = jnp.dot(oa_ref[...], wba_ref[...], preferred_element_type=F32)
    y_b = jnp.dot(ob_ref[...], wbb_ref[...], preferred_element_type=F32)
    g_a = jax.nn.sigmoid(jnp.dot(h, wmg_ref[:, :D_MODEL], preferred_element_type=F32))
    g_b = jax.nn.sigmoid(jnp.dot(h, wmg_ref[:, D_MODEL:], preferred_element_type=F32))
    x1 = x + jnp.dot((g_a * y_a + g_b * y_b).astype(BF16), wout_ref[...], preferred_element_type=F32)
    h2 = _rms_rows(x1, g2_ref[...]).astype(BF16)
    y = x1
    for f in range(D_FF // FF_TILE):
        up = jnp.dot(h2, wup_ref[:, f * FF_TILE:(f + 1) * FF_TILE], preferred_element_type=F32)
        act = jnp.square(jnp.maximum(up, 0.0)).astype(BF16)
        y = y + jnp.dot(act, wdn_ref[f * FF_TILE:(f + 1) * FF_TILE, :], preferred_element_type=F32)
    y_ref[...] = y


def _merge_ffn(x, o_a, o_b, wts, *, tm):
    n = x.shape[0]
    row = lambda w: pl.BlockSpec((tm, w), lambda i: (i, 0))
    const = lambda a: pl.BlockSpec(a.shape, lambda i: (0,) * a.ndim, pipeline_mode=pl.Buffered(1))
    ws = [wts['g1'], wts['w_mg'], wts['w_br_a'], wts['w_br_b'], wts['w_out'], wts['g2'], wts['w_up'], wts['w_down']]
    return pl.pallas_call(
        _merge_ffn_body, grid=(n // tm,),
        in_specs=[row(D_MODEL), row(A_WIDTH), row(B_WIDTH)] + [const(a) for a in ws],
        out_specs=row(D_MODEL), out_shape=jax.ShapeDtypeStruct((n, D_MODEL), F32),
        compiler_params=pltpu.CompilerParams(dimension_semantics=("arbitrary",), vmem_limit_bytes=VMEM_LIMIT),
        name="merge_ffn")(x, o_a, o_b, *ws)


def _rope_tables(pos):
    half = ROT_DIM // 2
    lane = jnp.arange(LANES) % HEAD_DIM
    inv = ROPE_THETA ** (-(2 * (lane % half)).astype(F32) / ROT_DIM)
    ang = pos.astype(F32)[:, None] * inv[None, :]
    cos, sin = jnp.cos(ang), jnp.sin(ang)
    cos_t = jnp.where(lane < ROT_DIM, cos, 1.0)
    sin_up = jnp.where((lane >= half) & (lane < ROT_DIM), sin, 0.0)
    sin_dn = jnp.where(lane < half, -sin, 0.0)
    return cos_t, sin_up, sin_dn


def _two(g):
    return jnp.concatenate([g, g])[None, :].astype(F32)


def _layer_weights(l, g_norm1, w_in, ln_v_g, ln_v_b, g_q, g_ks, g_kw, w_branch, w_out, g_norm2, w_up, w_down):
    w = w_in[l]
    q0, k0 = 2 * A_WIDTH, 2 * A_WIDTH + B_WIDTH
    g0 = k0 + 6 * KV_WIDTH
    perm = jnp.array(HEAD_PERM)
    q_cols = w[:, q0:k0].reshape(D_MODEL, N_HEADS, HEAD_DIM)[:, perm].reshape(D_MODEL, B_WIDTH)
    gate_cols = jnp.pad(w[:, g0:g0 + 3 * N_HEADS], ((0, 0), (0, LANES - 3 * N_HEADS)))
    w_a = jnp.concatenate([w[:, :q0], q_cols, w[:, k0:g0], gate_cols], axis=1).astype(BF16)
    wb = w_branch[l]
    w_br_b = wb[A_WIDTH:].reshape(N_HEADS, HEAD_DIM, D_MODEL)[perm].reshape(B_WIDTH, D_MODEL)
    return dict(
        g1=g_norm1[l][None, :], w_a=w_a, ln_g=ln_v_g[l][None, :], ln_b=ln_v_b[l][None, :],
        g_q=_two(g_q[l]), g_ks=_two(g_ks[l]), g_kw=_two(g_kw[l]),
        w_mg=w[:, g0 + 3 * N_HEADS:].astype(BF16), w_br_a=wb[:A_WIDTH].astype(BF16), w_br_b=w_br_b.astype(BF16),
        w_out=w_out[l].astype(BF16), g2=g_norm2[l][None, :], w_up=w_up[l].astype(BF16),
        w_down=w_down[l].astype(BF16))


def _compress_weights(w1, w2, pe, g):
    half = CMP_STRIDE * HEAD_DIM
    eye = jnp.eye(N_KV, dtype=F32)

    def spread(wh):
        return jnp.einsum('rde,kl->rkdle', wh.reshape(CMP_STRIDE, HEAD_DIM, HEAD_DIM), eye).reshape(
            CHUNK_LANES, KV_WIDTH)

    w1ab = jnp.concatenate([spread(w1[:half]), spread(w1[half:])], axis=1).astype(BF16)
    pe_rows = jnp.zeros((8, CMP_BLOCK * HEAD_DIM), F32).at[0].set(pe.reshape(-1)).astype(BF16)
    w1d = jnp.concatenate([w1, w1], axis=1).astype(BF16)
    w2d = jnp.einsum('de,kl->kdle', w2, eye).reshape(KV_WIDTH, KV_WIDTH).astype(BF16)
    return dict(w1ab=w1ab, pe=pe_rows, w1d=w1d, w2d=w2d, g=_two(g))


def _block_score_matrix(n_blk, n_chunk):
    j = jnp.arange(n_blk)[:, None]
    c = jnp.arange(n_chunk)[None, :]
    per = SEL_BLOCK // CMP_STRIDE
    a = ((c >= per * j) & (c <= per * j + per - 1)).astype(F32) + ((c >= per * j - 1) & (c <= per * j + per - 2))
    return a.astype(BF16)


def kernel(x_prompt, x_sample, cache_k_cmp, cache_v_cmp, cache_k_sel, cache_v_sel, cache_k_win, cache_v_win,
           page_table, g_norm1, w_in, ln_v_g, ln_v_b, w_s, b_s, g_q, g_kc, g_ks, g_kw, w_ck1, w_ck2, pe_k,
           w_cv1, w_cv2, pe_v, w_branch, w_out, g_norm2, w_up, w_down):
    nb, seq, _ = x_prompt.shape
    db, dseq, _ = x_sample.shape
    depth = w_in.shape[0]
    n_pages = page_table.shape[1]
    past = n_pages * PAGE_SIZE
    assert depth == 1 and seq % KEY_TILE == 0 and seq >= WINDOW + Q_TILE and dseq <= TOK_PAD
    assert past % (SEL_BLOCK * LANES) == 0 and (past + dseq) // CMP_STRIDE == past // CMP_STRIDE
    l = 0
    wts = _layer_weights(l, g_norm1, w_in, ln_v_g, ln_v_b, g_q, g_ks, g_kw, w_branch, w_out, g_norm2, w_up, w_down)
    cw_k = _compress_weights(w_ck1[l], w_ck2[l], pe_k[l], g_kc[l])
    cw_v = _compress_weights(w_cv1[l], w_cv2[l], pe_v[l], g_kc[l])
    causal = jnp.tril(jnp.ones((CHUNK, CHUNK), dtype=bool))
    ws_tril = jnp.where(causal[None], w_s[l], 0.0)

    n_p = nb * seq
    xp = x_prompt.reshape(n_p, D_MODEL)
    tabs_p = _rope_tables(jnp.arange(seq, dtype=jnp.int32))
    bmix_p = jnp.repeat(b_s[l].T, A_WIDTH // A_GROUPS, axis=1)
    (oa_p, vlast_p, qn_p, qr_p, kc_p, vc_p, ks_p, vs_p, kw_p, vw_p, kc16, vc16, ks16, vs16, kw16, vw16,
     gate_p) = _in_proj(
        xp, wts, tabs_p, ws_tril.astype(BF16), bmix_p, tm=512, seg_rows=seq)
    n_chunk_p = seq // CMP_STRIDE
    kcmp_p = _compress(kc16.reshape(nb, n_chunk_p, CHUNK_LANES), cw_k, norm=True)
    vcmp_p = _compress(vc16.reshape(nb, n_chunk_p, CHUNK_LANES), cw_v, norm=False)
    n_sel_p = seq // SEL_BLOCK
    n_tiles = seq // KEY_TILE
    kblk = (jnp.arange(seq) // SEL_BLOCK).reshape(n_tiles, 1, KEY_TILE)
    ebt = jnp.where(kblk == jnp.arange(n_sel_p)[None, :, None], MASK_BIAS, 0.0).astype(BF16)
    a_t_p = _block_score_matrix(n_sel_p, n_chunk_p)
    b3 = lambda a: a.reshape(nb, seq, a.shape[-1])
    kst = ks16.reshape(nb, n_tiles, KEY_TILE, KV_WIDTH).transpose(0, 1, 3, 2)
    ob_p = _prompt_attn(b3(qn_p), b3(qr_p), b3(gate_p), kcmp_p, vcmp_p, kst, b3(vs16), b3(kw16), b3(vw16),
                        ebt, a_t_p, n_cmp=n_chunk_p - 1)
    y_p = _merge_ffn(xp, oa_p, ob_p.reshape(n_p, B_WIDTH), wts, tm=512).reshape(nb, seq, D_MODEL)

    n_s = db * dseq
    xs = x_sample.reshape(n_s, D_MODEL)
    tabs_s = _rope_tables(jnp.tile(past + jnp.arange(dseq, dtype=jnp.int32), db))
    wmix_s = jnp.einsum('gpr,bc->gbpcr', ws_tril[:, :dseq, :dseq], jnp.eye(db, dtype=F32)).reshape(
        A_GROUPS, n_s, n_s)
    bmix_s = jnp.tile(bmix_p[:dseq], (db, 1))
    (oa_s, v_s, qn_s, qr_s, kc_s, vc_s, ks_s, vs_s, kw_s, vw_s, _, _, ks16s, vs16s, kw16s, vw16s,
     gate_s) = _in_proj(
        xs, wts, tabs_s, wmix_s.astype(BF16), bmix_s, tm=n_s, seg_rows=n_s)
    pt_flat = page_table.reshape(-1)
    keys_last = lambda a: a.transpose(0, 2, 3, 1).reshape(a.shape[0], KV_WIDTH, a.shape[1])
    kcmp_s = _paged_compress(pt_flat, keys_last(cache_k_cmp[l]), cw_k, nb=db, n_pages=n_pages, norm=True)
    vcmp_s = _paged_compress(pt_flat, keys_last(cache_v_cmp[l]), cw_v, nb=db, n_pages=n_pages, norm=False)
    n_chunk_s = past // CMP_STRIDE
    tokpad = lambda a: jnp.pad(a.reshape(db, dseq, a.shape[-1]), ((0, 0), (0, TOK_PAD - dseq), (0, 0)))
    hh = jnp.arange(ROWS_PAD) // TOK_PAD
    tt = jnp.arange(ROWS_PAD) % TOK_PAD
    gsum = ((hh[:, None] // GROUP == hh[None, :ROWS] // GROUP) & (tt[:, None] == tt[None, :ROWS])
            & (hh[:, None] < N_HEADS)).astype(BF16)
    n_chunks = 2
    chunk_keys = past // n_chunks
    ebt = jnp.where(jnp.arange(chunk_keys)[None, :] // SEL_BLOCK == jnp.arange(chunk_keys // SEL_BLOCK)[:, None],
                    MASK_BIAS, 0.0).astype(BF16)
    ob_s = _sample_attn(
        pt_flat, tokpad(qn_s), tokpad(qr_s), tokpad(gate_s), kcmp_s, vcmp_s,
        keys_last(cache_k_win[l]), keys_last(cache_v_win[l]),
        tokpad(ks16s), tokpad(vs16s), tokpad(kw16s), tokpad(vw16s),
        _block_score_matrix(past // SEL_BLOCK, n_chunk_s), gsum, ebt,
        keys_last(cache_k_sel[l]), keys_last(cache_v_sel[l]),
        n_chunks=n_chunks, pages_per_chunk=n_pages // n_chunks, past=past, n_new=dseq, n_cmp=n_chunk_s - 1)
    y_s = _merge_ffn(xs, oa_s, ob_s[:, :dseq].reshape(n_s, B_WIDTH), wts, tm=n_s).reshape(db, dseq, D_MODEL)

    w_p = min(WINDOW, seq)
    pk = lambda a: a.reshape(1, nb, N_KV, HEAD_DIM, seq).transpose(0, 1, 4, 2, 3)
    sk = lambda a: a[0].T.reshape(1, db, dseq, N_KV, HEAD_DIM)
    return (y_p, y_s, pk(kc_p), pk(vc_p), pk(ks_p), pk(vs_p), pk(kw_p)[:, :, seq - w_p:], pk(vw_p)[:, :, seq - w_p:],
            vlast_p[None], sk(kc_s), sk(vc_s), sk(ks_s), sk(vs_s), sk(kw_s), sk(vw_s),
            v_s.reshape(db, dseq, A_WIDTH)[None])
```

```python
import functools

import jax
import jax.numpy as jnp
from jax import lax
from jax.experimental import pallas as pl
from jax.experimental.pallas import tpu as pltpu

F32 = jnp.float32
BF16 = jnp.bfloat16

D_MODEL = 1024
A_WIDTH = 512
A_GROUPS = 8
CHUNK = 128
N_HEADS = 8
HEAD_DIM = 64
N_KV = 2
GROUP = N_HEADS // N_KV
B_WIDTH = N_HEADS * HEAD_DIM
KV_WIDTH = N_KV * HEAD_DIM
CMP_BLOCK = 32
CMP_STRIDE = 16
SEL_BLOCK = 64
N_SELECT = 16
WINDOW = 512
ROT_DIM = 16
ROPE_THETA = 500000.0
D_FF = 4096
PAGE_SIZE = 128
EPS = 1e-6
NEG = -1e30
FORCED_BONUS = 1e4
N_FORCED = 3
MASK_BIAS = -(2.0 ** 100)
SCALE = HEAD_DIM ** -0.5
Q_SCALE = SCALE * 1.4426950408889634

LANES = 128
Q_TILE = 128
KEY_TILE = 2048
FLASH_SPLITS = 4
CHUNK_LANES = CMP_STRIDE * KV_WIDTH
HEAD_PERM = (0, 4, 1, 5, 2, 6, 3, 7)
VMEM_LIMIT = 56 * 1024 * 1024


def _low_half(shape):
    return lax.broadcasted_iota(jnp.int32, shape, len(shape) - 1) < HEAD_DIM


def _head_rms(xs, g):
    lo = _low_half(xs.shape)
    sq = xs * xs
    sa = jnp.sum(jnp.where(lo, sq, 0.0), axis=-1, keepdims=True)
    sb = jnp.sum(jnp.where(lo, 0.0, sq), axis=-1, keepdims=True)
    ms = jnp.where(lo, sa, sb) * (1.0 / HEAD_DIM)
    return xs * lax.rsqrt(ms + EPS) * g


def _rope(xs, cos, sin_up, sin_dn):
    return xs * cos + pltpu.roll(xs, ROT_DIM // 2, 1) * sin_up + pltpu.roll(xs, LANES - ROT_DIM // 2, 1) * sin_dn


def _rms_rows(x, g):
    return x * lax.rsqrt(jnp.mean(x * x, axis=-1, keepdims=True) + EPS) * g


def _in_proj_body(x_ref, g1_ref, w_ref, lng_ref, lnb_ref, gq_ref, gks_ref, gkw_ref,
                  cos_ref, sup_ref, sdn_ref, wmix_ref, bmix_ref,
                  oa_ref, vlast_ref, qn_ref, qr_ref, kc_ref, vc_ref, ks_ref, vs_ref, kw_ref, vw_ref,
                  kc16_ref, vc16_ref, ks16_ref, vs16_ref, kw16_ref, vw16_ref, gate_ref, *, tm, tiles_per_seg):
    i = pl.program_id(0)
    h = _rms_rows(x_ref[...], g1_ref[...]).astype(BF16)

    def proj(lo, hi):
        return jnp.dot(h, w_ref[:, lo:hi], preferred_element_type=F32)

    u = jax.nn.gelu(proj(0, A_WIDTH))
    v = jax.nn.gelu(proj(A_WIDTH, 2 * A_WIDTH))
    vc = v - jnp.mean(v, axis=-1, keepdims=True)
    vn = vc * lax.rsqrt(jnp.mean(vc * vc, axis=-1, keepdims=True) + EPS) * lng_ref[...] + lnb_ref[...]

    @pl.when(i % tiles_per_seg == tiles_per_seg - 1)
    def _():
        vlast_ref[0] = vn[tm - CHUNK:tm]

    lo = _low_half((CHUNK, LANES))
    for c in range(tm // CHUNK):
        r0, r1 = c * CHUNK, (c + 1) * CHUNK
        for j in range(A_WIDTH // LANES):
            slab = vn[r0:r1, j * LANES:(j + 1) * LANES]
            a = jnp.where(lo, slab, 0.0).astype(BF16)
            b = jnp.where(lo, 0.0, slab).astype(BF16)
            s = (jnp.dot(wmix_ref[2 * j], a, preferred_element_type=F32)
                 + jnp.dot(wmix_ref[2 * j + 1], b, preferred_element_type=F32)
                 + bmix_ref[:, j * LANES:(j + 1) * LANES])
            oa_ref[r0:r1, j * LANES:(j + 1) * LANES] = (u[r0:r1, j * LANES:(j + 1) * LANES] * s).astype(BF16)

    cos, sup, sdn = cos_ref[...], sup_ref[...], sdn_ref[...]
    q0 = 2 * A_WIDTH
    for j in range(B_WIDTH // LANES):
        qn = _head_rms(proj(q0 + j * LANES, q0 + (j + 1) * LANES), gq_ref[...])
        qn_ref[:, j * LANES:(j + 1) * LANES] = (qn * Q_SCALE).astype(BF16)
        qr_ref[:, j * LANES:(j + 1) * LANES] = (_rope(qn, cos, sup, sdn) * Q_SCALE).astype(BF16)
    k0 = q0 + B_WIDTH

    def emit(val, state_ref, bf16_ref):
        state_ref[0] = val.T
        bf16_ref[...] = val.astype(BF16)

    emit(proj(k0, k0 + LANES), kc_ref, kc16_ref)
    emit(proj(k0 + LANES, k0 + 2 * LANES), vc_ref, vc16_ref)
    emit(_rope(_head_rms(proj(k0 + 2 * LANES, k0 + 3 * LANES), gks_ref[...]), cos, sup, sdn), ks_ref, ks16_ref)
    emit(proj(k0 + 3 * LANES, k0 + 4 * LANES), vs_ref, vs16_ref)
    emit(_rope(_head_rms(proj(k0 + 4 * LANES, k0 + 5 * LANES), gkw_ref[...]), cos, sup, sdn), kw_ref, kw16_ref)
    emit(proj(k0 + 5 * LANES, k0 + 6 * LANES), vw_ref, vw16_ref)
    gate_ref[...] = proj(k0 + 6 * LANES, k0 + 7 * LANES)


def _in_proj(x, wts, tabs, wmix, bmix, *, tm, seg_rows):
    n = x.shape[0]
    tiles_per_seg = seg_rows // tm
    n_seg = n // seg_rows
    row = lambda w: pl.BlockSpec((tm, w), lambda i: (i, 0))
    full = lambda a: pl.BlockSpec(a.shape, lambda i: (0,) * a.ndim)
    cos, sup, sdn = tabs
    ins = [x, wts['g1'], wts['w_a'], wts['ln_g'], wts['ln_b'], wts['g_q'], wts['g_ks'], wts['g_kw'],
           cos, sup, sdn, wmix, bmix]
    tab = pl.BlockSpec((tm, LANES), lambda i: (i % tiles_per_seg, 0))
    in_specs = [row(D_MODEL)] + [full(a) for a in ins[1:8]] + [tab] * 3 + [full(wmix), full(bmix)]
    f32o = lambda w: jax.ShapeDtypeStruct((n, w), F32)
    b16o = lambda w: jax.ShapeDtypeStruct((n, w), BF16)
    out_shape = ([b16o(A_WIDTH), jax.ShapeDtypeStruct((n_seg, CHUNK, A_WIDTH), F32), b16o(B_WIDTH), b16o(B_WIDTH)]
                 + [jax.ShapeDtypeStruct((n_seg, LANES, seg_rows), F32)] * 6 + [b16o(LANES)] * 6 + [f32o(LANES)])
    state = pl.BlockSpec((1, LANES, tm), lambda i: (i // tiles_per_seg, 0, i % tiles_per_seg))
    out_specs = ([row(A_WIDTH), pl.BlockSpec((1, CHUNK, A_WIDTH), lambda i: (i // tiles_per_seg, 0, 0)),
                  row(B_WIDTH), row(B_WIDTH)] + [state] * 6 + [row(LANES)] * 7)
    return pl.pallas_call(
        functools.partial(_in_proj_body, tm=tm, tiles_per_seg=tiles_per_seg),
        grid=(n // tm,), in_specs=in_specs, out_specs=out_specs, out_shape=out_shape,
        compiler_params=pltpu.CompilerParams(dimension_semantics=("arbitrary",), vmem_limit_bytes=VMEM_LIMIT),
        name="in_proj")(*ins)


COMPRESS_ROWS = 1024


def _compress_math(load_rows, m, w1ab_ref, pe_ref, w1d_ref, w2d_ref, g_ref, norm):
    step = min(COMPRESS_ROWS, m)
    f = jnp.concatenate(
        [jnp.dot(load_rows(r, r + step).astype(BF16), w1ab_ref[...], preferred_element_type=F32)
         for r in range(0, m, step)], axis=0)
    bias = jnp.dot(pe_ref[...], w1d_ref[...], preferred_element_type=F32)[0:1]
    second_next = pltpu.roll(f[:, LANES:], m - 1, 0)
    act = jax.nn.gelu(f[:, :LANES] + second_next + bias)
    out = jnp.dot(act.astype(BF16), w2d_ref[...], preferred_element_type=F32)
    if norm:
        out = _head_rms(out, g_ref[...])
    return out.astype(BF16)


def _compress_body(ch_ref, w1ab_ref, pe_ref, w1d_ref, w2d_ref, g_ref, out_ref, *, norm):
    out_ref[0] = _compress_math(lambda r0, r1: ch_ref[0, r0:r1], ch_ref.shape[1],
                                w1ab_ref, pe_ref, w1d_ref, w2d_ref, g_ref, norm)


def _compress(ch, cw, *, norm):
    nb, m, _ = ch.shape
    full = lambda a: pl.BlockSpec(a.shape, lambda b: (0,) * a.ndim)
    ws = [cw['w1ab'], cw['pe'], cw['w1d'], cw['w2d'], cw['g']]
    return pl.pallas_call(
        functools.partial(_compress_body, norm=norm),
        grid=(nb,),
        in_specs=[pl.BlockSpec((1, m, CHUNK_LANES), lambda b: (b, 0, 0))] + [full(a) for a in ws],
        out_specs=pl.BlockSpec((1, m, LANES), lambda b: (b, 0, 0)),
        out_shape=jax.ShapeDtypeStruct((nb, m, LANES), BF16),
        compiler_params=pltpu.CompilerParams(dimension_semantics=("arbitrary",), vmem_limit_bytes=VMEM_LIMIT),
        name="compress_k" if norm else "compress_v")(ch, *ws)


def _page_copy(pool_ref, page, buf_ref, slot, i, sem_ref):
    return pltpu.make_async_copy(pool_ref.at[page], buf_ref.at[slot, i], sem_ref.at[slot])


def _paged_compress_body(pt_ref, pool_ref, perm_ref, w1ab_ref, pe_ref, w1d_ref, w2d_ref, g_ref, out_ref,
                         buf_ref, sem_ref, ch_ref, *, n_pages, norm):
    b = pl.program_id(0)

    def fetch(bb, slot):
        def one(i, carry):
            _page_copy(pool_ref, pt_ref[bb * n_pages + i], buf_ref, slot, i, sem_ref).start()
            return carry
        lax.fori_loop(0, n_pages, one, 0, unroll=8)

    @pl.when(b == 0)
    def _():
        fetch(0, 0)

    @pl.when(b + 1 < pl.num_programs(0))
    def _():
        fetch(b + 1, (b + 1) % 2)

    slot = b % 2

    pltpu.make_async_copy(pool_ref.at[pl.ds(0, n_pages)], buf_ref.at[slot], sem_ref.at[slot]).wait()

    chunks = PAGE_SIZE // CMP_STRIDE

    def regroup(q, carry):
        xt = jnp.concatenate([buf_ref[slot, 2 * q], buf_ref[slot, 2 * q + 1]], axis=1).astype(BF16)
        y = jnp.dot(xt, perm_ref[...], preferred_element_type=F32)
        t0, t1 = y[:, :PAGE_SIZE].T, y[:, PAGE_SIZE:].T
        rows = pl.ds(pl.multiple_of(q * 2 * chunks, 2 * chunks), 2 * chunks)
        for r in range(CMP_STRIDE):
            piece = jnp.concatenate([t0[r * chunks:(r + 1) * chunks], t1[r * chunks:(r + 1) * chunks]], axis=0)
            ch_ref[rows, r * KV_WIDTH:(r + 1) * KV_WIDTH] = piece.astype(BF16)
        return carry
    lax.fori_loop(0, n_pages // 2, regroup, 0, unroll=16)
    m = n_pages * chunks
    out_ref[0] = _compress_math(lambda r0, r1: ch_ref[r0:r1], m, w1ab_ref, pe_ref, w1d_ref, w2d_ref, g_ref, norm)


def _paged_compress(pt_flat, pool, cw, *, nb, n_pages, norm):
    m = n_pages * (PAGE_SIZE // CMP_STRIDE)
    full = lambda a: pl.BlockSpec(a.shape, lambda b, pt: (0,) * a.ndim)
    lane = jnp.arange(2 * PAGE_SIZE)
    page, chunk, offset = lane // PAGE_SIZE, (lane % PAGE_SIZE) // CMP_STRIDE, lane % CMP_STRIDE
    dest = page * PAGE_SIZE + offset * (PAGE_SIZE // CMP_STRIDE) + chunk
    perm = (dest[:, None] == lane[None, :]).astype(BF16)
    ws = [perm, cw['w1ab'], cw['pe'], cw['w1d'], cw['w2d'], cw['g']]
    grid_spec = pltpu.PrefetchScalarGridSpec(
        num_scalar_prefetch=1, grid=(nb,),
        in_specs=[pl.BlockSpec(memory_space=pl.ANY)] + [full(a) for a in ws],
        out_specs=pl.BlockSpec((1, m, LANES), lambda b, pt: (b, 0, 0)),
        scratch_shapes=[pltpu.VMEM((2, n_pages, KV_WIDTH, PAGE_SIZE), F32), pltpu.SemaphoreType.DMA((2,)),
                        pltpu.VMEM((m, CHUNK_LANES), BF16)])
    return pl.pallas_call(
        functools.partial(_paged_compress_body, n_pages=n_pages, norm=norm),
        grid_spec=grid_spec, out_shape=jax.ShapeDtypeStruct((nb, m, LANES), BF16),
        compiler_params=pltpu.CompilerParams(dimension_semantics=("arbitrary",), vmem_limit_bytes=VMEM_LIMIT),
        name="paged_compress_k" if norm else "paged_compress_v")(pt_flat, pool, *ws)


def _split3(x):
    hi = x.astype(BF16)
    r1 = x - hi.astype(F32)
    mid = r1.astype(BF16)
    lo = (r1 - mid.astype(F32)).astype(BF16)
    return hi, mid, lo


def _dot_nt(a, b):
    return lax.dot_general(a, b, (((1,), (1,)), ((), ())), preferred_element_type=F32)


def _topk_mask_t(score, forced, n_pick):
    n_blk = score.shape[0]
    jidx = lax.broadcasted_iota(jnp.int32, score.shape, 0).astype(F32)

    def body(_, carry):
        r, sel = carry
        m = jnp.max(r, axis=0, keepdims=True)
        first = jnp.min(jnp.where(r == m, jidx, float(n_blk)), axis=0, keepdims=True)
        pick = jidx == first
        return jnp.where(pick, -jnp.inf, r), jnp.where(pick, 1.0, sel)

    start = (jnp.where(forced, -jnp.inf, score), jnp.where(forced, 1.0, 0.0))
    _, sel = lax.fori_loop(0, n_pick, body, start, unroll=True)
    return sel


def _masked_softmax(s, mask, row_valid):
    s = jnp.where(mask, s, NEG)
    e = jnp.exp2(s - jnp.max(s, axis=-1, keepdims=True))
    return e * jnp.where(row_valid, 1.0 / jnp.sum(e, axis=-1, keepdims=True), 0.0)


def _gate_col(gt, head, branch):
    c = head * 3 + branch
    return gt[:, c:c + 1]


def _prompt_attn_body(qn_ref, qr_ref, gate_ref, kc_ref, vc_ref, kst_ref, vs_ref, kw_ref, vw_ref, ebt_ref, at_ref,
                      o_ref, m_ref, l_ref, acc_ref, *, seq, n_cmp):
    i = pl.program_id(1)
    s0 = i * Q_TILE
    nc_pad = kc_ref.shape[1]
    n_sel = seq // SEL_BLOCK
    lo = _low_half((Q_TILE, LANES))
    pos_col = s0 + lax.broadcasted_iota(jnp.int32, (Q_TILE, 1), 0)

    def head_rows(q_ref):
        parts = []
        for hd in range(N_HEADS):
            keep = lo if hd < GROUP else jnp.logical_not(lo)
            j = hd % GROUP
            parts.append(jnp.where(keep, q_ref[0, :, j * LANES:(j + 1) * LANES], 0))
        return jnp.concatenate(parts, axis=0)

    cidx = lax.broadcasted_iota(jnp.int32, (Q_TILE, nc_pad), 1)
    m_c = (cidx * CMP_STRIDE + (CMP_BLOCK - 1) <= pos_col) & (cidx < n_cmp)
    jrow = lax.broadcasted_iota(jnp.int32, (n_sel, Q_TILE), 0)
    pos_row = s0 + lax.broadcasted_iota(jnp.int32, (n_sel, Q_TILE), 1)
    cur = pos_row >> 6
    forced = (jrow == 0) | (jrow == cur) | (jrow == cur - 1)
    valid_b = jrow * SEL_BLOCK <= pos_row
    q_sel = head_rows(qr_ref)
    n_win = WINDOW + Q_TILE
    w0 = pl.multiple_of(jnp.maximum(s0 - WINDOW, 0), Q_TILE)
    dpos = pos_col - (w0 + lax.broadcasted_iota(jnp.int32, (Q_TILE, n_win), 1))
    m_w = (dpos >= 0) & (dpos < WINDOW)
    s_c = _dot_nt(head_rows(qn_ref), kc_ref[0]).reshape(N_HEADS, Q_TILE, nc_pad)
    s_w = _dot_nt(q_sel, kw_ref[0, pl.ds(w0, n_win), :]).reshape(N_HEADS, Q_TILE, n_win)
    p = _masked_softmax(s_c, m_c[None], ((pos_col >= CMP_BLOCK - 1) & (n_cmp > 0))[None])
    o_cmp = jnp.dot(p.reshape(N_HEADS * Q_TILE, nc_pad).astype(BF16), vc_ref[0], preferred_element_type=F32)
    imps = [jnp.sum(p[kv * GROUP:(kv + 1) * GROUP], axis=0) for kv in range(N_KV)]
    s_w = jnp.where(m_w[None], s_w, NEG)
    e_w = jnp.exp2(s_w - jnp.max(s_w, axis=-1, keepdims=True))
    o_win = (jnp.dot(e_w.reshape(N_HEADS * Q_TILE, n_win).astype(BF16), vw_ref[0, pl.ds(w0, n_win), :],
                     preferred_element_type=F32)
             * (1.0 / jnp.sum(e_w, axis=-1, keepdims=True)).reshape(N_HEADS * Q_TILE, 1))

    notsel = []
    for kv in range(N_KV):
        imp = imps[kv]
        blk_t = sum(_dot_nt(at_ref[...], part) for part in _split3(imp))
        sel_t = _topk_mask_t(jnp.where(valid_b, blk_t, -FORCED_BONUS), forced, min(N_SELECT, n_sel) - N_FORCED)
        notsel += [(1.0 - sel_t.T).astype(BF16)] * GROUP

    m_ref[...] = jnp.full(m_ref.shape, NEG, F32)
    l_ref[...] = jnp.zeros(l_ref.shape, F32)
    acc_ref[...] = jnp.zeros(acc_ref.shape, F32)
    q_aug = jnp.concatenate([q_sel, jnp.concatenate(notsel, axis=0)], axis=1)

    def tile_step(tix, lane0, width, diagonal):
        lanes = slice(lane0, lane0 + width)
        key0 = pl.multiple_of(tix * KEY_TILE + lane0, width)
        k_aug = jnp.concatenate([kst_ref[0, tix, :, lanes], ebt_ref[tix, :, lanes]], axis=0)
        vt = vs_ref[0, pl.ds(key0, width), :]
        if diagonal:
            key = key0 + lax.broadcasted_iota(jnp.int32, (Q_TILE, width), 1)
            future = jnp.where(key <= pos_col, 0.0, NEG)
        part = N_HEADS * Q_TILE // FLASH_SPLITS
        scores = [jnp.dot(q_aug[h * part:(h + 1) * part], k_aug, preferred_element_type=F32)
                  for h in range(FLASH_SPLITS)]
        for h in range(FLASH_SPLITS):
            rows = slice(h * part, (h + 1) * part)
            s = scores[h]
            if diagonal:
                s = (s.reshape(part // Q_TILE, Q_TILE, width) + future[None]).reshape(part, width)
            m_old = m_ref[rows]
            m_new = jnp.maximum(m_old, jnp.max(s, axis=-1, keepdims=True))
            alpha = jnp.exp2(m_old - m_new)
            p_t = jnp.exp2(s - m_new)
            l_ref[rows] = alpha * l_ref[rows] + jnp.sum(p_t, axis=-1, keepdims=True)
            acc_ref[rows] = alpha * acc_ref[rows] + jnp.dot(p_t.astype(BF16), vt, preferred_element_type=F32)
            m_ref[rows] = m_new

    def tile_body(tix, carry):
        tile_step(tix, 0, KEY_TILE, False)
        return carry

    q_per_key = KEY_TILE // Q_TILE
    last_tile = lax.shift_right_logical(i, q_per_key.bit_length() - 1)
    lax.fori_loop(0, last_tile, tile_body, 0)
    half = KEY_TILE // 2
    in_second_half = (i & (q_per_key - 1)) >= q_per_key // 2

    @pl.when(in_second_half)
    def _():
        tile_step(last_tile, 0, half, False)
        tile_step(last_tile, half, half, True)

    @pl.when(jnp.logical_not(in_second_half))
    def _():
        tile_step(last_tile, 0, half, True)

    o_sel = acc_ref[...] / l_ref[...]
    gt = jax.nn.sigmoid(gate_ref[0])
    o_heads = []
    for hd in range(N_HEADS):
        rows = slice(hd * Q_TILE, (hd + 1) * Q_TILE)
        o_heads.append(_gate_col(gt, hd, 0) * o_cmp[rows] + _gate_col(gt, hd, 1) * o_sel[rows]
                       + _gate_col(gt, hd, 2) * o_win[rows])
    for j in range(GROUP):
        o_ref[0, :, j * LANES:(j + 1) * LANES] = jnp.where(lo, o_heads[j], o_heads[GROUP + j]).astype(BF16)


def _prompt_attn(qn, qr, gate, kc, vc, kst, vs16, kw16, vw16, ebt, a_t, *, n_cmp):
    nb, seq, _ = qn.shape
    qspec = lambda w: pl.BlockSpec((1, Q_TILE, w), lambda b, i: (b, i, 0))
    seqspec = lambda a: pl.BlockSpec((1,) + a.shape[1:], lambda b, i: (b,) + (0,) * (a.ndim - 1))
    full = lambda a: pl.BlockSpec(a.shape, lambda b, i: (0,) * a.ndim)
    rows = N_HEADS * Q_TILE
    return pl.pallas_call(
        functools.partial(_prompt_attn_body, seq=seq, n_cmp=n_cmp),
        grid=(nb, seq // Q_TILE),
        in_specs=[qspec(B_WIDTH), qspec(B_WIDTH), qspec(LANES), seqspec(kc), seqspec(vc),
                  seqspec(kst), seqspec(vs16), seqspec(kw16), seqspec(vw16), full(ebt), full(a_t)],
        out_specs=qspec(B_WIDTH),
        out_shape=jax.ShapeDtypeStruct((nb, seq, B_WIDTH), BF16),
        scratch_shapes=[pltpu.VMEM((rows, 1), F32), pltpu.VMEM((rows, 1), F32), pltpu.VMEM((rows, LANES), F32)],
        compiler_params=pltpu.CompilerParams(dimension_semantics=("arbitrary", "arbitrary"),
                                             vmem_limit_bytes=VMEM_LIMIT),
        name="prompt_attn")(qn, qr, gate, kc, vc, kst, vs16, kw16, vw16, ebt, a_t)


TOK_PAD = 8
ROWS = N_HEADS * TOK_PAD
ROWS_PAD = 128


def _page_copy_lanes(pool_ref, page, buf_ref, slot, i, sem_ref):
    return pltpu.make_async_copy(pool_ref.at[page], buf_ref.at[slot, :, pl.ds(i * PAGE_SIZE, PAGE_SIZE)],
                                 sem_ref.at[slot])


def _sample_attn_body(pt_ref, qn_ref, qr_ref, gate_ref, kc_ref, vc_ref, kwc_ref, vwc_ref,
                      ksn_ref, vsn_ref, kwn_ref, vwn_ref, at_ref, gsum_ref, ebt_ref, kpool_ref, vpool_ref,
                      o_ref, kbuf_ref, vbuf_ref, ksem_ref, vsem_ref, notsel_ref, m_ref, l_ref, acc_ref, ocw_ref,
                      q_ref, *, n_chunks, pages_per_chunk, past, n_new, n_cmp):
    b = pl.program_id(0)
    c = pl.program_id(1)
    step = b * n_chunks + c
    n_steps = pl.num_programs(0) * n_chunks
    n_past_blk = past // SEL_BLOCK
    blk_per_chunk = n_past_blk // n_chunks

    def fetch(st, slot):
        def one(i, carry):
            page = pt_ref[st * pages_per_chunk + i]
            _page_copy_lanes(kpool_ref, page, kbuf_ref, slot, i, ksem_ref).start()
            _page_copy_lanes(vpool_ref, page, vbuf_ref, slot, i, vsem_ref).start()
            return carry
        lax.fori_loop(0, pages_per_chunk, one, 0, unroll=8)

    @pl.when(step == 0)
    def _():
        fetch(0, 0)

    @pl.when(step + 1 < n_steps)
    def _():
        fetch(step + 1, (step + 1) % 2)

    lo = _low_half((TOK_PAD, LANES))
    tok_col = lax.broadcasted_iota(jnp.int32, (ROWS, 1), 0) % TOK_PAD
    pos_col = past + tok_col

    def rows_q(ref):
        parts = []
        for hd in range(N_HEADS):
            keep = lo if hd < GROUP else jnp.logical_not(lo)
            j = hd % GROUP
            parts.append(jnp.where(keep, ref[0, :, j * LANES:(j + 1) * LANES], 0))
        return jnp.concatenate(parts, axis=0)

    def gate_rows(gt, branch):
        return jnp.concatenate([_gate_col(gt, hd, branch) for hd in range(N_HEADS)], axis=0)

    @pl.when(c == 0)
    def _():
        gt = jax.nn.sigmoid(gate_ref[0])
        nc_pad = kc_ref.shape[1]
        cidx = lax.broadcasted_iota(jnp.int32, (ROWS, nc_pad), 1)
        m_c = (cidx * CMP_STRIDE + (CMP_BLOCK - 1) <= pos_col) & (cidx < n_cmp)
        p = _masked_softmax(_dot_nt(rows_q(qn_ref), kc_ref[0]), m_c, (pos_col >= CMP_BLOCK - 1) & (n_cmp > 0))
        o_c = jnp.dot(p.astype(BF16), vc_ref[0], preferred_element_type=F32)
        imp = sum(jnp.dot(gsum_ref[...], part, preferred_element_type=F32) for part in _split3(p))
        blk_t = sum(_dot_nt(at_ref[...], part) for part in _split3(imp))
        jrow = lax.broadcasted_iota(jnp.int32, blk_t.shape, 0)
        n_top = min(N_SELECT, n_past_blk + 1)
        sel_t = _topk_mask_t(blk_t, (jrow == 0) | (jrow == n_past_blk - 1), n_top - N_FORCED)
        for cc in range(n_chunks):
            blk = sel_t[cc * blk_per_chunk:(cc + 1) * blk_per_chunk]
            notsel_ref[cc] = (1.0 - blk.T[:ROWS]).astype(BF16)
        q_r = rows_q(qr_ref)
        q_ref[...] = q_r
        n_buf = kwc_ref.shape[2]
        kidx = lax.broadcasted_iota(jnp.int32, (ROWS, n_buf), 1)
        dpos = pos_col - (past - n_buf + kidx)
        m_old = (dpos >= 0) & (dpos < WINDOW)
        knew = lax.broadcasted_iota(jnp.int32, (ROWS, TOK_PAD), 1)
        m_new = (knew <= tok_col) & (knew < n_new)
        s_old = jnp.where(m_old, jnp.dot(q_r, kwc_ref[0].astype(BF16), preferred_element_type=F32), NEG)
        s_new = jnp.where(m_new, _dot_nt(q_r, kwn_ref[0]), NEG)
        mx = jnp.maximum(jnp.max(s_old, axis=-1, keepdims=True), jnp.max(s_new, axis=-1, keepdims=True))
        e_old = jnp.exp2(s_old - mx)
        e_new = jnp.exp2(s_new - mx)
        den = jnp.sum(e_old, axis=-1, keepdims=True) + jnp.sum(e_new, axis=-1, keepdims=True)
        p_old = e_old / den * m_old.astype(F32)
        p_new = e_new / den * m_new.astype(F32)
        o_w = (_dot_nt(p_old.astype(BF16), vwc_ref[0].astype(BF16))
               + jnp.dot(p_new.astype(BF16), vwn_ref[0], preferred_element_type=F32))
        ocw_ref[...] = gate_rows(gt, 0) * o_c + gate_rows(gt, 2) * o_w
        s = jnp.where(m_new, _dot_nt(q_r, ksn_ref[0]), NEG)
        m0 = jnp.max(s, axis=-1, keepdims=True)
        e = jnp.exp2(s - m0) * m_new.astype(F32)
        m_ref[...] = m0
        l_ref[...] = jnp.sum(e, axis=-1, keepdims=True)
        acc_ref[...] = jnp.dot(e.astype(BF16), vsn_ref[0], preferred_element_type=F32)

    slot = step % 2

    pltpu.make_async_copy(kbuf_ref.at[slot], kbuf_ref.at[slot], ksem_ref.at[slot]).wait()
    pltpu.make_async_copy(vbuf_ref.at[slot], vbuf_ref.at[slot], vsem_ref.at[slot]).wait()

    q_aug = jnp.concatenate([q_ref[...], notsel_ref[c]], axis=1)
    k_aug = jnp.concatenate([kbuf_ref[slot].astype(BF16), ebt_ref[...]], axis=0)
    s = jnp.dot(q_aug, k_aug, preferred_element_type=F32)
    m_old = m_ref[...]
    m_new = jnp.maximum(m_old, jnp.max(s, axis=-1, keepdims=True))
    alpha = jnp.exp2(m_old - m_new)
    p = jnp.exp2(s - m_new)
    l_ref[...] = alpha * l_ref[...] + jnp.sum(p, axis=-1, keepdims=True)
    acc_ref[...] = alpha * acc_ref[...] + _dot_nt(p.astype(BF16), vbuf_ref[slot].astype(BF16))
    m_ref[...] = m_new

    @pl.when(c == n_chunks - 1)
    def _():
        gt = jax.nn.sigmoid(gate_ref[0])
        o = ocw_ref[...] + gate_rows(gt, 1) * (acc_ref[...] / l_ref[...])
        for j in range(GROUP):
            top = o[j * TOK_PAD:(j + 1) * TOK_PAD]
            bot = o[(GROUP + j) * TOK_PAD:(GROUP + j + 1) * TOK_PAD]
            o_ref[0, :, j * LANES:(j + 1) * LANES] = jnp.where(lo, top, bot).astype(BF16)


def _sample_attn(pt_flat, qn, qr, gate, kc, vc, kwc, vwc, ksn, vsn, kwn, vwn, a_t, gsum, ebt, kpool, vpool,
                 *, n_chunks, pages_per_chunk, past, n_new, n_cmp):
    nb = qn.shape[0]
    chunk_keys = pages_per_chunk * PAGE_SIZE
    blk_per_chunk = chunk_keys // SEL_BLOCK
    bspec = lambda a: pl.BlockSpec((1,) + a.shape[1:], lambda b, c, pt: (b, 0, 0))
    full = lambda a: pl.BlockSpec(a.shape, lambda b, c, pt: (0,) * a.ndim)
    anyspec = pl.BlockSpec(memory_space=pl.ANY)
    blocked = [qn, qr, gate, kc, vc, kwc, vwc, ksn, vsn, kwn, vwn]
    grid_spec = pltpu.PrefetchScalarGridSpec(
        num_scalar_prefetch=1, grid=(nb, n_chunks),
        in_specs=[bspec(a) for a in blocked] + [full(a_t), full(gsum), full(ebt), anyspec, anyspec],
        out_specs=pl.BlockSpec((1, TOK_PAD, B_WIDTH), lambda b, c, pt: (b, 0, 0)),
        scratch_shapes=[pltpu.VMEM((2, KV_WIDTH, chunk_keys), F32), pltpu.VMEM((2, KV_WIDTH, chunk_keys), F32),
                        pltpu.SemaphoreType.DMA((2,)), pltpu.SemaphoreType.DMA((2,)),
                        pltpu.VMEM((n_chunks, ROWS, blk_per_chunk), BF16),
                        pltpu.VMEM((ROWS, 1), F32), pltpu.VMEM((ROWS, 1), F32), pltpu.VMEM((ROWS, LANES), F32),
                        pltpu.VMEM((ROWS, LANES), F32), pltpu.VMEM((ROWS, LANES), BF16)])
    return pl.pallas_call(
        functools.partial(_sample_attn_body, n_chunks=n_chunks, pages_per_chunk=pages_per_chunk, past=past,
                          n_new=n_new, n_cmp=n_cmp),
        grid_spec=grid_spec, out_shape=jax.ShapeDtypeStruct((nb, TOK_PAD, B_WIDTH), BF16),
        compiler_params=pltpu.CompilerParams(dimension_semantics=("arbitrary", "arbitrary"),
                                             vmem_limit_bytes=VMEM_LIMIT),
        name="sample_attn")(pt_flat, *blocked, a_t, gsum, ebt, kpool, vpool)


FF_TILE = 1024


def _merge_ffn_body(x_ref, oa_ref, ob_ref, g1_ref, wmg_ref, wba_ref, wbb_ref, wout_ref, g2_ref, wup_ref, wdn_ref,
                    y_ref):
    x = x_ref[...]
    h = _rms_rows(x, g1_ref[...]).astype(BF16)
    y_a = jnp.dot(oa_ref[...], wba_ref[...], preferred_element_type=F32)
    y_b = jnp.dot(ob_ref[...], wbb_ref[...], preferred_element_type=F32)
    g_a = jax.nn.sigmoid(jnp.dot(h, wmg_ref[:, :D_MODEL], preferred_element_type=F32))
    g_b = jax.nn.sigmoid(jnp.dot(h, wmg_ref[:, D_MODEL:], preferred_element_type=F32))
    x1 = x + jnp.dot((g_a * y_a + g_b * y_b).astype(BF16), wout_ref[...], preferred_element_type=F32)
    h2 = _rms_rows(x1, g2_ref[...]).astype(BF16)
    y = x1
    for f in range(D_FF // FF_TILE):
        up = jnp.dot(h2, wup_ref[:, f * FF_TILE:(f + 1) * FF_TILE], preferred_element_type=F32)
        act = jnp.square(jnp.maximum(up, 0.0)).astype(BF16)
        y = y + jnp.dot(act, wdn_ref[f * FF_TILE:(f + 1) * FF_TILE, :], preferred_element_type=F32)
    y_ref[...] = y


def _merge_ffn(x, o_a, o_b, wts, *, tm):
    n = x.shape[0]
    row = lambda w: pl.BlockSpec((tm, w), lambda i: (i, 0))
    const = lambda a: pl.BlockSpec(a.shape, lambda i: (0,) * a.ndim, pipeline_mode=pl.Buffered(1))
    ws = [wts['g1'], wts['w_mg'], wts['w_br_a'], wts['w_br_b'], wts['w_out'], wts['g2'], wts['w_up'], wts['w_down']]
    return pl.pallas_call(
        _merge_ffn_body, grid=(n // tm,),
        in_specs=[row(D_MODEL), row(A_WIDTH), row(B_WIDTH)] + [const(a) for a in ws],
        out_specs=row(D_MODEL), out_shape=jax.ShapeDtypeStruct((n, D_MODEL), F32),
        compiler_params=pltpu.CompilerParams(dimension_semantics=("arbitrary",), vmem_limit_bytes=VMEM_LIMIT),
        name="merge_ffn")(x, o_a, o_b, *ws)


def _rope_tables(pos):
    half = ROT_DIM // 2
    lane = jnp.arange(LANES) % HEAD_DIM
    inv = ROPE_THETA ** (-(2 * (lane % half)).astype(F32) / ROT_DIM)
    ang = pos.astype(F32)[:, None] * inv[None, :]
    cos, sin = jnp.cos(ang), jnp.sin(ang)
    cos_t = jnp.where(lane < ROT_DIM, cos, 1.0)
    sin_up = jnp.where((lane >= half) & (lane < ROT_DIM), sin, 0.0)
    sin_dn = jnp.where(lane < half, -sin, 0.0)
    return cos_t, sin_up, sin_dn


def _two(g):
    return jnp.concatenate([g, g])[None, :].astype(F32)


def _layer_weights(l, g_norm1, w_in, ln_v_g, ln_v_b, g_q, g_ks, g_kw, w_branch, w_out, g_norm2, w_up, w_down):
    w = w_in[l]
    q0, k0 = 2 * A_WIDTH, 2 * A_WIDTH + B_WIDTH
    g0 = k0 + 6 * KV_WIDTH
    perm = jnp.array(HEAD_PERM)
    q_cols = w[:, q0:k0].reshape(D_MODEL, N_HEADS, HEAD_DIM)[:, perm].reshape(D_MODEL, B_WIDTH)
    gate_cols = jnp.pad(w[:, g0:g0 + 3 * N_HEADS], ((0, 0), (0, LANES - 3 * N_HEADS)))
    w_a = jnp.concatenate([w[:, :q0], q_cols, w[:, k0:g0], gate_cols], axis=1).astype(BF16)
    wb = w_branch[l]
    w_br_b = wb[A_WIDTH:].reshape(N_HEADS, HEAD_DIM, D_MODEL)[perm].reshape(B_WIDTH, D_MODEL)
    return dict(
        g1=g_norm1[l][None, :], w_a=w_a, ln_g=ln_v_g[l][None, :], ln_b=ln_v_b[l][None, :],
        g_q=_two(g_q[l]), g_ks=_two(g_ks[l]), g_kw=_two(g_kw[l]),
        w_mg=w[:, g0 + 3 * N_HEADS:].astype(BF16), w_br_a=wb[:A_WIDTH].astype(BF16), w_br_b=w_br_b.astype(BF16),
        w_out=w_out[l].astype(BF16), g2=g_norm2[l][None, :], w_up=w_up[l].astype(BF16),
        w_down=w_down[l].astype(BF16))


def _compress_weights(w1, w2, pe, g):
    half = CMP_STRIDE * HEAD_DIM
    eye = jnp.eye(N_KV, dtype=F32)

    def spread(wh):
        return jnp.einsum('rde,kl->rkdle', wh.reshape(CMP_STRIDE, HEAD_DIM, HEAD_DIM), eye).reshape(
            CHUNK_LANES, KV_WIDTH)

    w1ab = jnp.concatenate([spread(w1[:half]), spread(w1[half:])], axis=1).astype(BF16)
    pe_rows = jnp.zeros((8, CMP_BLOCK * HEAD_DIM), F32).at[0].set(pe.reshape(-1)).astype(BF16)
    w1d = jnp.concatenate([w1, w1], axis=1).astype(BF16)
    w2d = jnp.einsum('de,kl->kdle', w2, eye).reshape(KV_WIDTH, KV_WIDTH).astype(BF16)
    return dict(w1ab=w1ab, pe=pe_rows, w1d=w1d, w2d=w2d, g=_two(g))


def _block_score_matrix(n_blk, n_chunk):
    j = jnp.arange(n_blk)[:, None]
    c = jnp.arange(n_chunk)[None, :]
    per = SEL_BLOCK // CMP_STRIDE
    a = ((c >= per * j) & (c <= per * j + per - 1)).astype(F32) + ((c >= per * j - 1) & (c <= per * j + per - 2))
    return a.astype(BF16)


def kernel(x_prompt, x_sample, cache_k_cmp, cache_v_cmp, cache_k_sel, cache_v_sel, cache_k_win, cache_v_win,
           page_table, g_norm1, w_in, ln_v_g, ln_v_b, w_s, b_s, g_q, g_kc, g_ks, g_kw, w_ck1, w_ck2, pe_k,
           w_cv1, w_cv2, pe_v, w_branch, w_out, g_norm2, w_up, w_down):
    nb, seq, _ = x_prompt.shape
    db, dseq, _ = x_sample.shape
    depth = w_in.shape[0]
    n_pages = page_table.shape[1]
    past = n_pages * PAGE_SIZE
    assert depth == 1 and seq % KEY_TILE == 0 and seq >= WINDOW + Q_TILE and dseq <= TOK_PAD
    assert past % (SEL_BLOCK * LANES) == 0 and (past + dseq) // CMP_STRIDE == past // CMP_STRIDE
    l = 0
    wts = _layer_weights(l, g_norm1, w_in, ln_v_g, ln_v_b, g_q, g_ks, g_kw, w_branch, w_out, g_norm2, w_up, w_down)
    cw_k = _compress_weights(w_ck1[l], w_ck2[l], pe_k[l], g_kc[l])
    cw_v = _compress_weights(w_cv1[l], w_cv2[l], pe_v[l], g_kc[l])
    causal = jnp.tril(jnp.ones((CHUNK, CHUNK), dtype=bool))
    ws_tril = jnp.where(causal[None], w_s[l], 0.0)

    n_p = nb * seq
    xp = x_prompt.reshape(n_p, D_MODEL)
    tabs_p = _rope_tables(jnp.arange(seq, dtype=jnp.int32))
    bmix_p = jnp.repeat(b_s[l].T, A_WIDTH // A_GROUPS, axis=1)
    (oa_p, vlast_p, qn_p, qr_p, kc_p, vc_p, ks_p, vs_p, kw_p, vw_p, kc16, vc16, ks16, vs16, kw16, vw16,
     gate_p) = _in_proj(
        xp, wts, tabs_p, ws_tril.astype(BF16), bmix_p, tm=512, seg_rows=seq)
    n_chunk_p = seq // CMP_STRIDE
    kcmp_p = _compress(kc16.reshape(nb, n_chunk_p, CHUNK_LANES), cw_k, norm=True)
    vcmp_p = _compress(vc16.reshape(nb, n_chunk_p, CHUNK_LANES), cw_v, norm=False)
    n_sel_p = seq // SEL_BLOCK
    n_tiles = seq // KEY_TILE
    kblk = (jnp.arange(seq) // SEL_BLOCK).reshape(n_tiles, 1, KEY_TILE)
    ebt = jnp.where(kblk == jnp.arange(n_sel_p)[None, :, None], MASK_BIAS, 0.0).astype(BF16)
    a_t_p = _block_score_matrix(n_sel_p, n_chunk_p)
    b3 = lambda a: a.reshape(nb, seq, a.shape[-1])
    kst = ks16.reshape(nb, n_tiles, KEY_TILE, KV_WIDTH).transpose(0, 1, 3, 2)
    ob_p = _prompt_attn(b3(qn_p), b3(qr_p), b3(gate_p), kcmp_p, vcmp_p, kst, b3(vs16), b3(kw16), b3(vw16),
                        ebt, a_t_p, n_cmp=n_chunk_p - 1)
    y_p = _merge_ffn(xp, oa_p, ob_p.reshape(n_p, B_WIDTH), wts, tm=512).reshape(nb, seq, D_MODEL)

    n_s = db * dseq
    xs = x_sample.reshape(n_s, D_MODEL)
    tabs_s = _rope_tables(jnp.tile(past + jnp.arange(dseq, dtype=jnp.int32), db))
    wmix_s = jnp.einsum('gpr,bc->gbpcr', ws_tril[:, :dseq, :dseq], jnp.eye(db, dtype=F32)).reshape(
        A_GROUPS, n_s, n_s)
    bmix_s = jnp.tile(bmix_p[:dseq], (db, 1))
    (oa_s, v_s, qn_s, qr_s, kc_s, vc_s, ks_s, vs_s, kw_s, vw_s, _, _, ks16s, vs16s, kw16s, vw16s,
     gate_s) = _in_proj(
        xs, wts, tabs_s, wmix_s.astype(BF16), bmix_s, tm=n_s, seg_rows=n_s)
    pt_flat = page_table.reshape(-1)
    keys_last = lambda a: a.transpose(0, 2, 3, 1).reshape(a.shape[0], KV_WIDTH, a.shape[1])
    kcmp_s = _paged_compress(pt_flat, keys_last(cache_k_cmp[l]), cw_k, nb=db, n_pages=n_pages, norm=True)
    vcmp_s = _paged_compress(pt_flat, keys_last(cache_v_cmp[l]), cw_v, nb=db, n_pages=n_pages, norm=False)
    n_chunk_s = past // CMP_STRIDE
    tokpad = lambda a: jnp.pad(a.reshape(db, dseq, a.shape[-1]), ((0, 0), (0, TOK_PAD - dseq), (0, 0)))
    hh = jnp.arange(ROWS_PAD) // TOK_PAD
    tt = jnp.arange(ROWS_PAD) % TOK_PAD
    gsum = ((hh[:, None] // GROUP == hh[None, :ROWS] // GROUP) & (tt[:, None] == tt[None, :ROWS])
            & (hh[:, None] < N_HEADS)).astype(BF16)
    n_chunks = 2
    chunk_keys = past // n_chunks
    ebt = jnp.where(jnp.arange(chunk_keys)[None, :] // SEL_BLOCK == jnp.arange(chunk_keys // SEL_BLOCK)[:, None],
                    MASK_BIAS, 0.0).astype(BF16)
    ob_s = _sample_attn(
        pt_flat, tokpad(qn_s), tokpad(qr_s), tokpad(gate_s), kcmp_s, vcmp_s,
        keys_last(cache_k_win[l]), keys_last(cache_v_win[l]),
        tokpad(ks16s), tokpad(vs16s), tokpad(kw16s), tokpad(vw16s),
        _block_score_matrix(past // SEL_BLOCK, n_chunk_s), gsum, ebt,
        keys_last(cache_k_sel[l]), keys_last(cache_v_sel[l]),
        n_chunks=n_chunks, pages_per_chunk=n_pages // n_chunks, past=past, n_new=dseq, n_cmp=n_chunk_s - 1)
    y_s = _merge_ffn(xs, oa_s, ob_s[:, :dseq].reshape(n_s, B_WIDTH), wts, tm=n_s).reshape(db, dseq, D_MODEL)

    w_p = min(WINDOW, seq)
    pk = lambda a: a.reshape(1, nb, N_KV, HEAD_DIM, seq).transpose(0, 1, 4, 2, 3)
    sk = lambda a: a[0].T.reshape(1, db, dseq, N_KV, HEAD_DIM)
    return (y_p, y_s, pk(kc_p), pk(vc_p), pk(ks_p), pk(vs_p), pk(kw_p)[:, :, seq - w_p:], pk(vw_p)[:, :, seq - w_p:],
            vlast_p[None], sk(kc_s), sk(vc_s), sk(ks_s), sk(vs_s), sk(kw_s), sk(vw_s),
            v_s.reshape(db, dseq, A_WIDTH)[None])
```

```python
import functools

import jax
import jax.numpy as jnp
from jax import lax
from jax.experimental import pallas as pl
from jax.experimental.pallas import tpu as pltpu

F32 = jnp.float32
BF16 = jnp.bfloat16

D_MODEL = 1024
A_WIDTH = 512
A_GROUPS = 8
CHUNK = 128
N_HEADS = 8
HEAD_DIM = 64
N_KV = 2
GROUP = N_HEADS // N_KV
B_WIDTH = N_HEADS * HEAD_DIM
KV_WIDTH = N_KV * HEAD_DIM
CMP_BLOCK = 32
CMP_STRIDE = 16
SEL_BLOCK = 64
N_SELECT = 16
WINDOW = 512
ROT_DIM = 16
ROPE_THETA = 500000.0
D_FF = 4096
PAGE_SIZE = 128
EPS = 1e-6
NEG = -1e30
FORCED_BONUS = 1e4
N_FORCED = 3
MASK_BIAS = -(2.0 ** 100)
SCALE = HEAD_DIM ** -0.5
Q_SCALE = SCALE * 1.4426950408889634

LANES = 128
Q_TILE = 128
KEY_TILE = 2048
FLASH_SPLITS = 4
CHUNK_LANES = CMP_STRIDE * KV_WIDTH
HEAD_PERM = (0, 4, 1, 5, 2, 6, 3, 7)
VMEM_LIMIT = 56 * 1024 * 1024


def _low_half(shape):
    return lax.broadcasted_iota(jnp.int32, shape, len(shape) - 1) < HEAD_DIM


def _head_rms(xs, g):
    lo = _low_half(xs.shape)
    sq = xs * xs
    sa = jnp.sum(jnp.where(lo, sq, 0.0), axis=-1, keepdims=True)
    sb = jnp.sum(jnp.where(lo, 0.0, sq), axis=-1, keepdims=True)
    ms = jnp.where(lo, sa, sb) * (1.0 / HEAD_DIM)
    return xs * lax.rsqrt(ms + EPS) * g


def _rope(xs, cos, sin_up, sin_dn):
    return xs * cos + pltpu.roll(xs, ROT_DIM // 2, 1) * sin_up + pltpu.roll(xs, LANES - ROT_DIM // 2, 1) * sin_dn


def _rms_rows(x, g):
    return x * lax.rsqrt(jnp.mean(x * x, axis=-1, keepdims=True) + EPS) * g


def _in_proj_body(x_ref, g1_ref, w_ref, lng_ref, lnb_ref, gq_ref, gks_ref, gkw_ref,
                  cos_ref, sup_ref, sdn_ref, wmix_ref, bmix_ref,
                  oa_ref, vlast_ref, qn_ref, qr_ref, kc_ref, vc_ref, ks_ref, vs_ref, kw_ref, vw_ref,
                  kc16_ref, vc16_ref, ks16_ref, vs16_ref, kw16_ref, vw16_ref, gate_ref, *, tm, tiles_per_seg):
    i = pl.program_id(0)
    h = _rms_rows(x_ref[...], g1_ref[...]).astype(BF16)

    def proj(lo, hi):
        return jnp.dot(h, w_ref[:, lo:hi], preferred_element_type=F32)

    u = jax.nn.gelu(proj(0, A_WIDTH))
    v = jax.nn.gelu(proj(A_WIDTH, 2 * A_WIDTH))
    vc = v - jnp.mean(v, axis=-1, keepdims=True)
    vn = vc * lax.rsqrt(jnp.mean(vc * vc, axis=-1, keepdims=True) + EPS) * lng_ref[...] + lnb_ref[...]

    @pl.when(i % tiles_per_seg == tiles_per_seg - 1)
    def _():
        vlast_ref[0] = vn[tm - CHUNK:tm]

    lo = _low_half((CHUNK, LANES))
    for c in range(tm // CHUNK):
        r0, r1 = c * CHUNK, (c + 1) * CHUNK
        for j in range(A_WIDTH // LANES):
            slab = vn[r0:r1, j * LANES:(j + 1) * LANES]
            a = jnp.where(lo, slab, 0.0).astype(BF16)
            b = jnp.where(lo, 0.0, slab).astype(BF16)
            s = (jnp.dot(wmix_ref[2 * j], a, preferred_element_type=F32)
                 + jnp.dot(wmix_ref[2 * j + 1], b, preferred_element_type=F32)
                 + bmix_ref[:, j * LANES:(j + 1) * LANES])
            oa_ref[r0:r1, j * LANES:(j + 1) * LANES] = (u[r0:r1, j * LANES:(j + 1) * LANES] * s).astype(BF16)

    cos, sup, sdn = cos_ref[...], sup_ref[...], sdn_ref[...]
    q0 = 2 * A_WIDTH
    for j in range(B_WIDTH // LANES):
        qn = _head_rms(proj(q0 + j * LANES, q0 + (j + 1) * LANES), gq_ref[...])
        qn_ref[:, j * LANES:(j + 1) * LANES] = (qn * Q_SCALE).astype(BF16)
        qr_ref[:, j * LANES:(j + 1) * LANES] = (_rope(qn, cos, sup, sdn) * Q_SCALE).astype(BF16)
    k0 = q0 + B_WIDTH

    def emit(val, state_ref, bf16_ref):
        state_ref[0] = val.T
        bf16_ref[...] = val.astype(BF16)

    emit(proj(k0, k0 + LANES), kc_ref, kc16_ref)
    emit(proj(k0 + LANES, k0 + 2 * LANES), vc_ref, vc16_ref)
    emit(_rope(_head_rms(proj(k0 + 2 * LANES, k0 + 3 * LANES), gks_ref[...]), cos, sup, sdn), ks_ref, ks16_ref)
    emit(proj(k0 + 3 * LANES, k0 + 4 * LANES), vs_ref, vs16_ref)
    emit(_rope(_head_rms(proj(k0 + 4 * LANES, k0 + 5 * LANES), gkw_ref[...]), cos, sup, sdn), kw_ref, kw16_ref)
    emit(proj(k0 + 5 * LANES, k0 + 6 * LANES), vw_ref, vw16_ref)
    gate_ref[...] = proj(k0 + 6 * LANES, k0 + 7 * LANES)


def _in_proj(x, wts, tabs, wmix, bmix, *, tm, seg_rows):
    n = x.shape[0]
    tiles_per_seg = seg_rows // tm
    n_seg = n // seg_rows
    row = lambda w: pl.BlockSpec((tm, w), lambda i: (i, 0))
    full = lambda a: pl.BlockSpec(a.shape, lambda i: (0,) * a.ndim)
    cos, sup, sdn = tabs
    ins = [x, wts['g1'], wts['w_a'], wts['ln_g'], wts['ln_b'], wts['g_q'], wts['g_ks'], wts['g_kw'],
           cos, sup, sdn, wmix, bmix]
    tab = pl.BlockSpec((tm, LANES), lambda i: (i % tiles_per_seg, 0))
    in_specs = [row(D_MODEL)] + [full(a) for a in ins[1:8]] + [tab] * 3 + [full(wmix), full(bmix)]
    f32o = lambda w: jax.ShapeDtypeStruct((n, w), F32)
    b16o = lambda w: jax.ShapeDtypeStruct((n, w), BF16)
    out_shape = ([b16o(A_WIDTH), jax.ShapeDtypeStruct((n_seg, CHUNK, A_WIDTH), F32), b16o(B_WIDTH), b16o(B_WIDTH)]
                 + [jax.ShapeDtypeStruct((n_seg, LANES, seg_rows), F32)] * 6 + [b16o(LANES)] * 6 + [f32o(LANES)])
    state = pl.BlockSpec((1, LANES, tm), lambda i: (i // tiles_per_seg, 0, i % tiles_per_seg))
    out_specs = ([row(A_WIDTH), pl.BlockSpec((1, CHUNK, A_WIDTH), lambda i: (i // tiles_per_seg, 0, 0)),
                  row(B_WIDTH), row(B_WIDTH)] + [state] * 6 + [row(LANES)] * 7)
    return pl.pallas_call(
        functools.partial(_in_proj_body, tm=tm, tiles_per_seg=tiles_per_seg),
        grid=(n // tm,), in_specs=in_specs, out_specs=out_specs, out_shape=out_shape,
        compiler_params=pltpu.CompilerParams(dimension_semantics=("arbitrary",), vmem_limit_bytes=VMEM_LIMIT),
        name="in_proj")(*ins)


COMPRESS_ROWS = 1024


def _compress_math(load_rows, m, w1ab_ref, pe_ref, w1d_ref, w2d_ref, g_ref, norm):
    step = min(COMPRESS_ROWS, m)
    f = jnp.concatenate(
        [jnp.dot(load_rows(r, r + step).astype(BF16), w1ab_ref[...], preferred_element_type=F32)
         for r in range(0, m, step)], axis=0)
    bias = jnp.dot(pe_ref[...], w1d_ref[...], preferred_element_type=F32)[0:1]
    second_next = pltpu.roll(f[:, LANES:], m - 1, 0)
    act = jax.nn.gelu(f[:, :LANES] + second_next + bias)
    out = jnp.dot(act.astype(BF16), w2d_ref[...], preferred_element_type=F32)
    if norm:
        out = _head_rms(out, g_ref[...])
    return out.astype(BF16)


def _compress_body(ch_ref, w1ab_ref, pe_ref, w1d_ref, w2d_ref, g_ref, out_ref, *, norm):
    out_ref[0] = _compress_math(lambda r0, r1: ch_ref[0, r0:r1], ch_ref.shape[1],
                                w1ab_ref, pe_ref, w1d_ref, w2d_ref, g_ref, norm)


def _compress(ch, cw, *, norm):
    nb, m, _ = ch.shape
    full = lambda a: pl.BlockSpec(a.shape, lambda b: (0,) * a.ndim)
    ws = [cw['w1ab'], cw['pe'], cw['w1d'], cw['w2d'], cw['g']]
    return pl.pallas_call(
        functools.partial(_compress_body, norm=norm),
        grid=(nb,),
        in_specs=[pl.BlockSpec((1, m, CHUNK_LANES), lambda b: (b, 0, 0))] + [full(a) for a in ws],
        out_specs=pl.BlockSpec((1, m, LANES), lambda b: (b, 0, 0)),
        out_shape=jax.ShapeDtypeStruct((nb, m, LANES), BF16),
        compiler_params=pltpu.CompilerParams(dimension_semantics=("arbitrary",), vmem_limit_bytes=VMEM_LIMIT),
        name="compress_k" if norm else "compress_v")(ch, *ws)


def _page_copy(pool_ref, page, buf_ref, slot, i, sem_ref):
    return pltpu.make_async_copy(pool_ref.at[page], buf_ref.at[slot, i], sem_ref.at[slot])


def _paged_compress_body(pt_ref, pool_ref, perm_ref, w1ab_ref, pe_ref, w1d_ref, w2d_ref, g_ref, out_ref,
                         buf_ref, sem_ref, ch_ref, *, n_pages, norm):
    b = pl.program_id(0)

    def fetch(bb, slot):
        def one(i, carry):
            _page_copy(pool_ref, pt_ref[bb * n_pages + i], buf_ref, slot, i, sem_ref).start()
            return carry
        lax.fori_loop(0, n_pages, one, 0, unroll=8)

    @pl.when(b == 0)
    def _():
        fetch(0, 0)

    @pl.when(b + 1 < pl.num_programs(0))
    def _():
        fetch(b + 1, (b + 1) % 2)

    slot = b % 2

    pltpu.make_async_copy(pool_ref.at[pl.ds(0, n_pages)], buf_ref.at[slot], sem_ref.at[slot]).wait()

    chunks = PAGE_SIZE // CMP_STRIDE

    def regroup(q, carry):
        xt = jnp.concatenate([buf_ref[slot, 2 * q], buf_ref[slot, 2 * q + 1]], axis=1).astype(BF16)
        y = jnp.dot(xt, perm_ref[...], preferred_element_type=F32)
        t0, t1 = y[:, :PAGE_SIZE].T, y[:, PAGE_SIZE:].T
        rows = pl.ds(pl.multiple_of(q * 2 * chunks, 2 * chunks), 2 * chunks)
        for r in range(CMP_STRIDE):
            piece = jnp.concatenate([t0[r * chunks:(r + 1) * chunks], t1[r * chunks:(r + 1) * chunks]], axis=0)
            ch_ref[rows, r * KV_WIDTH:(r + 1) * KV_WIDTH] = piece.astype(BF16)
        return carry
    lax.fori_loop(0, n_pages // 2, regroup, 0, unroll=32)
    m = n_pages * chunks
    out_ref[0] = _compress_math(lambda r0, r1: ch_ref[r0:r1], m, w1ab_ref, pe_ref, w1d_ref, w2d_ref, g_ref, norm)


def _paged_compress(pt_flat, pool, cw, *, nb, n_pages, norm):
    m = n_pages * (PAGE_SIZE // CMP_STRIDE)
    full = lambda a: pl.BlockSpec(a.shape, lambda b, pt: (0,) * a.ndim)
    lane = jnp.arange(2 * PAGE_SIZE)
    page, chunk, offset = lane // PAGE_SIZE, (lane % PAGE_SIZE) // CMP_STRIDE, lane % CMP_STRIDE
    dest = page * PAGE_SIZE + offset * (PAGE_SIZE // CMP_STRIDE) + chunk
    perm = (dest[:, None] == lane[None, :]).astype(BF16)
    ws = [perm, cw['w1ab'], cw['pe'], cw['w1d'], cw['w2d'], cw['g']]
    grid_spec = pltpu.PrefetchScalarGridSpec(
        num_scalar_prefetch=1, grid=(nb,),
        in_specs=[pl.BlockSpec(memory_space=pl.ANY)] + [full(a) for a in ws],
        out_specs=pl.BlockSpec((1, m, LANES), lambda b, pt: (b, 0, 0)),
        scratch_shapes=[pltpu.VMEM((2, n_pages, KV_WIDTH, PAGE_SIZE), F32), pltpu.SemaphoreType.DMA((2,)),
                        pltpu.VMEM((m, CHUNK_LANES), BF16)])
    return pl.pallas_call(
        functools.partial(_paged_compress_body, n_pages=n_pages, norm=norm),
        grid_spec=grid_spec, out_shape=jax.ShapeDtypeStruct((nb, m, LANES), BF16),
        compiler_params=pltpu.CompilerParams(dimension_semantics=("arbitrary",), vmem_limit_bytes=VMEM_LIMIT),
        name="paged_compress_k" if norm else "paged_compress_v")(pt_flat, pool, *ws)


def _split3(x):
    hi = x.astype(BF16)
    r1 = x - hi.astype(F32)
    mid = r1.astype(BF16)
    lo = (r1 - mid.astype(F32)).astype(BF16)
    return hi, mid, lo


def _dot_nt(a, b):
    return lax.dot_general(a, b, (((1,), (1,)), ((), ())), preferred_element_type=F32)


def _topk_mask_t(score, forced, n_pick):
    n_blk = score.shape[0]
    jidx = lax.broadcasted_iota(jnp.int32, score.shape, 0).astype(F32)

    def body(_, carry):
        r, sel = carry
        m = jnp.max(r, axis=0, keepdims=True)
        first = jnp.min(jnp.where(r == m, jidx, float(n_blk)), axis=0, keepdims=True)
        pick = jidx == first
        return jnp.where(pick, -jnp.inf, r), jnp.where(pick, 1.0, sel)

    start = (jnp.where(forced, -jnp.inf, score), jnp.where(forced, 1.0, 0.0))
    _, sel = lax.fori_loop(0, n_pick, body, start, unroll=True)
    return sel


def _masked_softmax(s, mask, row_valid):
    s = jnp.where(mask, s, NEG)
    e = jnp.exp2(s - jnp.max(s, axis=-1, keepdims=True))
    return e * jnp.where(row_valid, 1.0 / jnp.sum(e, axis=-1, keepdims=True), 0.0)


def _gate_col(gt, head, branch):
    c = head * 3 + branch
    return gt[:, c:c + 1]


def _prompt_attn_body(qn_ref, qr_ref, gate_ref, kc_ref, vc_ref, kst_ref, vs_ref, kw_ref, vw_ref, ebt_ref, at_ref,
                      o_ref, m_ref, l_ref, acc_ref, *, seq, n_cmp):
    i = pl.program_id(1)
    s0 = i * Q_TILE
    nc_pad = kc_ref.shape[1]
    n_sel = seq // SEL_BLOCK
    lo = _low_half((Q_TILE, LANES))
    pos_col = s0 + lax.broadcasted_iota(jnp.int32, (Q_TILE, 1), 0)

    def head_rows(q_ref):
        parts = []
        for hd in range(N_HEADS):
            keep = lo if hd < GROUP else jnp.logical_not(lo)
            j = hd % GROUP
            parts.append(jnp.where(keep, q_ref[0, :, j * LANES:(j + 1) * LANES], 0))
        return jnp.concatenate(parts, axis=0)

    cidx = lax.broadcasted_iota(jnp.int32, (Q_TILE, nc_pad), 1)
    m_c = (cidx * CMP_STRIDE + (CMP_BLOCK - 1) <= pos_col) & (cidx < n_cmp)
    jrow = lax.broadcasted_iota(jnp.int32, (n_sel, Q_TILE), 0)
    pos_row = s0 + lax.broadcasted_iota(jnp.int32, (n_sel, Q_TILE), 1)
    cur = pos_row >> 6
    forced = (jrow == 0) | (jrow == cur) | (jrow == cur - 1)
    valid_b = jrow * SEL_BLOCK <= pos_row
    q_sel = head_rows(qr_ref)
    n_win = WINDOW + Q_TILE
    w0 = pl.multiple_of(jnp.maximum(s0 - WINDOW, 0), Q_TILE)
    dpos = pos_col - (w0 + lax.broadcasted_iota(jnp.int32, (Q_TILE, n_win), 1))
    m_w = (dpos >= 0) & (dpos < WINDOW)
    s_c = _dot_nt(head_rows(qn_ref), kc_ref[0]).reshape(N_HEADS, Q_TILE, nc_pad)
    s_w = _dot_nt(q_sel, kw_ref[0, pl.ds(w0, n_win), :]).reshape(N_HEADS, Q_TILE, n_win)
    p = _masked_softmax(s_c, m_c[None], ((pos_col >= CMP_BLOCK - 1) & (n_cmp > 0))[None])
    o_cmp = jnp.dot(p.reshape(N_HEADS * Q_TILE, nc_pad).astype(BF16), vc_ref[0], preferred_element_type=F32)
    imps = [jnp.sum(p[kv * GROUP:(kv + 1) * GROUP], axis=0) for kv in range(N_KV)]
    s_w = jnp.where(m_w[None], s_w, NEG)
    e_w = jnp.exp2(s_w - jnp.max(s_w, axis=-1, keepdims=True))
    o_win = (jnp.dot(e_w.reshape(N_HEADS * Q_TILE, n_win).astype(BF16), vw_ref[0, pl.ds(w0, n_win), :],
                     preferred_element_type=F32)
             * (1.0 / jnp.sum(e_w, axis=-1, keepdims=True)).reshape(N_HEADS * Q_TILE, 1))

    notsel = []
    for kv in range(N_KV):
        imp = imps[kv]
        blk_t = sum(_dot_nt(at_ref[...], part) for part in _split3(imp))
        sel_t = _topk_mask_t(jnp.where(valid_b, blk_t, -FORCED_BONUS), forced, min(N_SELECT, n_sel) - N_FORCED)
        notsel += [(1.0 - sel_t.T).astype(BF16)] * GROUP

    m_ref[...] = jnp.full(m_ref.shape, NEG, F32)
    l_ref[...] = jnp.zeros(l_ref.shape, F32)
    acc_ref[...] = jnp.zeros(acc_ref.shape, F32)
    q_aug = jnp.concatenate([q_sel, jnp.concatenate(notsel, axis=0)], axis=1)

    def tile_step(tix, lane0, width, diagonal):
        lanes = slice(lane0, lane0 + width)
        key0 = pl.multiple_of(tix * KEY_TILE + lane0, width)
        k_aug = jnp.concatenate([kst_ref[0, tix, :, lanes], ebt_ref[tix, :, lanes]], axis=0)
        vt = vs_ref[0, pl.ds(key0, width), :]
        if diagonal:
            key = key0 + lax.broadcasted_iota(jnp.int32, (Q_TILE, width), 1)
            future = jnp.where(key <= pos_col, 0.0, NEG)
        part = N_HEADS * Q_TILE // FLASH_SPLITS
        scores = [jnp.dot(q_aug[h * part:(h + 1) * part], k_aug, preferred_element_type=F32)
                  for h in range(FLASH_SPLITS)]
        for h in range(FLASH_SPLITS):
            rows = slice(h * part, (h + 1) * part)
            s = scores[h]
            if diagonal:
                s = (s.reshape(part // Q_TILE, Q_TILE, width) + future[None]).reshape(part, width)
            m_old = m_ref[rows]
            m_new = jnp.maximum(m_old, jnp.max(s, axis=-1, keepdims=True))
            alpha = jnp.exp2(m_old - m_new)
            p_t = jnp.exp2(s - m_new)
            l_ref[rows] = alpha * l_ref[rows] + jnp.sum(p_t, axis=-1, keepdims=True)
            acc_ref[rows] = alpha * acc_ref[rows] + jnp.dot(p_t.astype(BF16), vt, preferred_element_type=F32)
            m_ref[rows] = m_new

    def tile_body(tix, carry):
        tile_step(tix, 0, KEY_TILE, False)
        return carry

    q_per_key = KEY_TILE // Q_TILE
    last_tile = lax.shift_right_logical(i, q_per_key.bit_length() - 1)
    lax.fori_loop(0, last_tile, tile_body, 0)
    half = KEY_TILE // 2
    in_second_half = (i & (q_per_key - 1)) >= q_per_key // 2

    @pl.when(in_second_half)
    def _():
        tile_step(last_tile, 0, half, False)
        tile_step(last_tile, half, half, True)

    @pl.when(jnp.logical_not(in_second_half))
    def _():
        tile_step(last_tile, 0, half, True)

    o_sel = acc_ref[...] / l_ref[...]
    gt = jax.nn.sigmoid(gate_ref[0])
    o_heads = []
    for hd in range(N_HEADS):
        rows = slice(hd * Q_TILE, (hd + 1) * Q_TILE)
        o_heads.append(_gate_col(gt, hd, 0) * o_cmp[rows] + _gate_col(gt, hd, 1) * o_sel[rows]
                       + _gate_col(gt, hd, 2) * o_win[rows])
    for j in range(GROUP):
        o_ref[0, :, j * LANES:(j + 1) * LANES] = jnp.where(lo, o_heads[j], o_heads[GROUP + j]).astype(BF16)


def _prompt_attn(qn, qr, gate, kc, vc, kst, vs16, kw16, vw16, ebt, a_t, *, n_cmp):
    nb, seq, _ = qn.shape
    qspec = lambda w: pl.BlockSpec((1, Q_TILE, w), lambda b, i: (b, i, 0))
    seqspec = lambda a: pl.BlockSpec((1,) + a.shape[1:], lambda b, i: (b,) + (0,) * (a.ndim - 1))
    full = lambda a: pl.BlockSpec(a.shape, lambda b, i: (0,) * a.ndim)
    rows = N_HEADS * Q_TILE
    return pl.pallas_call(
        functools.partial(_prompt_attn_body, seq=seq, n_cmp=n_cmp),
        grid=(nb, seq // Q_TILE),
        in_specs=[qspec(B_WIDTH), qspec(B_WIDTH), qspec(LANES), seqspec(kc), seqspec(vc),
                  seqspec(kst), seqspec(vs16), seqspec(kw16), seqspec(vw16), full(ebt), full(a_t)],
        out_specs=qspec(B_WIDTH),
        out_shape=jax.ShapeDtypeStruct((nb, seq, B_WIDTH), BF16),
        scratch_shapes=[pltpu.VMEM((rows, 1), F32), pltpu.VMEM((rows, 1), F32), pltpu.VMEM((rows, LANES), F32)],
        compiler_params=pltpu.CompilerParams(dimension_semantics=("arbitrary", "arbitrary"),
                                             vmem_limit_bytes=VMEM_LIMIT),
        name="prompt_attn")(qn, qr, gate, kc, vc, kst, vs16, kw16, vw16, ebt, a_t)


TOK_PAD = 8
ROWS = N_HEADS * TOK_PAD
ROWS_PAD = 128


def _page_copy_lanes(pool_ref, page, buf_ref, slot, i, sem_ref):
    return pltpu.make_async_copy(pool_ref.at[page], buf_ref.at[slot, :, pl.ds(i * PAGE_SIZE, PAGE_SIZE)],
                                 sem_ref.at[slot])


def _sample_attn_body(pt_ref, qn_ref, qr_ref, gate_ref, kc_ref, vc_ref, kwc_ref, vwc_ref,
                      ksn_ref, vsn_ref, kwn_ref, vwn_ref, at_ref, gsum_ref, ebt_ref, kpool_ref, vpool_ref,
                      o_ref, kbuf_ref, vbuf_ref, ksem_ref, vsem_ref, notsel_ref, m_ref, l_ref, acc_ref, ocw_ref,
                      q_ref, *, n_chunks, pages_per_chunk, past, n_new, n_cmp):
    b = pl.program_id(0)
    c = pl.program_id(1)
    step = b * n_chunks + c
    n_steps = pl.num_programs(0) * n_chunks
    n_past_blk = past // SEL_BLOCK
    blk_per_chunk = n_past_blk // n_chunks

    def fetch(st, slot):
        def one(i, carry):
            page = pt_ref[st * pages_per_chunk + i]
            _page_copy_lanes(kpool_ref, page, kbuf_ref, slot, i, ksem_ref).start()
            _page_copy_lanes(vpool_ref, page, vbuf_ref, slot, i, vsem_ref).start()
            return carry
        lax.fori_loop(0, pages_per_chunk, one, 0, unroll=8)

    @pl.when(step == 0)
    def _():
        fetch(0, 0)

    @pl.when(step + 1 < n_steps)
    def _():
        fetch(step + 1, (step + 1) % 2)

    lo = _low_half((TOK_PAD, LANES))
    tok_col = lax.broadcasted_iota(jnp.int32, (ROWS, 1), 0) % TOK_PAD
    pos_col = past + tok_col

    def rows_q(ref):
        parts = []
        for hd in range(N_HEADS):
            keep = lo if hd < GROUP else jnp.logical_not(lo)
            j = hd % GROUP
            parts.append(jnp.where(keep, ref[0, :, j * LANES:(j + 1) * LANES], 0))
        return jnp.concatenate(parts, axis=0)

    def gate_rows(gt, branch):
        return jnp.concatenate([_gate_col(gt, hd, branch) for hd in range(N_HEADS)], axis=0)

    @pl.when(c == 0)
    def _():
        gt = jax.nn.sigmoid(gate_ref[0])
        nc_pad = kc_ref.shape[1]
        cidx = lax.broadcasted_iota(jnp.int32, (ROWS, nc_pad), 1)
        m_c = (cidx * CMP_STRIDE + (CMP_BLOCK - 1) <= pos_col) & (cidx < n_cmp)
        p = _masked_softmax(_dot_nt(rows_q(qn_ref), kc_ref[0]), m_c, (pos_col >= CMP_BLOCK - 1) & (n_cmp > 0))
        o_c = jnp.dot(p.astype(BF16), vc_ref[0], preferred_element_type=F32)
        imp = sum(jnp.dot(gsum_ref[...], part, preferred_element_type=F32) for part in _split3(p))
        blk_t = sum(_dot_nt(at_ref[...], part) for part in _split3(imp))
        jrow = lax.broadcasted_iota(jnp.int32, blk_t.shape, 0)
        n_top = min(N_SELECT, n_past_blk + 1)
        sel_t = _topk_mask_t(blk_t, (jrow == 0) | (jrow == n_past_blk - 1), n_top - N_FORCED)
        for cc in range(n_chunks):
            blk = sel_t[cc * blk_per_chunk:(cc + 1) * blk_per_chunk]
            notsel_ref[cc] = (1.0 - blk.T[:ROWS]).astype(BF16)
        q_r = rows_q(qr_ref)
        q_ref[...] = q_r
        n_buf = kwc_ref.shape[2]
        kidx = lax.broadcasted_iota(jnp.int32, (ROWS, n_buf), 1)
        dpos = pos_col - (past - n_buf + kidx)
        m_old = (dpos >= 0) & (dpos < WINDOW)
        knew = lax.broadcasted_iota(jnp.int32, (ROWS, TOK_PAD), 1)
        m_new = (knew <= tok_col) & (knew < n_new)
        s_old = jnp.where(m_old, jnp.dot(q_r, kwc_ref[0].astype(BF16), preferred_element_type=F32), NEG)
        s_new = jnp.where(m_new, _dot_nt(q_r, kwn_ref[0]), NEG)
        mx = jnp.maximum(jnp.max(s_old, axis=-1, keepdims=True), jnp.max(s_new, axis=-1, keepdims=True))
        e_old = jnp.exp2(s_old - mx)
        e_new = jnp.exp2(s_new - mx)
        den = jnp.sum(e_old, axis=-1, keepdims=True) + jnp.sum(e_new, axis=-1, keepdims=True)
        p_old = e_old / den * m_old.astype(F32)
        p_new = e_new / den * m_new.astype(F32)
        o_w = (_dot_nt(p_old.astype(BF16), vwc_ref[0].astype(BF16))
               + jnp.dot(p_new.astype(BF16), vwn_ref[0], preferred_element_type=F32))
        ocw_ref[...] = gate_rows(gt, 0) * o_c + gate_rows(gt, 2) * o_w
        s = jnp.where(m_new, _dot_nt(q_r, ksn_ref[0]), NEG)
        m0 = jnp.max(s, axis=-1, keepdims=True)
        e = jnp.exp2(s - m0) * m_new.astype(F32)
        m_ref[...] = m0
        l_ref[...] = jnp.sum(e, axis=-1, keepdims=True)
        acc_ref[...] = jnp.dot(e.astype(BF16), vsn_ref[0], preferred_element_type=F32)

    slot = step % 2

    pltpu.make_async_copy(kbuf_ref.at[slot], kbuf_ref.at[slot], ksem_ref.at[slot]).wait()
    pltpu.make_async_copy(vbuf_ref.at[slot], vbuf_ref.at[slot], vsem_ref.at[slot]).wait()

    q_aug = jnp.concatenate([q_ref[...], notsel_ref[c]], axis=1)
    k_aug = jnp.concatenate([kbuf_ref[slot].astype(BF16), ebt_ref[...]], axis=0)
    s = jnp.dot(q_aug, k_aug, preferred_element_type=F32)
    m_old = m_ref[...]
    m_new = jnp.maximum(m_old, jnp.max(s, axis=-1, keepdims=True))
    alpha = jnp.exp2(m_old - m_new)
    p = jnp.exp2(s - m_new)
    l_ref[...] = alpha * l_ref[...] + jnp.sum(p, axis=-1, keepdims=True)
    acc_ref[...] = alpha * acc_ref[...] + _dot_nt(p.astype(BF16), vbuf_ref[slot].astype(BF16))
    m_ref[...] = m_new

    @pl.when(c == n_chunks - 1)
    def _():
        gt = jax.nn.sigmoid(gate_ref[0])
        o = ocw_ref[...] + gate_rows(gt, 1) * (acc_ref[...] / l_ref[...])
        for j in range(GROUP):
            top = o[j * TOK_PAD:(j + 1) * TOK_PAD]
            bot = o[(GROUP + j) * TOK_PAD:(GROUP + j + 1) * TOK_PAD]
            o_ref[0, :, j * LANES:(j + 1) * LANES] = jnp.where(lo, top, bot).astype(BF16)


def _sample_attn(pt_flat, qn, qr, gate, kc, vc, kwc, vwc, ksn, vsn, kwn, vwn, a_t, gsum, ebt, kpool, vpool,
                 *, n_chunks, pages_per_chunk, past, n_new, n_cmp):
    nb = qn.shape[0]
    chunk_keys = pages_per_chunk * PAGE_SIZE
    blk_per_chunk = chunk_keys // SEL_BLOCK
    bspec = lambda a: pl.BlockSpec((1,) + a.shape[1:], lambda b, c, pt: (b, 0, 0))
    full = lambda a: pl.BlockSpec(a.shape, lambda b, c, pt: (0,) * a.ndim)
    anyspec = pl.BlockSpec(memory_space=pl.ANY)
    blocked = [qn, qr, gate, kc, vc, kwc, vwc, ksn, vsn, kwn, vwn]
    grid_spec = pltpu.PrefetchScalarGridSpec(
        num_scalar_prefetch=1, grid=(nb, n_chunks),
        in_specs=[bspec(a) for a in blocked] + [full(a_t), full(gsum), full(ebt), anyspec, anyspec],
        out_specs=pl.BlockSpec((1, TOK_PAD, B_WIDTH), lambda b, c, pt: (b, 0, 0)),
        scratch_shapes=[pltpu.VMEM((2, KV_WIDTH, chunk_keys), F32), pltpu.VMEM((2, KV_WIDTH, chunk_keys), F32),
                        pltpu.SemaphoreType.DMA((2,)), pltpu.SemaphoreType.DMA((2,)),
                        pltpu.VMEM((n_chunks, ROWS, blk_per_chunk), BF16),
                        pltpu.VMEM((ROWS, 1), F32), pltpu.VMEM((ROWS, 1), F32), pltpu.VMEM((ROWS, LANES), F32),
                        pltpu.VMEM((ROWS, LANES), F32), pltpu.VMEM((ROWS, LANES), BF16)])
    return pl.pallas_call(
        functools.partial(_sample_attn_body, n_chunks=n_chunks, pages_per_chunk=pages_per_chunk, past=past,
                          n_new=n_new, n_cmp=n_cmp),
        grid_spec=grid_spec, out_shape=jax.ShapeDtypeStruct((nb, TOK_PAD, B_WIDTH), BF16),
        compiler_params=pltpu.CompilerParams(dimension_semantics=("arbitrary", "arbitrary"),
                                             vmem_limit_bytes=VMEM_LIMIT),
        name="sample_attn")(pt_flat, *blocked, a_t, gsum, ebt, kpool, vpool)


FF_TILE = 1024


def _merge_ffn_body(x_ref, oa_ref, ob_ref, g1_ref, wmg_ref, wba_ref, wbb_ref, wout_ref, g2_ref, wup_ref, wdn_ref,
                    y_ref):
    x = x_ref[...]
    h = _rms_rows(x, g1_ref[...]).astype(BF16)
    y_a = jnp.dot(oa_ref[...], wba_ref[...], preferred_element_type=F32)
    y_b = jnp.dot(ob_ref[...], wbb_ref[...], preferred_element_type=F32)
    g_a = jax.nn.sigmoid(jnp.dot(h, wmg_ref[:, :D_MODEL], preferred_element_type=F32))
    g_b = jax.nn.sigmoid(jnp.dot(h, wmg_ref[:, D_MODEL:], preferred_element_type=F32))
    x1 = x + jnp.dot((g_a * y_a + g_b * y_b).astype(BF16), wout_ref[...], preferred_element_type=F32)
    h2 = _rms_rows(x1, g2_ref[...]).astype(BF16)
    y = x1
    for f in range(D_FF // FF_TILE):
        up = jnp.dot(h2, wup_ref[:, f * FF_TILE:(f + 1) * FF_TILE], preferred_element_type=F32)
        act = jnp.square(jnp.maximum(up, 0.0)).astype(BF16)
        y = y + jnp.dot(act, wdn_ref[f * FF_TILE:(f + 1) * FF_TILE, :], preferred_element_type=F32)
    y_ref[...] = y


def _merge_ffn(x, o_a, o_b, wts, *, tm):
    n = x.shape[0]
    row = lambda w: pl.BlockSpec((tm, w), lambda i: (i, 0))
    const = lambda a: pl.BlockSpec(a.shape, lambda i: (0,) * a.ndim, pipeline_mode=pl.Buffered(1))
    ws = [wts['g1'], wts['w_mg'], wts['w_br_a'], wts['w_br_b'], wts['w_out'], wts['g2'], wts['w_up'], wts['w_down']]
    return pl.pallas_call(
        _merge_ffn_body, grid=(n // tm,),
        in_specs=[row(D_MODEL), row(A_WIDTH), row(B_WIDTH)] + [const(a) for a in ws],
        out_specs=row(D_MODEL), out_shape=jax.ShapeDtypeStruct((n, D_MODEL), F32),
        compiler_params=pltpu.CompilerParams(dimension_semantics=("arbitrary",), vmem_limit_bytes=VMEM_LIMIT),
        name="merge_ffn")(x, o_a, o_b, *ws)


def _rope_tables(pos):
    half = ROT_DIM // 2
    lane = jnp.arange(LANES) % HEAD_DIM
    inv = ROPE_THETA ** (-(2 * (lane % half)).astype(F32) / ROT_DIM)
    ang = pos.astype(F32)[:, None] * inv[None, :]
    cos, sin = jnp.cos(ang), jnp.sin(ang)
    cos_t = jnp.where(lane < ROT_DIM, cos, 1.0)
    sin_up = jnp.where((lane >= half) & (lane < ROT_DIM), sin, 0.0)
    sin_dn = jnp.where(lane < half, -sin, 0.0)
    return cos_t, sin_up, sin_dn


def _two(g):
    return jnp.concatenate([g, g])[None, :].astype(F32)


def _layer_weights(l, g_norm1, w_in, ln_v_g, ln_v_b, g_q, g_ks, g_kw, w_branch, w_out, g_norm2, w_up, w_down):
    w = w_in[l]
    q0, k0 = 2 * A_WIDTH, 2 * A_WIDTH + B_WIDTH
    g0 = k0 + 6 * KV_WIDTH
    perm = jnp.array(HEAD_PERM)
    q_cols = w[:, q0:k0].reshape(D_MODEL, N_HEADS, HEAD_DIM)[:, perm].reshape(D_MODEL, B_WIDTH)
    gate_cols = jnp.pad(w[:, g0:g0 + 3 * N_HEADS], ((0, 0), (0, LANES - 3 * N_HEADS)))
    w_a = jnp.concatenate([w[:, :q0], q_cols, w[:, k0:g0], gate_cols], axis=1).astype(BF16)
    wb = w_branch[l]
    w_br_b = wb[A_WIDTH:].reshape(N_HEADS, HEAD_DIM, D_MODEL)[perm].reshape(B_WIDTH, D_MODEL)
    return dict(
        g1=g_norm1[l][None, :], w_a=w_a, ln_g=ln_v_g[l][None, :], ln_b=ln_v_b[l][None, :],
        g_q=_two(g_q[l]), g_ks=_two(g_ks[l]), g_kw=_two(g_kw[l]),
        w_mg=w[:, g0 + 3 * N_HEADS:].astype(BF16), w_br_a=wb[:A_WIDTH].astype(BF16), w_br_b=w_br_b.astype(BF16),
        w_out=w_out[l].astype(BF16), g2=g_norm2[l][None, :], w_up=w_up[l].astype(BF16),
        w_down=w_down[l].astype(BF16))


def _compress_weights(w1, w2, pe, g):
    half = CMP_STRIDE * HEAD_DIM
    eye = jnp.eye(N_KV, dtype=F32)

    def spread(wh):
        return jnp.einsum('rde,kl->rkdle', wh.reshape(CMP_STRIDE, HEAD_DIM, HEAD_DIM), eye).reshape(
            CHUNK_LANES, KV_WIDTH)

    w1ab = jnp.concatenate([spread(w1[:half]), spread(w1[half:])], axis=1).astype(BF16)
    pe_rows = jnp.zeros((8, CMP_BLOCK * HEAD_DIM), F32).at[0].set(pe.reshape(-1)).astype(BF16)
    w1d = jnp.concatenate([w1, w1], axis=1).astype(BF16)
    w2d = jnp.einsum('de,kl->kdle', w2, eye).reshape(KV_WIDTH, KV_WIDTH).astype(BF16)
    return dict(w1ab=w1ab, pe=pe_rows, w1d=w1d, w2d=w2d, g=_two(g))


def _block_score_matrix(n_blk, n_chunk):
    j = jnp.arange(n_blk)[:, None]
    c = jnp.arange(n_chunk)[None, :]
    per = SEL_BLOCK // CMP_STRIDE
    a = ((c >= per * j) & (c <= per * j + per - 1)).astype(F32) + ((c >= per * j - 1) & (c <= per * j + per - 2))
    return a.astype(BF16)


def kernel(x_prompt, x_sample, cache_k_cmp, cache_v_cmp, cache_k_sel, cache_v_sel, cache_k_win, cache_v_win,
           page_table, g_norm1, w_in, ln_v_g, ln_v_b, w_s, b_s, g_q, g_kc, g_ks, g_kw, w_ck1, w_ck2, pe_k,
           w_cv1, w_cv2, pe_v, w_branch, w_out, g_norm2, w_up, w_down):
    nb, seq, _ = x_prompt.shape
    db, dseq, _ = x_sample.shape
    depth = w_in.shape[0]
    n_pages = page_table.shape[1]
    past = n_pages * PAGE_SIZE
    assert depth == 1 and seq % KEY_TILE == 0 and seq >= WINDOW + Q_TILE and dseq <= TOK_PAD
    assert past % (SEL_BLOCK * LANES) == 0 and (past + dseq) // CMP_STRIDE == past // CMP_STRIDE
    l = 0
    wts = _layer_weights(l, g_norm1, w_in, ln_v_g, ln_v_b, g_q, g_ks, g_kw, w_branch, w_out, g_norm2, w_up, w_down)
    cw_k = _compress_weights(w_ck1[l], w_ck2[l], pe_k[l], g_kc[l])
    cw_v = _compress_weights(w_cv1[l], w_cv2[l], pe_v[l], g_kc[l])
    causal = jnp.tril(jnp.ones((CHUNK, CHUNK), dtype=bool))
    ws_tril = jnp.where(causal[None], w_s[l], 0.0)

    n_p = nb * seq
    xp = x_prompt.reshape(n_p, D_MODEL)
    tabs_p = _rope_tables(jnp.arange(seq, dtype=jnp.int32))
    bmix_p = jnp.repeat(b_s[l].T, A_WIDTH // A_GROUPS, axis=1)
    (oa_p, vlast_p, qn_p, qr_p, kc_p, vc_p, ks_p, vs_p, kw_p, vw_p, kc16, vc16, ks16, vs16, kw16, vw16,
     gate_p) = _in_proj(
        xp, wts, tabs_p, ws_tril.astype(BF16), bmix_p, tm=512, seg_rows=seq)
    n_chunk_p = seq // CMP_STRIDE
    kcmp_p = _compress(kc16.reshape(nb, n_chunk_p, CHUNK_LANES), cw_k, norm=True)
    vcmp_p = _compress(vc16.reshape(nb, n_chunk_p, CHUNK_LANES), cw_v, norm=False)
    n_sel_p = seq // SEL_BLOCK
    n_tiles = seq // KEY_TILE
    kblk = (jnp.arange(seq) // SEL_BLOCK).reshape(n_tiles, 1, KEY_TILE)
    ebt = jnp.where(kblk == jnp.arange(n_sel_p)[None, :, None], MASK_BIAS, 0.0).astype(BF16)
    a_t_p = _block_score_matrix(n_sel_p, n_chunk_p)
    b3 = lambda a: a.reshape(nb, seq, a.shape[-1])
    kst = ks16.reshape(nb, n_tiles, KEY_TILE, KV_WIDTH).transpose(0, 1, 3, 2)
    ob_p = _prompt_attn(b3(qn_p), b3(qr_p), b3(gate_p), kcmp_p, vcmp_p, kst, b3(vs16), b3(kw16), b3(vw16),
                        ebt, a_t_p, n_cmp=n_chunk_p - 1)
    y_p = _merge_ffn(xp, oa_p, ob_p.reshape(n_p, B_WIDTH), wts, tm=512).reshape(nb, seq, D_MODEL)

    n_s = db * dseq
    xs = x_sample.reshape(n_s, D_MODEL)
    tabs_s = _rope_tables(jnp.tile(past + jnp.arange(dseq, dtype=jnp.int32), db))
    wmix_s = jnp.einsum('gpr,bc->gbpcr', ws_tril[:, :dseq, :dseq], jnp.eye(db, dtype=F32)).reshape(
        A_GROUPS, n_s, n_s)
    bmix_s = jnp.tile(bmix_p[:dseq], (db, 1))
    (oa_s, v_s, qn_s, qr_s, kc_s, vc_s, ks_s, vs_s, kw_s, vw_s, _, _, ks16s, vs16s, kw16s, vw16s,
     gate_s) = _in_proj(
        xs, wts, tabs_s, wmix_s.astype(BF16), bmix_s, tm=n_s, seg_rows=n_s)
    pt_flat = page_table.reshape(-1)
    keys_last = lambda a: a.transpose(0, 2, 3, 1).reshape(a.shape[0], KV_WIDTH, a.shape[1])
    kcmp_s = _paged_compress(pt_flat, keys_last(cache_k_cmp[l]), cw_k, nb=db, n_pages=n_pages, norm=True)
    vcmp_s = _paged_compress(pt_flat, keys_last(cache_v_cmp[l]), cw_v, nb=db, n_pages=n_pages, norm=False)
    n_chunk_s = past // CMP_STRIDE
    tokpad = lambda a: jnp.pad(a.reshape(db, dseq, a.shape[-1]), ((0, 0), (0, TOK_PAD - dseq), (0, 0)))
    hh = jnp.arange(ROWS_PAD) // TOK_PAD
    tt = jnp.arange(ROWS_PAD) % TOK_PAD
    gsum = ((hh[:, None] // GROUP == hh[None, :ROWS] // GROUP) & (tt[:, None] == tt[None, :ROWS])
            & (hh[:, None] < N_HEADS)).astype(BF16)
    n_chunks = 2
    chunk_keys = past // n_chunks
    ebt = jnp.where(jnp.arange(chunk_keys)[None, :] // SEL_BLOCK == jnp.arange(chunk_keys // SEL_BLOCK)[:, None],
                    MASK_BIAS, 0.0).astype(BF16)
    ob_s = _sample_attn(
        pt_flat, tokpad(qn_s), tokpad(qr_s), tokpad(gate_s), kcmp_s, vcmp_s,
        keys_last(cache_k_win[l]), keys_last(cache_v_win[l]),
        tokpad(ks16s), tokpad(vs16s), tokpad(kw16s), tokpad(vw16s),
        _block_score_matrix(past // SEL_BLOCK, n_chunk_s), gsum, ebt,
        keys_last(cache_k_sel[l]), keys_last(cache_v_sel[l]),
        n_chunks=n_chunks, pages_per_chunk=n_pages // n_chunks, past=past, n_new=dseq, n_cmp=n_chunk_s - 1)
    y_s = _merge_ffn(xs, oa_s, ob_s[:, :dseq].reshape(n_s, B_WIDTH), wts, tm=n_s).reshape(db, dseq, D_MODEL)

    w_p = min(WINDOW, seq)
    pk = lambda a: a.reshape(1, nb, N_KV, HEAD_DIM, seq).transpose(0, 1, 4, 2, 3)
    sk = lambda a: a[0].T.reshape(1, db, dseq, N_KV, HEAD_DIM)
    return (y_p, y_s, pk(kc_p), pk(vc_p), pk(ks_p), pk(vs_p), pk(kw_p)[:, :, seq - w_p:], pk(vw_p)[:, :, seq - w_p:],
            vlast_p[None], sk(kc_s), sk(vc_s), sk(ks_s), sk(vs_s), sk(kw_s), sk(vw_s),
            v_s.reshape(db, dseq, A_WIDTH)[None])
```
